```python
import math
import jax
import jax.numpy as jnp
from jax import lax
import numpy as np

D_MODEL = 1024
BATCH = 16
SEQ = 2048
DEPTH = 1
DEC_BATCH = 128
DEC_SEQ = 4
PAST_LEN = 8192
PAGE_SIZE = 128

MLA_HEADS = 8
MLA_NOPE = 64
MLA_ROPE = 32
MLA_V = 64
Q_LORA = 384
KV_LORA = 256
MLA_SCALE = (MLA_NOPE + MLA_ROPE) ** -0.5
DSA_HEADS = 8
DSA_KV_HEADS = 2
DSA_HEAD_DIM = 64
DSA_SCALE = DSA_HEAD_DIM ** -0.5
IDX_HEADS = 8
IDX_DIM = 64
IDX_ROPE = 32
IDX_TOPK_MAX = 256
IDX_W_SCALE = (IDX_HEADS * IDX_DIM) ** -0.5
REL_BUCKETS = 32
REL_MAX_DIST = 128
PEER_HEADS = 8
PEER_NKEYS = 128
PEER_EXPERTS = PEER_NKEYS * PEER_NKEYS
PEER_QDIM = 256
PEER_HALF = PEER_QDIM // 2
PEER_TOPK = 16
PEER_CHUNK = 256

ROPE_THETA = 10000.0
NORM_EPS = 1e-6
Q_BLOCK = 128

SPLITS = (Q_LORA, KV_LORA, MLA_ROPE, DSA_HEADS * DSA_HEAD_DIM, DSA_KV_HEADS * DSA_HEAD_DIM,
          DSA_KV_HEADS * DSA_HEAD_DIM, IDX_HEADS * IDX_DIM, IDX_DIM, IDX_HEADS)
D_IN = sum(SPLITS)
MIX_WIDTH = MLA_HEADS * MLA_V + DSA_HEADS * DSA_HEAD_DIM

kernel_name = 'hybrid_mla_dsa_peer_step'


def rmsnorm(x, g):
    xf = x.astype(jnp.float32)
    y = xf * lax.rsqrt(jnp.mean(xf * xf, axis=-1, keepdims=True) + NORM_EPS)
    return y.astype(x.dtype) * g


def rope(x, pos):
    half = x.shape[-1] // 2
    inv = ROPE_THETA ** (-jnp.arange(half, dtype=jnp.float32) / half)
    ang = pos.astype(jnp.float32)[:, None, None] * inv
    cos, sin = jnp.cos(ang), jnp.sin(ang)
    xf = x.astype(jnp.float32)
    x1, x2 = xf[..., :half], xf[..., half:]
    return jnp.concatenate([x1 * cos - x2 * sin, x2 * cos + x1 * sin], axis=-1).astype(x.dtype)


def rel_bucket(dist):
    max_exact = REL_BUCKETS // 2
    n = jnp.maximum(dist, 0)
    nf = jnp.maximum(n, 1).astype(jnp.float32)
    large = max_exact + (jnp.log(nf / max_exact) / math.log(REL_MAX_DIST / max_exact)
                         * (REL_BUCKETS - max_exact)).astype(jnp.int32)
    large = jnp.minimum(large, REL_BUCKETS - 1)
    return jnp.where(n < max_exact, n, large)


def to_blocks(a):
    b, s = a.shape[:2]
    return a.reshape((b, s // Q_BLOCK, Q_BLOCK) + a.shape[2:]).swapaxes(0, 1)


def from_blocks(a):
    nb, b, qb = a.shape[:3]
    return a.swapaxes(0, 1).reshape((b, nb * qb) + a.shape[3:])


def take_rows(rows, idx):
    return jax.vmap(lambda r, i: r[i])(rows, idx)


def mixer_inputs(xn, pos, w_in, g_q, w_uq, g_kv, w_uk):
    b, s, _ = xn.shape
    cuts = tuple(int(c) for c in np.cumsum(SPLITS)[:-1])
    c_q, c_kv, k_r, q_b, k_b, v_b, q_i, k_i, w_i = jnp.split(xn @ w_in, cuts, axis=-1)
    c_q = rmsnorm(c_q, g_q)
    q = (c_q @ w_uq).reshape(b, s, MLA_HEADS, MLA_NOPE + MLA_ROPE)
    q_lat = jnp.einsum('bshn,chn->bshc', q[..., :MLA_NOPE], w_uk)
    q_rope = rope(q[..., MLA_NOPE:], pos)
    c_kv = rmsnorm(c_kv, g_kv)
    k_rope = rope(k_r[:, :, None, :], pos)[:, :, 0]
    q_b = q_b.reshape(b, s, DSA_KV_HEADS, DSA_HEADS // DSA_KV_HEADS, DSA_HEAD_DIM)
    k_b = k_b.reshape(b, s, DSA_KV_HEADS, DSA_HEAD_DIM)
    v_b = v_b.reshape(b, s, DSA_KV_HEADS, DSA_HEAD_DIM)
    q_i = q_i.reshape(b, s, IDX_HEADS, IDX_DIM)
    q_i = jnp.concatenate([rope(q_i[..., :IDX_ROPE], pos), q_i[..., IDX_ROPE:]], axis=-1)
    k_i = jnp.concatenate([rope(k_i[:, :, None, :IDX_ROPE], pos)[:, :, 0], k_i[..., IDX_ROPE:]], axis=-1)
    w_i = w_i * IDX_W_SCALE
    return q_lat, q_rope, c_kv, k_rope, q_b, k_b, v_b, q_i, k_i, w_i


def mla_logits(q_lat, q_rope, ckv, krope):
    s = jnp.einsum('bthc,bsc->bhts', q_lat, ckv) + jnp.einsum('bthr,bsr->bhts', q_rope, krope)
    return s.astype(jnp.float32) * MLA_SCALE


def mla_out(o_lat, w_uv):
    b, t = o_lat.shape[:2]
    return jnp.einsum('bthc,chv->bthv', o_lat, w_uv).reshape(b, t, MLA_HEADS * MLA_V)


def mla_prompt(q_lat, q_rope, c_kv, k_rope, w_uv):
    b, s = c_kv.shape[:2]
    kpos = jnp.arange(s)

    def block(args):
        ql, qr, qpos = args
        logits = mla_logits(ql, qr, c_kv, k_rope)
        logits = jnp.where((kpos[None, :] <= qpos[:, None])[None, None], logits, -jnp.inf)
        p = jax.nn.softmax(logits, axis=-1).astype(c_kv.dtype)
        return jnp.einsum('bhts,bsc->bthc', p, c_kv)

    qpos = jnp.arange(s).reshape(s // Q_BLOCK, Q_BLOCK)
    o_lat = from_blocks(lax.map(block, (to_blocks(q_lat), to_blocks(q_rope), qpos)))
    return mla_out(o_lat, w_uv)


def mla_sample(q_lat, q_rope, c_kv, k_rope, w_uv, cache_ckv, cache_krope, layer, page_table):
    b, t = c_kv.shape[:2]
    past = page_table.shape[1] * PAGE_SIZE
    ckv_past = cache_ckv[layer, page_table].reshape(b, past, KV_LORA)
    kr_past = cache_krope[layer, page_table].reshape(b, past, MLA_ROPE)
    causal = jnp.arange(t)[None, :] <= jnp.arange(t)[:, None]
    logits = jnp.concatenate(
        [mla_logits(q_lat, q_rope, ckv_past, kr_past),
         jnp.where(causal[None, None], mla_logits(q_lat, q_rope, c_kv, k_rope), -jnp.inf)], axis=-1)
    p = jax.nn.softmax(logits, axis=-1).astype(c_kv.dtype)
    o_lat = (jnp.einsum('bhts,bsc->bthc', p[..., :past], ckv_past)
             + jnp.einsum('bhts,bsc->bthc', p[..., past:], c_kv))
    return mla_out(o_lat, w_uv)


def indexer_scores(q_i, w_i, k_i):
    a = jax.nn.relu(jnp.einsum('bthd,bsd->bths', q_i, k_i).astype(jnp.float32))
    return jnp.einsum('bth,bths->bts', w_i.astype(jnp.float32), a)


def dsa_attend(q, k_sel, v_sel, s_sel, qpos, rel_bias):
    b, t, g, r, d = q.shape
    dist = qpos[None, :, None] - s_sel
    bias = rel_bias[rel_bucket(dist)].astype(jnp.float32)
    bias = bias.transpose(0, 1, 3, 2).reshape(b, t, g, r, -1)
    logits = jnp.einsum('btgrd,btkgd->btgrk', q, k_sel).astype(jnp.float32) * DSA_SCALE + bias
    logits = jnp.where((dist >= 0)[:, :, None, None, :], logits, -jnp.inf)
    p = jax.nn.softmax(logits, axis=-1).astype(v_sel.dtype)
    return jnp.einsum('btgrk,btkgd->btgrd', p, v_sel).reshape(b, t, g * r * d)


def dsa_prompt(q_b, k_b, v_b, q_i, k_i, w_i, rel_bias):
    b, s = k_b.shape[:2]
    topk = min(IDX_TOPK_MAX, s // 4)
    kpos = jnp.arange(s)

    def block(args):
        qb, qib, wib, qpos = args
        sc = indexer_scores(qib, wib, k_i)
        sc = jnp.where((kpos[None, :] <= qpos[:, None])[None], sc, -jnp.inf)
        _, idx = lax.top_k(sc, topk)
        return dsa_attend(qb, take_rows(k_b, idx), take_rows(v_b, idx), idx, qpos, rel_bias)

    qpos = jnp.arange(s).reshape(s // Q_BLOCK, Q_BLOCK)
    return from_blocks(lax.map(block, (to_blocks(q_b), to_blocks(q_i), to_blocks(w_i), qpos)))


def gather_rows(pool, layer, new_rows, page_table, idx):
    b, t, k = idx.shape
    past = page_table.shape[1] * PAGE_SIZE
    ip = jnp.minimum(idx, past - 1)
    phys = jnp.take_along_axis(page_table, (ip // PAGE_SIZE).reshape(b, t * k), axis=1).reshape(b, t, k)
    from_pool = pool[layer, phys, ip % PAGE_SIZE]
    from_new = take_rows(new_rows, jnp.clip(idx - past, 0, new_rows.shape[1] - 1))
    is_past = (idx < past).reshape(idx.shape + (1,) * (from_pool.ndim - 3))
    return jnp.where(is_past, from_pool, from_new)


def dsa_sample(q_b, k_b, v_b, q_i, k_i, w_i, cache_k, cache_v, cache_kidx, layer, page_table, qpos, rel_bias):
    b, t = k_b.shape[:2]
    past = page_table.shape[1] * PAGE_SIZE
    topk = min(IDX_TOPK_MAX, (past + t) // 4)
    kidx_past = cache_kidx[layer, page_table].reshape(b, past, IDX_DIM)
    causal = jnp.arange(t)[None, :] <= jnp.arange(t)[:, None]
    sc = jnp.concatenate([indexer_scores(q_i, w_i, kidx_past),
                          jnp.where(causal[None], indexer_scores(q_i, w_i, k_i), -jnp.inf)], axis=-1)
    _, idx = lax.top_k(sc, topk)
    k_sel = gather_rows(cache_k, layer, k_b, page_table, idx)
    v_sel = gather_rows(cache_v, layer, v_b, page_table, idx)
    return dsa_attend(q_b, k_sel, v_sel, idx, qpos, rel_bias)


def peer_ffn(x, w_pq, sub_keys, u_emb, v_emb):
    n, d = x.shape
    xp = jnp.pad(x, ((0, -n % PEER_CHUNK), (0, 0))).reshape(-1, PEER_CHUNK, d)

    def chunk(xc):
        c = xc.shape[0]
        q = (xc @ w_pq).reshape(c, PEER_HEADS, 2, PEER_HALF)
        s = jnp.einsum('chpd,hpkd->chpk', q, sub_keys).astype(jnp.float32)
        sv, si = lax.top_k(s, PEER_TOPK)
        cand = (sv[:, :, 0, :, None] + sv[:, :, 1, None, :]).reshape(c, PEER_HEADS, -1)
        cid = (si[:, :, 0, :, None] * PEER_NKEYS + si[:, :, 1, None, :]).reshape(c, PEER_HEADS, -1)
        cv, ci = lax.top_k(cand, PEER_TOPK)
        eid = jnp.take_along_axis(cid, ci, axis=-1)
        gate = jax.nn.softmax(cv, axis=-1)
        act = jax.nn.gelu(jnp.einsum('chkd,cd->chk', u_emb[eid], xc), approximate=False)
        return jnp.einsum('chk,chkd->cd', (gate * act.astype(jnp.float32)).astype(xc.dtype), v_emb[eid])

    return lax.map(chunk, xp).reshape(-1, d)[:n]


def peer_block(h, g_ffn, w_pq, sub_keys, u_emb, v_emb):
    xn = rmsnorm(h, g_ffn)
    return h + peer_ffn(xn.reshape(-1, h.shape[-1]), w_pq, sub_keys, u_emb, v_emb).reshape(h.shape)


def setup_inputs(seed: int = 0) -> dict:
    key = jax.random.key(seed)
    ks = jax.random.split(key, 24)
    f32 = jnp.float32
    n_pages = PAST_LEN // PAGE_SIZE
    n_used = DEC_BATCH * n_pages
    n_pool = n_used + n_used // 4

    def nrm(k, shape, scale=1.0):
        return jax.random.normal(k, shape, f32) * scale

    def gain(k, shape):
        return 1.0 + 0.02 * jax.random.normal(k, shape, f32)

    page_table = jax.random.permutation(ks[7], n_pool)[:n_used].reshape(DEC_BATCH, n_pages).astype(jnp.int32)
    return {
        'x_prompt': nrm(ks[0], (BATCH, SEQ, D_MODEL)),
        'x_sample': nrm(ks[1], (DEC_BATCH, DEC_SEQ, D_MODEL)),
        'cache_ckv': nrm(ks[2], (DEPTH, n_pool, PAGE_SIZE, KV_LORA)),
        'cache_krope': nrm(ks[3], (DEPTH, n_pool, PAGE_SIZE, MLA_ROPE)),
        'cache_k': nrm(ks[4], (DEPTH, n_pool, PAGE_SIZE, DSA_KV_HEADS, DSA_HEAD_DIM)),
        'cache_v': nrm(ks[5], (DEPTH, n_pool, PAGE_SIZE, DSA_KV_HEADS, DSA_HEAD_DIM)),
        'cache_kidx': nrm(ks[6], (DEPTH, n_pool, PAGE_SIZE, IDX_DIM)),
        'page_table': page_table,
        'rel_bias': nrm(ks[8], (REL_BUCKETS, DSA_HEADS), 0.5),
        'g_attn': gain(ks[9], (DEPTH, D_MODEL)),
        'w_in': nrm(ks[10], (DEPTH, D_MODEL, D_IN), D_MODEL ** -0.5),
        'g_q': gain(ks[11], (DEPTH, Q_LORA)),
        'w_uq': nrm(ks[12], (DEPTH, Q_LORA, MLA_HEADS * (MLA_NOPE + MLA_ROPE)), Q_LORA ** -0.5),
        'g_kv': gain(ks[13], (DEPTH, KV_LORA)),
        'w_uk': nrm(ks[14], (DEPTH, KV_LORA, MLA_HEADS, MLA_NOPE), KV_LORA ** -0.5),
        'w_uv': nrm(ks[15], (DEPTH, KV_LORA, MLA_HEADS, MLA_V), KV_LORA ** -0.5),
        'w_out': nrm(ks[16], (DEPTH, MIX_WIDTH, D_MODEL), MIX_WIDTH ** -0.5),
        'g_ffn': gain(ks[17], (DEPTH, D_MODEL)),
        'w_pq': nrm(ks[18], (DEPTH, D_MODEL, PEER_HEADS * PEER_QDIM), D_MODEL ** -0.5),
        'peer_keys': nrm(ks[19], (DEPTH, PEER_HEADS, 2, PEER_NKEYS, PEER_HALF), PEER_HALF ** -0.5),
        'peer_u': nrm(ks[20], (DEPTH, PEER_EXPERTS, D_MODEL), D_MODEL ** -0.5),
        'peer_v': nrm(ks[21], (DEPTH, PEER_EXPERTS, D_MODEL), PEER_HEADS ** -0.5),
        'g_final': gain(ks[22], (D_MODEL,)),
    }


def reference(x_prompt, x_sample, cache_ckv, cache_krope, cache_k, cache_v, cache_kidx, page_table,
              rel_bias, g_attn, w_in, g_q, w_uq, g_kv, w_uk, w_uv, w_out, g_ffn, w_pq, peer_keys,
              peer_u, peer_v, g_final):
    hp, hs = x_prompt, x_sample
    pos_p = jnp.arange(x_prompt.shape[1])
    pos_s = page_table.shape[1] * PAGE_SIZE + jnp.arange(x_sample.shape[1])
    rows_p = [[], [], [], [], []]
    rows_s = [[], [], [], [], []]
    for l in range(DEPTH):
        q_lat, q_rope, c_kv, k_rope, q_b, k_b, v_b, q_i, k_i, w_i = mixer_inputs(
            rmsnorm(hp, g_attn[l]), pos_p, w_in[l], g_q[l], w_uq[l], g_kv[l], w_uk[l])
        mix = jnp.concatenate([mla_prompt(q_lat, q_rope, c_kv, k_rope, w_uv[l]),
                               dsa_prompt(q_b, k_b, v_b, q_i, k_i, w_i, rel_bias)], axis=-1)
        hp = hp + mix @ w_out[l]
        hp = peer_block(hp, g_ffn[l], w_pq[l], peer_keys[l], peer_u[l], peer_v[l])
        for lst, a in zip(rows_p, (c_kv, k_rope, k_b, v_b, k_i)):
            lst.append(a)
        q_lat, q_rope, c_kv, k_rope, q_b, k_b, v_b, q_i, k_i, w_i = mixer_inputs(
            rmsnorm(hs, g_attn[l]), pos_s, w_in[l], g_q[l], w_uq[l], g_kv[l], w_uk[l])
        mix = jnp.concatenate([
            mla_sample(q_lat, q_rope, c_kv, k_rope, w_uv[l], cache_ckv, cache_krope, l, page_table),
            dsa_sample(q_b, k_b, v_b, q_i, k_i, w_i, cache_k, cache_v, cache_kidx, l, page_table,
                       pos_s, rel_bias)], axis=-1)
        hs = hs + mix @ w_out[l]
        hs = peer_block(hs, g_ffn[l], w_pq[l], peer_keys[l], peer_u[l], peer_v[l])
        for lst, a in zip(rows_s, (c_kv, k_rope, k_b, v_b, k_i)):
            lst.append(a)
    y_prompt = rmsnorm(hp, g_final)
    y_sample = rmsnorm(hs, g_final)
    new_ckv_p, new_krope_p, new_k_p, new_v_p, new_kidx_p = [jnp.stack(r, axis=0) for r in rows_p]
    new_ckv_s, new_krope_s, new_k_s, new_v_s, new_kidx_s = [jnp.stack(r, axis=0) for r in rows_s]
    return (y_prompt, y_sample, new_ckv_p, new_krope_p, new_k_p, new_v_p, new_kidx_p,
            new_ckv_s, new_krope_s, new_k_s, new_v_s, new_kidx_s)
```

```python
import functools
import math

import jax
import jax.numpy as jnp
import numpy as np
from jax import lax
from jax.experimental import pallas as pl
from jax.experimental.pallas import tpu as pltpu

F32 = jnp.float32
BF16 = jnp.bfloat16
I32 = jnp.int32

D_MODEL = 1024
PAGE = 128
MLA_HEADS = 8
MLA_NOPE = 64
MLA_ROPE = 32
MLA_V = 64
Q_LORA = 384
KV_LORA = 256
MLA_SCALE = (MLA_NOPE + MLA_ROPE) ** -0.5
DSA_HEADS = 8
DSA_KV_HEADS = 2
DSA_REP = DSA_HEADS // DSA_KV_HEADS
DSA_HEAD_DIM = 64
DSA_SCALE = DSA_HEAD_DIM ** -0.5
IDX_HEADS = 8
IDX_DIM = 64
IDX_ROPE = 32
IDX_TOPK_MAX = 256
IDX_W_SCALE = (IDX_HEADS * IDX_DIM) ** -0.5
REL_BUCKETS = 32
REL_MAX_DIST = 128
PEER_HEADS = 8
PEER_NKEYS = 128
PEER_EXPERTS = PEER_NKEYS * PEER_NKEYS
PEER_HALF = 128
PEER_TOPK = 16
ROPE_THETA = 10000.0
NORM_EPS = 1e-6

LANES = 128
NEG = -1e30
INT_MIN = -(2 ** 31)
KCAT = KV_LORA + LANES
KV_CHUNK = 256
Q_TILE = 128

_C_CQ, _C_CKV, _C_QB, _C_KB, _C_VB = 0, 384, 640, 1152, 1280
_C_QI, _C_QIP, _C_KR, _C_KRP, _C_KI, _C_KIP, _C_WI, _C_END = 1408, 1920, 2432, 2560, 2688, 2816, 2944, 3072
_VMEM_LIMIT = 56 * 1024 * 1024


def _cparams(n_axes):
    return pltpu.CompilerParams(dimension_semantics=("arbitrary",) * n_axes, vmem_limit_bytes=_VMEM_LIMIT)


def _dot(a, b):
    return jnp.dot(a, b, preferred_element_type=F32)


def _dot_nt(a, b):
    return lax.dot_general(a, b, (((1,), (1,)), ((), ())), preferred_element_type=F32)


def _rms(x, g):
    return x * lax.rsqrt(jnp.mean(x * x, axis=-1, keepdims=True) + NORM_EPS) * g


def _sort_key(x):
    x = jnp.where(x == 0.0, 0.0, x)
    bits = pltpu.bitcast(x, I32)
    return bits ^ ((bits >> 31) & 0x7FFFFFFF)


def _bucket_starts():
    max_exact = REL_BUCKETS // 2
    n = np.arange(0, 2 * REL_MAX_DIST, dtype=np.int64)
    nf = np.maximum(n, 1).astype(np.float32)
    large = max_exact + (np.log(nf / np.float32(max_exact)) / np.float32(math.log(REL_MAX_DIST / max_exact))
                         * np.float32(REL_BUCKETS - max_exact)).astype(np.int32)
    large = np.minimum(large, REL_BUCKETS - 1)
    bucket = np.where(n < max_exact, n, large)
    starts = []
    for k in range(REL_BUCKETS):
        hit = np.nonzero(bucket >= k)[0]
        starts.append(int(hit[0]) if hit.size else int(n[-1]) + 1)
    return starts


_BUCKET_START = _bucket_starts()


def _bias_kernel(rb_ref, bp_ref, bs_ref, bf_ref):
    def bias_of(n, h):
        b = jnp.full(n.shape, rb_ref[REL_BUCKETS - 1, h], F32)
        for k in range(REL_BUCKETS - 2, -1, -1):
            b = jnp.where(n < _BUCKET_START[k + 1], rb_ref[k, h], b)
        return b

    s_i = lax.broadcasted_iota(I32, (KV_CHUNK, Q_TILE), 0)
    t_i = lax.broadcasted_iota(I32, (KV_CHUNK, Q_TILE), 1)
    for w in range(4):
        n = jnp.maximum(w * Q_TILE + t_i - s_i, 0)
        for h in range(DSA_HEADS):
            bp_ref[w, h] = bias_of(n, h)
    t_s = lax.broadcasted_iota(I32, (8, PAGE), 0)
    u_s = lax.broadcasted_iota(I32, (8, PAGE), 1)
    for w in range(2):
        n = jnp.maximum((1 - w) * PAGE + t_s - u_s, 0)
        for h in range(DSA_HEADS):
            bs_ref[w, h] = bias_of(n, h)
    for h in range(DSA_HEADS):
        bf_ref[h] = jnp.full((8, PAGE), rb_ref[REL_BUCKETS - 1, h], F32)


def _bias_tables(rel_bias):
    return pl.pallas_call(
        _bias_kernel,
        out_shape=(jax.ShapeDtypeStruct((4, DSA_HEADS, KV_CHUNK, Q_TILE), F32),
                   jax.ShapeDtypeStruct((2, DSA_HEADS, 8, PAGE), F32),
                   jax.ShapeDtypeStruct((DSA_HEADS, 8, PAGE), F32)),
        in_specs=[pl.BlockSpec(memory_space=pltpu.SMEM)],
        name="bias_tables",
    )(rel_bias)


def _inproj_kernel(x_ref, ga_ref, w1_ref, gq_ref, wuq_ref, gkv_ref, wuk_ref, ca_ref, sa_ref, cb_ref, sb_ref,
                   ckv_ref, krope_ref, kb_ref, vb_ref, ki_ref,
                   kcat_ref, kbbf_ref, vbbf_ref, kibf_ref, qcat_ref, qb_ref, qi_ref, wi_ref):
    xn = _rms(x_ref[...], ga_ref[...]).astype(BF16)

    def proj(lo, hi):
        return _dot(xn, w1_ref[:, lo:hi])

    ca, sa, cb, sb = ca_ref[...], sa_ref[...], cb_ref[...], sb_ref[...]

    ckv = _rms(proj(_C_CKV, _C_QB), gkv_ref[...])
    ckv_ref[...] = ckv
    kcat_ref[:, 0:KV_LORA] = ckv.astype(BF16)
    kr = proj(_C_KR, _C_KRP) * ca + proj(_C_KRP, _C_KI) * sa
    krope_ref[...] = kr[:, :MLA_ROPE]
    kcat_ref[:, KV_LORA:KCAT] = kr.astype(BF16)
    ki = proj(_C_KI, _C_KIP) * ca + proj(_C_KIP, _C_WI) * sa
    ki_ref[...] = ki[:, :IDX_DIM]
    kibf_ref[...] = ki[:, :IDX_DIM].astype(BF16)
    kb = proj(_C_KB, _C_VB)
    kb_ref[...] = kb
    kbbf_ref[...] = kb.astype(BF16)
    vb = proj(_C_VB, _C_QI)
    vb_ref[...] = vb
    vbbf_ref[...] = vb.astype(BF16)
    wi_ref[...] = proj(_C_WI, _C_END)[:, :IDX_HEADS] * IDX_W_SCALE

    qb = proj(_C_QB, _C_KB)
    for h in range(DSA_HEADS):
        qb_ref[h] = qb[:, h * DSA_HEAD_DIM:(h + 1) * DSA_HEAD_DIM].astype(BF16)
    qi = proj(_C_QI, _C_QIP)
    qip = proj(_C_QIP, _C_KR)
    for s in range(4):
        slab = qi[:, s * LANES:(s + 1) * LANES] * cb + qip[:, s * LANES:(s + 1) * LANES] * sb
        qi_ref[2 * s] = slab[:, :IDX_DIM].astype(BF16)
        qi_ref[2 * s + 1] = slab[:, IDX_DIM:].astype(BF16)

    cq = _rms(proj(_C_CQ, _C_CKV), gq_ref[...]).astype(BF16)
    n_nope = MLA_HEADS * MLA_NOPE
    n_pad = MLA_HEADS * LANES
    nope = _dot(cq, wuq_ref[:, 0:n_nope]).astype(BF16)
    for p in range(MLA_HEADS // 2):
        ql = _dot(nope[:, p * LANES:(p + 1) * LANES], wuk_ref[p])
        qcat_ref[2 * p, :, 0:KV_LORA] = ql[:, :KV_LORA].astype(BF16)
        qcat_ref[2 * p + 1, :, 0:KV_LORA] = ql[:, KV_LORA:].astype(BF16)
    for h in range(MLA_HEADS):
        lo = n_nope + h * LANES
        qr = _dot(cq, wuq_ref[:, lo:lo + LANES]) * ca + _dot(cq, wuq_ref[:, lo + n_pad:lo + n_pad + LANES]) * sa
        qcat_ref[h, :, KV_LORA:KCAT] = qr.astype(BF16)


def _inproj(x2d, tabs, tab_blocks, wts, tm):
    n = x2d.shape[0]
    const2 = lambda i: (0, 0)
    const3 = lambda i: (0, 0, 0)
    row = lambda i: (i, 0)
    tab = lambda i: (i % tab_blocks, 0)
    hm = lambda i: (0, i, 0)
    in_specs = [
        pl.BlockSpec((tm, D_MODEL), row),
        pl.BlockSpec((1, D_MODEL), const2),
        pl.BlockSpec((D_MODEL, _C_END), const2),
        pl.BlockSpec((1, Q_LORA), const2),
        pl.BlockSpec(wts["wuq"].shape, const2),
        pl.BlockSpec((1, KV_LORA), const2),
        pl.BlockSpec(wts["wuk"].shape, const3),
    ] + [pl.BlockSpec((tm, LANES), tab)] * 4
    out_shape = (
        jax.ShapeDtypeStruct((n, KV_LORA), F32), jax.ShapeDtypeStruct((n, MLA_ROPE), F32),
        jax.ShapeDtypeStruct((n, LANES), F32), jax.ShapeDtypeStruct((n, LANES), F32),
        jax.ShapeDtypeStruct((n, IDX_DIM), F32),
        jax.ShapeDtypeStruct((n, KCAT), BF16), jax.ShapeDtypeStruct((n, LANES), BF16),
        jax.ShapeDtypeStruct((n, LANES), BF16), jax.ShapeDtypeStruct((n, IDX_DIM), BF16),
        jax.ShapeDtypeStruct((MLA_HEADS, n, KCAT), BF16),
        jax.ShapeDtypeStruct((DSA_HEADS, n, DSA_HEAD_DIM), BF16),
        jax.ShapeDtypeStruct((IDX_HEADS, n, IDX_DIM), BF16),
        jax.ShapeDtypeStruct((n, IDX_HEADS), F32),
    )
    out_specs = (
        pl.BlockSpec((tm, KV_LORA), row), pl.BlockSpec((tm, MLA_ROPE), row),
        pl.BlockSpec((tm, LANES), row), pl.BlockSpec((tm, LANES), row), pl.BlockSpec((tm, IDX_DIM), row),
        pl.BlockSpec((tm, KCAT), row), pl.BlockSpec((tm, LANES), row),
        pl.BlockSpec((tm, LANES), row), pl.BlockSpec((tm, IDX_DIM), row),
        pl.BlockSpec((MLA_HEADS, tm, KCAT), hm),
        pl.BlockSpec((DSA_HEADS, tm, DSA_HEAD_DIM), hm),
        pl.BlockSpec((IDX_HEADS, tm, IDX_DIM), hm),
        pl.BlockSpec((tm, IDX_HEADS), row),
    )
    return pl.pallas_call(
        _inproj_kernel, grid=(n // tm,), in_specs=in_specs, out_specs=out_specs, out_shape=out_shape,
        compiler_params=_cparams(1), name="inproj",
    )(x2d, wts["g_attn"], wts["w1"], wts["g_q"], wts["wuq"], wts["g_kv"], wts["wuk"], *tabs)


def _rep(x, width):
    k = width // LANES
    return x if k == 1 else jnp.concatenate([x] * k, axis=1)


def _mla_prompt_kernel(q_ref, k_ref, wuv_ref, o_ref, m_scr, l_scr, acc_scr):
    j = pl.program_id(1)
    rows = MLA_HEADS * Q_TILE
    q = q_ref[...].reshape(rows, KCAT)
    m_scr[...] = jnp.full(m_scr.shape, NEG, F32)
    l_scr[...] = jnp.zeros(l_scr.shape, F32)
    acc_scr[...] = jnp.zeros(acc_scr.shape, F32)
    t_row = j * Q_TILE + lax.broadcasted_iota(I32, (rows, KV_CHUNK), 0) % Q_TILE
    u_col = lax.broadcasted_iota(I32, (rows, KV_CHUNK), 1)

    def body(c, carry):
        k = k_ref[pl.ds(pl.multiple_of(c * KV_CHUNK, KV_CHUNK), KV_CHUNK), :]
        s = _dot_nt(q, k) * MLA_SCALE
        s = jnp.where(c * KV_CHUNK + u_col <= t_row, s, NEG)
        m_prev = m_scr[...]
        m_new = jnp.maximum(m_prev, jnp.max(s, axis=1, keepdims=True))
        alpha = jnp.exp(m_prev - m_new)
        p = jnp.exp(s - _rep(m_new, KV_CHUNK))
        l_scr[...] = alpha * l_scr[...] + jnp.sum(p, axis=1, keepdims=True)
        acc_scr[...] = acc_scr[...] * _rep(alpha, KV_LORA) + _dot(p.astype(BF16), k[:, :KV_LORA])
        m_scr[...] = m_new
        return carry

    lax.fori_loop(0, (j * Q_TILE) // KV_CHUNK + 1, body, 0)
    o_lat = (acc_scr[...] / _rep(l_scr[...], KV_LORA)).astype(BF16)
    for h in range(MLA_HEADS):
        o = _dot(o_lat[h * Q_TILE:(h + 1) * Q_TILE], wuv_ref[h])
        o_ref[:, h * MLA_V:(h + 1) * MLA_V] = o.astype(BF16)


def _mla_prompt(qcat, kcat, wuv, b, s):
    nq = s // Q_TILE
    rows = MLA_HEADS * Q_TILE
    return pl.pallas_call(
        _mla_prompt_kernel, grid=(b, nq),
        in_specs=[pl.BlockSpec((MLA_HEADS, Q_TILE, KCAT), lambda bi, j: (0, bi * nq + j, 0)),
                  pl.BlockSpec((None, s, KCAT), lambda bi, j: (bi, 0, 0)),
                  pl.BlockSpec(wuv.shape, lambda bi, j: (0, 0, 0))],
        out_specs=pl.BlockSpec((Q_TILE, MLA_HEADS * MLA_V), lambda bi, j: (bi * nq + j, 0)),
        out_shape=jax.ShapeDtypeStruct((b * s, MLA_HEADS * MLA_V), BF16),
        scratch_shapes=[pltpu.VMEM((rows, LANES), F32), pltpu.VMEM((rows, LANES), F32),
                        pltpu.VMEM((rows, KV_LORA), F32)],
        compiler_params=_cparams(2), name="mla_prompt",
    )(qcat, kcat.reshape(b, s, KCAT), wuv)


def _topk_threshold(key_scr, n_chunks, topk, n_keys_pow2_bits):
    lanes = key_scr.shape[1]
    sub = KV_CHUNK // 8

    def count(pred_fn):
        def body(c, acc):
            off = pl.multiple_of(c * KV_CHUNK, KV_CHUNK)
            k = key_scr[pl.ds(off, KV_CHUNK), :]
            hit = pred_fn(k, c).astype(I32)
            return acc + jnp.sum(hit.reshape(sub, 8, lanes), axis=0)

        acc = lax.fori_loop(0, n_chunks, body, jnp.zeros((8, lanes), I32))
        return jnp.sum(acc, axis=0, keepdims=True)

    def bit_body(i, res):
        cand = res | jnp.left_shift(jnp.int32(1), 31 - i)
        cs = cand ^ INT_MIN
        cnt = count(lambda k, c: k >= cs)
        return jnp.where(cnt >= topk, cand, res)

    res = lax.fori_loop(0, 32, bit_body, jnp.zeros((1, lanes), I32))
    tau = res ^ INT_MIN
    cnt_gt = count(lambda k, c: k > tau)
    cnt_eq = count(lambda k, c: k == tau)
    need = topk - cnt_gt
    row0 = lax.broadcasted_iota(I32, (KV_CHUNK, lanes), 0)
    big = jnp.int32(2 ** 30)

    def cut_search():
        def cbody(i, cur):
            cand = cur | jnp.left_shift(jnp.int32(1), n_keys_pow2_bits - 1 - i)
            f = count(lambda k, c: jnp.where(k == tau, row0 + c * KV_CHUNK, big) < cand)
            return jnp.where(f < need, cand, cur)

        return lax.fori_loop(0, n_keys_pow2_bits, cbody, jnp.zeros((1, lanes), I32))

    cut = lax.cond(jnp.max(cnt_eq - need) > 0, cut_search, lambda: jnp.full((1, lanes), big, I32))
    return tau, cut


def _dsa_prompt_kernel(qi_ref, wt_ref, ki_ref, qb_ref, kb_ref, vt_ref, bias_ref, o_ref,
                       key_scr, mb_scr, tc_scr, m_scr, l_scr, acc_scr, *, topk, idx_bits):
    j = pl.program_id(1)
    n_chunks = (j * Q_TILE) // KV_CHUNK + 1
    t_row = j * Q_TILE + lax.broadcasted_iota(I32, (KV_CHUNK, Q_TILE), 1)
    s_loc = lax.broadcasted_iota(I32, (KV_CHUNK, Q_TILE), 0)
    qi = qi_ref[...].reshape(IDX_HEADS * Q_TILE, IDX_DIM)
    wt = wt_ref[...]

    def score_body(c, carry):
        off = pl.multiple_of(c * KV_CHUNK, KV_CHUNK)
        a = _dot_nt(ki_ref[pl.ds(off, KV_CHUNK), :], qi)
        sc = jnp.zeros((KV_CHUNK, Q_TILE), F32)
        for h in range(IDX_HEADS):
            sc = sc + wt[h:h + 1, :] * jnp.maximum(a[:, h * Q_TILE:(h + 1) * Q_TILE], 0.0)
        key = jnp.where(off + s_loc <= t_row, _sort_key(sc), INT_MIN)
        key_scr[pl.ds(off, KV_CHUNK), :] = key
        return carry

    lax.fori_loop(0, n_chunks, score_body, 0)

    @pl.when((j + 1) * Q_TILE <= topk)
    def _():
        tc_scr[0:1, :] = jnp.full((1, Q_TILE), INT_MIN, I32)
        tc_scr[1:2, :] = jnp.full((1, Q_TILE), -1, I32)

    @pl.when((j + 1) * Q_TILE > topk)
    def _():
        tau, cut = _topk_threshold(key_scr, n_chunks, topk, idx_bits)
        tc_scr[0:1, :] = tau
        tc_scr[1:2, :] = cut

    tau = tc_scr[0:1, :]
    cut = tc_scr[1:2, :]

    def mask_body(c, carry):
        off = pl.multiple_of(c * KV_CHUNK, KV_CHUNK)
        k = key_scr[pl.ds(off, KV_CHUNK), :]
        spos = off + s_loc
        v = jnp.where(k > tau, 0.0, jnp.where(k == tau, jnp.where(spos <= cut, 0.0, NEG), NEG))
        mb_scr[pl.ds(off, KV_CHUNK), :] = jnp.where(spos <= t_row, v, NEG)
        return carry

    lax.fori_loop(0, n_chunks, mask_body, 0)

    m_scr[...] = jnp.full(m_scr.shape, NEG, F32)
    l_scr[...] = jnp.zeros(l_scr.shape, F32)
    acc_scr[...] = jnp.zeros(acc_scr.shape, F32)
    qb = qb_ref[...]

    def att_body(c, carry):
        off = pl.multiple_of(c * KV_CHUNK, KV_CHUNK)
        kb = kb_ref[pl.ds(off, KV_CHUNK), :]
        mb = mb_scr[pl.ds(off, KV_CHUNK), :]
        bidx = jnp.minimum((j * Q_TILE - c * KV_CHUNK) // Q_TILE, 3)
        for g in range(DSA_KV_HEADS):
            kg = kb[:, g * DSA_HEAD_DIM:(g + 1) * DSA_HEAD_DIM]
            qg = qb[g * DSA_REP:(g + 1) * DSA_REP].reshape(DSA_REP * Q_TILE, DSA_HEAD_DIM)
            lg4 = _dot_nt(kg, qg) * DSA_SCALE
            vg = vt_ref[c, g * DSA_HEAD_DIM:(g + 1) * DSA_HEAD_DIM, :]
            for r in range(DSA_REP):
                h = g * DSA_REP + r
                lg = lg4[:, r * Q_TILE:(r + 1) * Q_TILE] + bias_ref[bidx, h] + mb
                m_prev = m_scr[h:h + 1, :]
                m_new = jnp.maximum(m_prev, jnp.max(lg, axis=0, keepdims=True))
                alpha = jnp.exp(m_prev - m_new)
                p = jnp.exp(lg - m_new)
                l_scr[h:h + 1, :] = alpha * l_scr[h:h + 1, :] + jnp.sum(p, axis=0, keepdims=True)
                rs = slice(h * DSA_HEAD_DIM, (h + 1) * DSA_HEAD_DIM)
                acc_scr[rs, :] = alpha * acc_scr[rs, :] + _dot(vg, p.astype(BF16))
                m_scr[h:h + 1, :] = m_new
        return carry

    lax.fori_loop(0, n_chunks, att_body, 0)
    inv = 1.0 / l_scr[...]
    parts = [acc_scr[h * DSA_HEAD_DIM:(h + 1) * DSA_HEAD_DIM, :] * inv[h:h + 1, :] for h in range(DSA_HEADS)]
    o_ref[...] = jnp.concatenate(parts, axis=0).T.astype(BF16)


def _dsa_prompt(qi_hm, wi_t, ki_bf, qb_hm, kb_bf, v_t, bias_p, b, s):
    nq = s // Q_TILE
    topk = min(IDX_TOPK_MAX, s // 4)
    idx_bits = max(1, int(math.ceil(math.log2(s))))
    width = DSA_HEADS * DSA_HEAD_DIM
    kern = functools.partial(_dsa_prompt_kernel, topk=topk, idx_bits=idx_bits)
    return pl.pallas_call(
        kern, grid=(b, nq),
        in_specs=[pl.BlockSpec((IDX_HEADS, Q_TILE, IDX_DIM), lambda bi, j: (0, bi * nq + j, 0)),
                  pl.BlockSpec((IDX_HEADS, Q_TILE), lambda bi, j: (0, bi * nq + j)),
                  pl.BlockSpec((None, s, IDX_DIM), lambda bi, j: (bi, 0, 0)),
                  pl.BlockSpec((DSA_HEADS, Q_TILE, DSA_HEAD_DIM), lambda bi, j: (0, bi * nq + j, 0)),
                  pl.BlockSpec((None, s, LANES), lambda bi, j: (bi, 0, 0)),
                  pl.BlockSpec((None, s // KV_CHUNK, LANES, KV_CHUNK), lambda bi, j: (bi, 0, 0, 0)),
                  pl.BlockSpec(bias_p.shape, lambda bi, j: (0, 0, 0, 0))],
        out_specs=pl.BlockSpec((Q_TILE, width), lambda bi, j: (bi * nq + j, 0)),
        out_shape=jax.ShapeDtypeStruct((b * s, width), BF16),
        scratch_shapes=[pltpu.VMEM((s, Q_TILE), I32), pltpu.VMEM((s, Q_TILE), F32), pltpu.VMEM((8, Q_TILE), I32),
                        pltpu.VMEM((DSA_HEADS, Q_TILE), F32), pltpu.VMEM((DSA_HEADS, Q_TILE), F32),
                        pltpu.VMEM((width, Q_TILE), F32)],
        compiler_params=_cparams(2), name="dsa_prompt",
    )(qi_hm, wi_t, ki_bf.reshape(b, s, IDX_DIM), qb_hm, kb_bf.reshape(b, s, LANES), v_t, bias_p)


def _sample1_kernel(pt_ref, ql_ref, qr_ref, qi_ref, wi_ref, ckvn_ref, krn_ref, kin_ref, *rest, pp):
    ckv_pages, kr_pages, ki_pages = rest[0:pp], rest[pp:2 * pp], rest[2 * pp:3 * pp]
    olat_ref, sc_ref, scn_ref = rest[3 * pp:3 * pp + 3]
    m_scr, l_scr, acc_scr = rest[3 * pp + 3:]
    j = pl.program_id(1)
    last = pl.num_programs(1) - 1
    rows = ql_ref.shape[0]
    n_tok = rows // MLA_HEADS

    @pl.when(j == 0)
    def _():
        m_scr[...] = jnp.full(m_scr.shape, NEG, F32)
        l_scr[...] = jnp.zeros(l_scr.shape, F32)
        acc_scr[...] = jnp.zeros(acc_scr.shape, F32)

    ql, qr, qi, wi = ql_ref[...], qr_ref[...], qi_ref[...], wi_ref[...]

    def attend(kc, kr, mask):
        s = (_dot_nt(ql, kc) + _dot_nt(qr, kr)) * MLA_SCALE
        if mask is not None:
            s = jnp.where(mask, s, NEG)
        m_prev = m_scr[...]
        m_new = jnp.maximum(m_prev, jnp.max(s, axis=1, keepdims=True))
        alpha = jnp.exp(m_prev - m_new)
        p = jnp.exp(s - m_new)
        l_scr[...] = alpha * l_scr[...] + jnp.sum(p, axis=1, keepdims=True)
        acc_scr[...] = acc_scr[...] * _rep(alpha, KV_LORA) + _dot(p.astype(BF16), kc)
        m_scr[...] = m_new

    def index(kidx):
        a = jnp.maximum(_dot_nt(qi, kidx), 0.0) * wi
        return jnp.sum(a.reshape(n_tok, IDX_HEADS, PAGE), axis=1)

    for k in range(pp):
        attend(ckv_pages[k][...].astype(BF16), kr_pages[k][...].astype(BF16), None)
        sc_ref[:, k * PAGE:(k + 1) * PAGE] = index(ki_pages[k][...].astype(BF16))

    @pl.when(j == last)
    def _():
        t_r = lax.broadcasted_iota(I32, (rows, PAGE), 0) % n_tok
        u_c = lax.broadcasted_iota(I32, (rows, PAGE), 1)
        attend(ckvn_ref[...], krn_ref[...], u_c <= t_r)
        t4 = lax.broadcasted_iota(I32, (n_tok, PAGE), 0)
        u4 = lax.broadcasted_iota(I32, (n_tok, PAGE), 1)
        scn_ref[...] = jnp.where(u4 <= t4, index(kin_ref[...]), -jnp.inf)
        olat_ref[...] = acc_scr[...] / _rep(l_scr[...], KV_LORA)


def _page_specs(shape_tail, n_pages, pp):
    nd = len(shape_tail)

    def make(k):
        return pl.BlockSpec((None,) + shape_tail,
                            lambda bi, j, pt: (pt[bi * n_pages + j * pp + k],) + (0,) * nd)

    return [make(k) for k in range(pp)]


def _sample1(pt_flat, ql, qr, qi, wi, ckvn, krn, kin, c_ckv, c_kr, c_ki, n_pages, pp):
    bs, rows = ql.shape[0], ql.shape[1]
    n_tok = rows // MLA_HEADS
    per_b = lambda tail: pl.BlockSpec((None,) + tail, lambda bi, j, pt: (bi,) + (0,) * len(tail))
    in_specs = [per_b((rows, KV_LORA)), per_b((rows, MLA_ROPE)), per_b((rows, IDX_DIM)), per_b((rows, 1)),
                per_b((PAGE, KV_LORA)), per_b((PAGE, MLA_ROPE)), per_b((PAGE, IDX_DIM))]
    in_specs += _page_specs((PAGE, KV_LORA), n_pages, pp) + _page_specs((PAGE, MLA_ROPE), n_pages, pp)
    in_specs += _page_specs((PAGE, IDX_DIM), n_pages, pp)
    out_specs = (per_b((rows, KV_LORA)),
                 pl.BlockSpec((None, n_tok, pp * PAGE), lambda bi, j, pt: (bi, 0, j)),
                 per_b((n_tok, PAGE)))
    out_shape = (jax.ShapeDtypeStruct((bs, rows, KV_LORA), F32),
                 jax.ShapeDtypeStruct((bs, n_tok, n_pages * PAGE), F32),
                 jax.ShapeDtypeStruct((bs, n_tok, PAGE), F32))
    grid_spec = pltpu.PrefetchScalarGridSpec(
        num_scalar_prefetch=1, grid=(bs, n_pages // pp), in_specs=in_specs, out_specs=out_specs,
        scratch_shapes=[pltpu.VMEM((rows, LANES), F32), pltpu.VMEM((rows, LANES), F32),
                        pltpu.VMEM((rows, KV_LORA), F32)])
    return pl.pallas_call(
        functools.partial(_sample1_kernel, pp=pp), grid_spec=grid_spec, out_shape=out_shape,
        compiler_params=_cparams(2), name="sample_mla_index",
    )(pt_flat, ql, qr, qi, wi, ckvn, krn, kin, *([c_ckv] * pp), *([c_kr] * pp), *([c_ki] * pp))


def _mla_out_kernel(o_ref, wuv_ref, out_ref):
    for h in range(MLA_HEADS):
        out_ref[:, h * MLA_V:(h + 1) * MLA_V] = _dot(o_ref[h], wuv_ref[h]).astype(BF16)


def _mla_out(olat_hm, wuv):
    n = olat_hm.shape[1]
    return pl.pallas_call(
        _mla_out_kernel, out_shape=jax.ShapeDtypeStruct((n, MLA_HEADS * MLA_V), BF16), name="sample_mla_out",
    )(olat_hm, wuv)


def _sample_select_kernel(sc_ref, tau_ref, cut_ref, key_scr, *, topk, idx_bits):
    n_chunks = sc_ref.shape[0] // KV_CHUNK

    def kbody(c, carry):
        off = pl.multiple_of(c * KV_CHUNK, KV_CHUNK)
        key_scr[pl.ds(off, KV_CHUNK), :] = _sort_key(sc_ref[pl.ds(off, KV_CHUNK), :])
        return carry

    lax.fori_loop(0, n_chunks, kbody, 0)
    tau, cut = _topk_threshold(key_scr, n_chunks, topk, idx_bits)
    tau_ref[...] = tau
    cut_ref[...] = cut


def _sample_select(sc_t, topk):
    kp, ns = sc_t.shape
    lt = min(LANES, ns)
    idx_bits = max(1, int(math.ceil(math.log2(kp))))
    kern = functools.partial(_sample_select_kernel, topk=topk, idx_bits=idx_bits)
    return pl.pallas_call(
        kern, grid=(ns // lt,),
        in_specs=[pl.BlockSpec((kp, lt), lambda i: (0, i))],
        out_specs=(pl.BlockSpec((1, lt), lambda i: (0, i)), pl.BlockSpec((1, lt), lambda i: (0, i))),
        out_shape=(jax.ShapeDtypeStruct((1, ns), I32), jax.ShapeDtypeStruct((1, ns), I32)),
        scratch_shapes=[pltpu.VMEM((kp, lt), I32)],
        compiler_params=_cparams(1), name="sample_select",
    )(sc_t)


def _sample3_kernel(pt_ref, qb_ref, sc_ref, scn_ref, tau_ref, cut_ref, kn_ref, vn_ref, bs_ref, bf_ref, *rest,
                    pp, past):
    k_pages, v_pages = rest[0:pp], rest[pp:2 * pp]
    o_ref = rest[2 * pp]
    m_scr, l_scr, acc_scr = rest[2 * pp + 1:]
    j = pl.program_id(1)
    last = pl.num_programs(1) - 1
    n_tok = sc_ref.shape[0]

    @pl.when(j == 0)
    def _():
        m_scr[...] = jnp.full(m_scr.shape, NEG, F32)
        l_scr[...] = jnp.zeros(l_scr.shape, F32)
        acc_scr[...] = jnp.zeros(acc_scr.shape, F32)

    tau, cut = tau_ref[...], cut_ref[...]
    u4 = lax.broadcasted_iota(I32, (n_tok, PAGE), 1)

    def mask_bias(sc, base):
        k = _sort_key(sc)
        spos = base + u4
        return jnp.where(k > tau, 0.0, jnp.where(k == tau, jnp.where(spos <= cut, 0.0, NEG), NEG))

    def attend(kp, vp, mb4, bias):
        mb = jnp.concatenate([mb4] * DSA_REP, axis=0)
        for g in range(DSA_KV_HEADS):
            cs = slice(g * DSA_HEAD_DIM, (g + 1) * DSA_HEAD_DIM)
            lg = _dot_nt(qb_ref[g], kp[:, cs]) * DSA_SCALE + bias[g] + mb
            m_prev = m_scr[g]
            m_new = jnp.maximum(m_prev, jnp.max(lg, axis=1, keepdims=True))
            alpha = jnp.exp(m_prev - m_new)
            p = jnp.exp(lg - m_new)
            l_scr[g] = alpha * l_scr[g] + jnp.sum(p, axis=1, keepdims=True)
            acc_scr[g] = acc_scr[g] * alpha[:, :DSA_HEAD_DIM] + _dot(p.astype(BF16), vp[:, cs])
            m_scr[g] = m_new

    far = bf_ref[...]
    for k in range(pp):
        base = (j * pp + k) * PAGE
        mb4 = mask_bias(sc_ref[:, k * PAGE:(k + 1) * PAGE], base)
        bias = far
        if k == pp - 1:
            bias = jnp.where(j == last, bs_ref[0], far)
        attend(k_pages[k][...].astype(BF16), v_pages[k][...].astype(BF16), mb4, bias)

    @pl.when(j == last)
    def _():
        attend(kn_ref[...], vn_ref[...], mask_bias(scn_ref[...], past), bs_ref[1])
        for g in range(DSA_KV_HEADS):
            o_ref[g] = acc_scr[g] / l_scr[g][:, :DSA_HEAD_DIM]


def _sample3(pt_flat, qb, sc, scn, tau, cut, kn, vn, bias_s, bias_f, c_k, c_v, n_pages, pp):
    bs, n_tok = sc.shape[0], sc.shape[1]
    rows = DSA_REP * n_tok
    per_b = lambda tail: pl.BlockSpec((None,) + tail, lambda bi, j, pt: (bi,) + (0,) * len(tail))
    const = lambda shape: pl.BlockSpec(shape, lambda bi, j, pt: (0,) * len(shape))
    in_specs = [per_b((DSA_KV_HEADS, rows, DSA_HEAD_DIM)),
                pl.BlockSpec((None, n_tok, pp * PAGE), lambda bi, j, pt: (bi, 0, j)),
                per_b((n_tok, PAGE)), per_b((n_tok, 1)), per_b((n_tok, 1)),
                per_b((PAGE, LANES)), per_b((PAGE, LANES)),
                const(bias_s.shape), const(bias_f.shape)]
    in_specs += _page_specs((PAGE, LANES), n_pages, pp) + _page_specs((PAGE, LANES), n_pages, pp)
    grid_spec = pltpu.PrefetchScalarGridSpec(
        num_scalar_prefetch=1, grid=(bs, n_pages // pp), in_specs=in_specs,
        out_specs=per_b((DSA_KV_HEADS, rows, DSA_HEAD_DIM)),
        scratch_shapes=[pltpu.VMEM((DSA_KV_HEADS, rows, LANES), F32), pltpu.VMEM((DSA_KV_HEADS, rows, LANES), F32),
                        pltpu.VMEM((DSA_KV_HEADS, rows, DSA_HEAD_DIM), F32)])
    kern = functools.partial(_sample3_kernel, pp=pp, past=n_pages * PAGE)
    return pl.pallas_call(
        kern, grid_spec=grid_spec,
        out_shape=jax.ShapeDtypeStruct((bs, DSA_KV_HEADS, rows, DSA_HEAD_DIM), F32),
        compiler_params=_cparams(2), name="sample_dsa",
    )(pt_flat, qb, sc, scn, tau, cut, kn, vn, bias_s, bias_f, *([c_k] * pp), *([c_v] * pp))


_N_EXTRACT = PEER_TOPK + 1


def _extract_top(cur, n):
    vals = []
    for _ in range(n):
        m = jnp.max(cur, axis=0, keepdims=True)
        vals.append(m)
        cur = jnp.where(cur == m, -jnp.inf, cur)
    return vals


def _peer_prep_kernel(x_ref, mla_ref, dsa_ref, wo_ref, g_ref, wpq_ref, keys_ref,
                      h_ref, xnt_ref, thr_ref, a_ref, s2_ref, b_ref):
    half = wo_ref.shape[0] // 2
    h = x_ref[...] + _dot(mla_ref[...], wo_ref[0:half, :]) + _dot(dsa_ref[...], wo_ref[half:, :])
    h_ref[...] = h
    xnt = _rms(h, g_ref[...]).T.astype(BF16)
    xnt_ref[...] = xnt
    tc = xnt.shape[1]
    r8 = lax.broadcasted_iota(I32, (8, tc), 0)
    for hh in range(PEER_HEADS):
        scores, tops = [], []
        for p in range(2):
            hp = hh * 2 + p
            qt = _dot(wpq_ref[hp * PEER_HALF:(hp + 1) * PEER_HALF, :], xnt)
            s = _dot(keys_ref[hp], qt.astype(BF16))
            scores.append(s)
            tops.append(_extract_top(s, _N_EXTRACT))
        sv1, sv2 = tops
        sv2_16 = jnp.concatenate(sv2[:PEER_TOPK], axis=0)
        sv2_8 = sv2_16[:8]
        blocks = [sv1[0] + sv2_16]
        for r1 in range(1, 8):
            blocks.append(jnp.where(r8 < PEER_TOPK // (r1 + 1), sv1[r1] + sv2_8, -jnp.inf))
        blocks.append(jnp.concatenate(sv1[8:PEER_TOPK], axis=0) + sv2[0])
        extra = jnp.where(r8 == 0, sv1[0] + sv2[PEER_TOPK],
                          jnp.where(r8 == 1, sv1[PEER_TOPK] + sv2[0], -jnp.inf))
        blocks.append(extra)
        cand = _extract_top(jnp.concatenate(blocks, axis=0), _N_EXTRACT)
        m0 = sv1[0] + sv2[0]
        z = jnp.zeros_like(m0)
        for r in range(PEER_TOPK):
            z = z + jnp.exp(cand[r] - m0)
        c16, c17 = cand[PEER_TOPK - 1], cand[PEER_TOPK]
        tau = jnp.where(c17 == -jnp.inf, c16, 0.5 * (c16 + c17))
        thr_ref[hh] = tau - scores[0]
        a_ref[hh] = jnp.exp(scores[0] - sv1[0]) / z
        s2_ref[hh] = scores[1]
        b_ref[hh] = jnp.exp(scores[1] - sv2[0])


def _peer_prep(x2d, mla, dsa, wts, tc):
    n = x2d.shape[0]
    row = lambda i: (i, 0)
    const2 = lambda i: (0, 0)
    col3 = lambda i: (0, 0, i)
    gate_shape = jax.ShapeDtypeStruct((PEER_HEADS, PEER_NKEYS, n), F32)
    gate_spec = pl.BlockSpec((PEER_HEADS, PEER_NKEYS, tc), col3)
    mix = mla.shape[1]
    return pl.pallas_call(
        _peer_prep_kernel, grid=(n // tc,),
        in_specs=[pl.BlockSpec((tc, D_MODEL), row), pl.BlockSpec((tc, mix), row), pl.BlockSpec((tc, mix), row),
                  pl.BlockSpec(wts["w_out"].shape, const2), pl.BlockSpec((1, D_MODEL), const2),
                  pl.BlockSpec(wts["wpq_t"].shape, const2), pl.BlockSpec(wts["peer_keys"].shape, lambda i: (0, 0, 0))],
        out_specs=(pl.BlockSpec((tc, D_MODEL), row), pl.BlockSpec((D_MODEL, tc), lambda i: (0, i)),
                   gate_spec, gate_spec, gate_spec, gate_spec),
        out_shape=(jax.ShapeDtypeStruct((n, D_MODEL), F32), jax.ShapeDtypeStruct((D_MODEL, n), BF16),
                   gate_shape, gate_shape, gate_shape, gate_shape),
        compiler_params=_cparams(1), name="peer_prep",
    )(x2d, mla, dsa, wts["w_out"], wts["g_ffn"], wts["wpq_t"], wts["peer_keys"])


def _gelu(x):
    return 0.5 * x * (1.0 + lax.erf(x * np.float32(math.sqrt(0.5))))


def _peer_main_kernel(xnt_ref, thr_ref, a_ref, s2_ref, b_ref, u_ref, vt_ref, h_ref, gf_ref, y_ref, acc_scr, *, ni):
    eb = pl.program_id(1)

    @pl.when(eb == 0)
    def _():
        acc_scr[...] = jnp.zeros(acc_scr.shape, F32)

    act = _gelu(_dot(u_ref[...], xnt_ref[...]))
    ws = []
    for ii in range(ni):
        i1 = eb * ni + ii
        gate = jnp.zeros((PEER_NKEYS, act.shape[1]), F32)
        for hh in range(PEER_HEADS):
            thr_row = thr_ref[hh, pl.ds(i1, 1), :]
            a_row = a_ref[hh, pl.ds(i1, 1), :]
            gate = gate + jnp.where(s2_ref[hh] >= thr_row, b_ref[hh], 0.0) * a_row
        ws.append((gate * act[ii * PEER_NKEYS:(ii + 1) * PEER_NKEYS]).astype(BF16))
    acc_scr[...] += _dot(vt_ref[...], jnp.concatenate(ws, axis=0))

    @pl.when(eb == pl.num_programs(1) - 1)
    def _():
        y_ref[...] = _rms(acc_scr[...].T + h_ref[...], gf_ref[...])


def _peer_main(xnt, thr, a, s2, b, h, wts, tc, ni):
    n = h.shape[0]
    eb = ni * PEER_NKEYS
    gate_spec = pl.BlockSpec((PEER_HEADS, PEER_NKEYS, tc), lambda i, e: (0, 0, i))
    return pl.pallas_call(
        functools.partial(_peer_main_kernel, ni=ni), grid=(n // tc, PEER_EXPERTS // eb),
        in_specs=[pl.BlockSpec((D_MODEL, tc), lambda i, e: (0, i)), gate_spec, gate_spec, gate_spec, gate_spec,
                  pl.BlockSpec((eb, D_MODEL), lambda i, e: (e, 0)), pl.BlockSpec((D_MODEL, eb), lambda i, e: (0, e)),
                  pl.BlockSpec((tc, D_MODEL), lambda i, e: (i, 0)), pl.BlockSpec((1, D_MODEL), lambda i, e: (0, 0))],
        out_specs=pl.BlockSpec((tc, D_MODEL), lambda i, e: (i, 0)),
        out_shape=jax.ShapeDtypeStruct((n, D_MODEL), F32),
        scratch_shapes=[pltpu.VMEM((D_MODEL, tc), F32)],
        compiler_params=_cparams(2), name="peer_main",
    )(xnt, thr, a, s2, b, wts["peer_u"], wts["peer_vt"], h, wts["g_final"])


def _peer(x2d, mla, dsa, wts):
    n = x2d.shape[0]
    tc = min(512, n)
    h, xnt, thr, a, s2, b = _peer_prep(x2d, mla, dsa, wts, tc)
    return _peer_main(xnt, thr, a, s2, b, h, wts, tc, ni=4)


def _pad_cols(w, n):
    return jnp.pad(w, ((0, 0), (0, n - w.shape[1])))


def _swap_halves(w):
    half = w.shape[-1] // 2
    return jnp.concatenate([w[..., half:], w[..., :half]], axis=-1)


def _prep_weights(g_attn, w_in, g_q, w_uq, g_kv, w_uk, w_uv, w_out, g_ffn, w_pq, peer_keys, peer_u, peer_v, g_final):
    w_cq, w_ckv, w_kr = w_in[:, 0:384], w_in[:, 384:640], w_in[:, 640:672]
    w_qb, w_kb, w_vb = w_in[:, 672:1184], w_in[:, 1184:1312], w_in[:, 1312:1440]
    w_qi, w_ki, w_wi = w_in[:, 1440:1952], w_in[:, 1952:2016], w_in[:, 2016:2024]
    qi3 = w_qi.reshape(D_MODEL, IDX_HEADS, IDX_DIM)
    qi_partner = jnp.concatenate([_swap_halves(qi3[..., :IDX_ROPE]), jnp.zeros_like(qi3[..., IDX_ROPE:])], axis=-1)
    w1 = jnp.concatenate([
        w_cq, w_ckv, w_qb, w_kb, w_vb, w_qi, qi_partner.reshape(D_MODEL, IDX_HEADS * IDX_DIM),
        _pad_cols(w_kr, LANES), _pad_cols(_swap_halves(w_kr), LANES),
        _pad_cols(w_ki, LANES), _pad_cols(_swap_halves(w_ki[:, :IDX_ROPE]), LANES),
        _pad_cols(w_wi, LANES)], axis=1).astype(BF16)
    uq3 = w_uq.reshape(Q_LORA, MLA_HEADS, MLA_NOPE + MLA_ROPE)
    rope3 = uq3[..., MLA_NOPE:]
    pad3 = lambda w: jnp.pad(w, ((0, 0), (0, 0), (0, LANES - MLA_ROPE))).reshape(Q_LORA, MLA_HEADS * LANES)
    wuq = jnp.concatenate([uq3[..., :MLA_NOPE].reshape(Q_LORA, MLA_HEADS * MLA_NOPE),
                           pad3(rope3), pad3(_swap_halves(rope3))], axis=1).astype(BF16)
    ukt = jnp.transpose(w_uk, (1, 2, 0))
    zero = jnp.zeros((MLA_NOPE, KV_LORA), F32)
    wuk = jnp.stack([jnp.concatenate([jnp.concatenate([ukt[2 * p], zero], axis=1),
                                      jnp.concatenate([zero, ukt[2 * p + 1]], axis=1)], axis=0)
                     for p in range(MLA_HEADS // 2)]).astype(BF16)
    return dict(
        g_attn=g_attn.reshape(1, -1), w1=w1, g_q=g_q.reshape(1, -1), wuq=wuq, g_kv=g_kv.reshape(1, -1), wuk=wuk,
        wuv=jnp.transpose(w_uv, (1, 0, 2)).astype(BF16), w_out=w_out.astype(BF16), g_ffn=g_ffn.reshape(1, -1),
        wpq_t=w_pq.T.astype(BF16),
        peer_keys=peer_keys.reshape(PEER_HEADS * 2, PEER_NKEYS, PEER_HALF).astype(BF16),
        peer_u=peer_u.astype(BF16), peer_vt=peer_v.T.astype(BF16), g_final=g_final.reshape(1, -1))


def _rope_tables(pos):
    half = MLA_ROPE // 2
    inv = ROPE_THETA ** (-jnp.arange(half, dtype=F32) / half)
    ang = pos.astype(F32)[:, None] * inv
    cos, sin = jnp.cos(ang), jnp.sin(ang)
    c32 = jnp.concatenate([cos, cos], axis=1)
    s32 = jnp.concatenate([-sin, sin], axis=1)
    n = pos.shape[0]
    one, zero = jnp.ones((n, 32), F32), jnp.zeros((n, 32), F32)
    ca = jnp.concatenate([c32, one, one, one], axis=1)
    sa = jnp.concatenate([s32, zero, zero, zero], axis=1)
    cb = jnp.concatenate([c32, one, c32, one], axis=1)
    sb = jnp.concatenate([s32, zero, s32, zero], axis=1)
    return ca, sa, cb, sb


def _pick_tile(n, choices):
    for c in choices:
        if n % c == 0:
            return c
    raise ValueError(f"no tile in {choices} divides {n}")


def kernel(x_prompt, x_sample, cache_ckv, cache_krope, cache_k, cache_v, cache_kidx, page_table, rel_bias, g_attn,
           w_in, g_q, w_uq, g_kv, w_uk, w_uv, w_out, g_ffn, w_pq, peer_keys, peer_u, peer_v, g_final):
    assert g_attn.shape[0] == 1, "single-layer kernel"
    b, s, d = x_prompt.shape
    bs, ts, _ = x_sample.shape
    n_pages = page_table.shape[1]
    past = n_pages * PAGE
    assert s % KV_CHUNK == 0 and ts <= 8 and (bs * ts) % LANES == 0
    wts = _prep_weights(g_attn[0], w_in[0], g_q[0], w_uq[0], g_kv[0], w_uk[0], w_uv[0], w_out[0], g_ffn[0],
                        w_pq[0], peer_keys[0], peer_u[0], peer_v[0], g_final)
    bias_p, bias_s, bias_f = _bias_tables(rel_bias)

    xp = x_prompt.reshape(b * s, d)
    tm = _pick_tile(s, (512, 256))
    (ckv_p, kr_p, kb_p, vb_p, ki_p, kcat, kb_bf, vb_bf, ki_bf, qcat, qb_hm, qi_hm, wi_p) = _inproj(
        xp, _rope_tables(jnp.arange(s)), s // tm, wts, tm)
    mla_p = _mla_prompt(qcat, kcat, wts["wuv"], b, s)
    v_t = jnp.transpose(vb_bf.reshape(b, s // KV_CHUNK, KV_CHUNK, LANES), (0, 1, 3, 2))
    dsa_p = _dsa_prompt(qi_hm, wi_p.T, ki_bf, qb_hm, kb_bf, v_t, bias_p, b, s)
    y_p = _peer(xp, mla_p, dsa_p, wts)

    ns = bs * ts
    xs = x_sample.reshape(ns, d)
    pos_s = past + jnp.tile(jnp.arange(ts), bs)
    (ckv_s, kr_s, kb_s, vb_s, ki_s, kcat_s, kb_sbf, vb_sbf, ki_sbf, qcat_s, qb_shm, qi_shm, wi_s) = _inproj(
        xs, _rope_tables(pos_s), 1, wts, _pick_tile(ns, (512, 256, 128)))
    pt_flat = page_table.reshape(-1).astype(I32)
    pp = _pick_tile(n_pages, (8, 4, 2, 1))
    q5 = qcat_s.reshape(MLA_HEADS, bs, ts, KCAT).transpose(1, 0, 2, 3).reshape(bs, MLA_HEADS * ts, KCAT)
    qi_s = qi_shm.reshape(IDX_HEADS, bs, ts, IDX_DIM).transpose(1, 2, 0, 3).reshape(bs, ts * IDX_HEADS, IDX_DIM)
    wi_col = wi_s.reshape(bs, ts * IDX_HEADS, 1)
    pad_new = lambda a: jnp.pad(a.reshape(bs, ts, a.shape[-1]), ((0, 0), (0, PAGE - ts), (0, 0)))
    olat, sc_past, sc_new = _sample1(
        pt_flat, q5[..., :KV_LORA], q5[..., KV_LORA:KV_LORA + MLA_ROPE], qi_s, wi_col,
        pad_new(kcat_s[:, :KV_LORA]), pad_new(kcat_s[:, KV_LORA:KV_LORA + MLA_ROPE]), pad_new(ki_sbf),
        cache_ckv[0], cache_krope[0], cache_kidx[0], n_pages, pp)
    olat_hm = olat.reshape(bs, MLA_HEADS, ts, KV_LORA).transpose(1, 0, 2, 3).reshape(MLA_HEADS, ns, KV_LORA)
    mla_s = _mla_out(olat_hm.astype(BF16), wts["wuv"])
    topk_s = min(IDX_TOPK_MAX, (past + ts) // 4)
    kp = -(-(past + PAGE) // KV_CHUNK) * KV_CHUNK
    sc_all = jnp.concatenate([sc_past, sc_new], axis=2).reshape(ns, past + PAGE)
    sc_t = jnp.pad(sc_all, ((0, 0), (0, kp - past - PAGE)), constant_values=-jnp.inf).T
    tau_s, cut_s = _sample_select(sc_t, topk_s)
    qb_s = qb_shm.reshape(DSA_KV_HEADS, DSA_REP, bs, ts, DSA_HEAD_DIM).transpose(2, 0, 1, 3, 4)
    qb_s = qb_s.reshape(bs, DSA_KV_HEADS, DSA_REP * ts, DSA_HEAD_DIM)
    bias_s4 = bias_s[:, :, :ts, :].reshape(2, DSA_KV_HEADS, DSA_REP * ts, PAGE)
    bias_f4 = bias_f[:, :ts, :].reshape(DSA_KV_HEADS, DSA_REP * ts, PAGE)
    n_pool = cache_k.shape[1]
    o_s = _sample3(pt_flat, qb_s, sc_past, sc_new, tau_s.reshape(bs, ts, 1), cut_s.reshape(bs, ts, 1),
                   pad_new(kb_sbf), pad_new(vb_sbf), bias_s4, bias_f4,
                   cache_k[0].reshape(n_pool, PAGE, LANES), cache_v[0].reshape(n_pool, PAGE, LANES), n_pages, pp)
    dsa_s = o_s.reshape(bs, DSA_KV_HEADS, DSA_REP, ts, DSA_HEAD_DIM).transpose(0, 3, 1, 2, 4)
    dsa_s = dsa_s.reshape(ns, DSA_HEADS * DSA_HEAD_DIM).astype(BF16)
    y_s = _peer(xs, mla_s, dsa_s, wts)

    kvh = (DSA_KV_HEADS, DSA_HEAD_DIM)
    return (y_p.reshape(b, s, d), y_s.reshape(bs, ts, d),
            ckv_p.reshape(1, b, s, KV_LORA), kr_p.reshape(1, b, s, MLA_ROPE),
            kb_p.reshape((1, b, s) + kvh), vb_p.reshape((1, b, s) + kvh), ki_p.reshape(1, b, s, IDX_DIM),
            ckv_s.reshape(1, bs, ts, KV_LORA), kr_s.reshape(1, bs, ts, MLA_ROPE),
            kb_s.reshape((1, bs, ts) + kvh), vb_s.reshape((1, bs, ts) + kvh), ki_s.reshape(1, bs, ts, IDX_DIM))
```

```python
import functools
import math

import jax
import jax.numpy as jnp
import numpy as np
from jax import lax
from jax.experimental import pallas as pl
from jax.experimental.pallas import tpu as pltpu

F32 = jnp.float32
BF16 = jnp.bfloat16
I32 = jnp.int32

D_MODEL = 1024
PAGE = 128
MLA_HEADS = 8
MLA_NOPE = 64
MLA_ROPE = 32
MLA_V = 64
Q_LORA = 384
KV_LORA = 256
MLA_SCALE = (MLA_NOPE + MLA_ROPE) ** -0.5
DSA_HEADS = 8
DSA_KV_HEADS = 2
DSA_REP = DSA_HEADS // DSA_KV_HEADS
DSA_HEAD_DIM = 64
DSA_SCALE = DSA_HEAD_DIM ** -0.5
IDX_HEADS = 8
IDX_DIM = 64
IDX_ROPE = 32
IDX_TOPK_MAX = 256
IDX_W_SCALE = (IDX_HEADS * IDX_DIM) ** -0.5
REL_BUCKETS = 32
REL_MAX_DIST = 128
PEER_HEADS = 8
PEER_NKEYS = 128
PEER_EXPERTS = PEER_NKEYS * PEER_NKEYS
PEER_HALF = 128
PEER_TOPK = 16
ROPE_THETA = 10000.0
NORM_EPS = 1e-6

LANES = 128
MXU_N = 256
NEG = -1e30
INT_MIN = -(2 ** 31)
KCAT = KV_LORA + LANES
KV_CHUNK = 256
Q_TILE = 128

_C_CQ, _C_CKV, _C_QB, _C_KB, _C_VB = 0, 384, 640, 1152, 1280
_C_QI, _C_QIP, _C_KR, _C_KRP, _C_KI, _C_KIP, _C_WI, _C_END = 1408, 1920, 2432, 2560, 2688, 2816, 2944, 3072
_VMEM_LIMIT = 56 * 1024 * 1024


def _cparams(n_axes):
    return pltpu.CompilerParams(dimension_semantics=("arbitrary",) * n_axes, vmem_limit_bytes=_VMEM_LIMIT)


def _dot(a, b):
    return jnp.dot(a, b, preferred_element_type=F32)


def _dot_nt(a, b):
    return lax.dot_general(a, b, (((1,), (1,)), ((), ())), preferred_element_type=F32)


def _rms(x, g):
    return x * lax.rsqrt(jnp.mean(x * x, axis=-1, keepdims=True) + NORM_EPS) * g


def _sort_key(x):
    x = jnp.where(x == 0.0, 0.0, x)
    bits = pltpu.bitcast(x, I32)
    return bits ^ ((bits >> 31) & 0x7FFFFFFF)


def _bucket_starts():
    max_exact = REL_BUCKETS // 2
    n = np.arange(0, 2 * REL_MAX_DIST, dtype=np.int64)
    nf = np.maximum(n, 1).astype(np.float32)
    large = max_exact + (np.log(nf / np.float32(max_exact)) / np.float32(math.log(REL_MAX_DIST / max_exact))
                         * np.float32(REL_BUCKETS - max_exact)).astype(np.int32)
    large = np.minimum(large, REL_BUCKETS - 1)
    bucket = np.where(n < max_exact, n, large)
    starts = []
    for k in range(REL_BUCKETS):
        hit = np.nonzero(bucket >= k)[0]
        starts.append(int(hit[0]) if hit.size else int(n[-1]) + 1)
    return starts


_BUCKET_START = _bucket_starts()


def _bias_kernel(rb_ref, bp_ref, bs_ref, bf_ref):
    def bias_of(n, h):
        b = jnp.full(n.shape, rb_ref[REL_BUCKETS - 1, h], F32)
        for k in range(REL_BUCKETS - 2, -1, -1):
            b = jnp.where(n < _BUCKET_START[k + 1], rb_ref[k, h], b)
        return b

    s_i = lax.broadcasted_iota(I32, (KV_CHUNK, Q_TILE), 0)
    t_i = lax.broadcasted_iota(I32, (KV_CHUNK, Q_TILE), 1)
    for w in range(4):
        n = jnp.maximum(w * Q_TILE + t_i - s_i, 0)
        for h in range(DSA_HEADS):
            bp_ref[w, h] = bias_of(n, h)
    t_s = lax.broadcasted_iota(I32, (8, PAGE), 0)
    u_s = lax.broadcasted_iota(I32, (8, PAGE), 1)
    for w in range(2):
        n = jnp.maximum((1 - w) * PAGE + t_s - u_s, 0)
        for h in range(DSA_HEADS):
            bs_ref[w, h] = bias_of(n, h)
    for h in range(DSA_HEADS):
        bf_ref[h] = jnp.full((8, PAGE), rb_ref[REL_BUCKETS - 1, h], F32)


def _bias_tables(rel_bias):
    return pl.pallas_call(
        _bias_kernel,
        out_shape=(jax.ShapeDtypeStruct((4, DSA_HEADS, KV_CHUNK, Q_TILE), F32),
                   jax.ShapeDtypeStruct((2, DSA_HEADS, 8, PAGE), F32),
                   jax.ShapeDtypeStruct((DSA_HEADS, 8, PAGE), F32)),
        in_specs=[pl.BlockSpec(memory_space=pltpu.SMEM)],
        name="bias_tables",
    )(rel_bias)


def _inproj_kernel(x_ref, ga_ref, w1_ref, gq_ref, wuq_ref, gkv_ref, wuk_ref, ca_ref, sa_ref, cb_ref, sb_ref,
                   ckv_ref, krope_ref, kb_ref, vb_ref, ki_ref,
                   kcat_ref, kbbf_ref, vbbf_ref, kibf_ref, qcat_ref, qb_ref, qi_ref, wi_ref):
    xn = _rms(x_ref[...], ga_ref[...]).astype(BF16)

    def proj(lo, hi):
        return _dot(xn, w1_ref[:, lo:hi])

    ca, sa, cb, sb = ca_ref[...], sa_ref[...], cb_ref[...], sb_ref[...]

    ckv = _rms(proj(_C_CKV, _C_QB), gkv_ref[...])
    ckv_ref[...] = ckv
    kcat_ref[:, 0:KV_LORA] = ckv.astype(BF16)
    kr = proj(_C_KR, _C_KRP) * ca + proj(_C_KRP, _C_KI) * sa
    krope_ref[...] = kr[:, :MLA_ROPE]
    kcat_ref[:, KV_LORA:KCAT] = kr.astype(BF16)
    ki = proj(_C_KI, _C_KIP) * ca + proj(_C_KIP, _C_WI) * sa
    ki_ref[...] = ki[:, :IDX_DIM]
    kibf_ref[...] = ki[:, :IDX_DIM].astype(BF16)
    kb = proj(_C_KB, _C_VB)
    kb_ref[...] = kb
    kbbf_ref[...] = kb.astype(BF16)
    vb = proj(_C_VB, _C_QI)
    vb_ref[...] = vb
    vbbf_ref[...] = vb.astype(BF16)
    wi_ref[...] = proj(_C_WI, _C_END)[:, :IDX_HEADS] * IDX_W_SCALE

    qb = proj(_C_QB, _C_KB)
    for h in range(DSA_HEADS):
        qb_ref[h] = qb[:, h * DSA_HEAD_DIM:(h + 1) * DSA_HEAD_DIM].astype(BF16)
    qi = proj(_C_QI, _C_QIP)
    qip = proj(_C_QIP, _C_KR)
    for s in range(4):
        slab = qi[:, s * LANES:(s + 1) * LANES] * cb + qip[:, s * LANES:(s + 1) * LANES] * sb
        qi_ref[2 * s] = slab[:, :IDX_DIM].astype(BF16)
        qi_ref[2 * s + 1] = slab[:, IDX_DIM:].astype(BF16)

    cq = _rms(proj(_C_CQ, _C_CKV), gq_ref[...]).astype(BF16)
    n_nope = MLA_HEADS * MLA_NOPE
    n_pad = MLA_HEADS * LANES
    nope = _dot(cq, wuq_ref[:, 0:n_nope]).astype(BF16)
    for p in range(MLA_HEADS // 2):
        ql = _dot(nope[:, p * LANES:(p + 1) * LANES], wuk_ref[p])
        qcat_ref[2 * p, :, 0:KV_LORA] = ql[:, :KV_LORA].astype(BF16)
        qcat_ref[2 * p + 1, :, 0:KV_LORA] = ql[:, KV_LORA:].astype(BF16)
    for h in range(MLA_HEADS):
        lo = n_nope + h * LANES
        qr = _dot(cq, wuq_ref[:, lo:lo + LANES]) * ca + _dot(cq, wuq_ref[:, lo + n_pad:lo + n_pad + LANES]) * sa
        qcat_ref[h, :, KV_LORA:KCAT] = qr.astype(BF16)


def _inproj(x2d, tabs, tab_blocks, wts, tm):
    n = x2d.shape[0]
    const2 = lambda i: (0, 0)
    const3 = lambda i: (0, 0, 0)
    row = lambda i: (i, 0)
    tab = lambda i: (i % tab_blocks, 0)
    hm = lambda i: (0, i, 0)
    in_specs = [
        pl.BlockSpec((tm, D_MODEL), row),
        pl.BlockSpec((1, D_MODEL), const2),
        pl.BlockSpec((D_MODEL, _C_END), const2),
        pl.BlockSpec((1, Q_LORA), const2),
        pl.BlockSpec(wts["wuq"].shape, const2),
        pl.BlockSpec((1, KV_LORA), const2),
        pl.BlockSpec(wts["wuk"].shape, const3),
    ] + [pl.BlockSpec((tm, LANES), tab)] * 4
    out_shape = (
        jax.ShapeDtypeStruct((n, KV_LORA), F32), jax.ShapeDtypeStruct((n, MLA_ROPE), F32),
        jax.ShapeDtypeStruct((n, LANES), F32), jax.ShapeDtypeStruct((n, LANES), F32),
        jax.ShapeDtypeStruct((n, IDX_DIM), F32),
        jax.ShapeDtypeStruct((n, KCAT), BF16), jax.ShapeDtypeStruct((n, LANES), BF16),
        jax.ShapeDtypeStruct((n, LANES), BF16), jax.ShapeDtypeStruct((n, IDX_DIM), BF16),
        jax.ShapeDtypeStruct((MLA_HEADS, n, KCAT), BF16),
        jax.ShapeDtypeStruct((DSA_HEADS, n, DSA_HEAD_DIM), BF16),
        jax.ShapeDtypeStruct((IDX_HEADS, n, IDX_DIM), BF16),
        jax.ShapeDtypeStruct((n, IDX_HEADS), F32),
    )
    out_specs = (
        pl.BlockSpec((tm, KV_LORA), row), pl.BlockSpec((tm, MLA_ROPE), row),
        pl.BlockSpec((tm, LANES), row), pl.BlockSpec((tm, LANES), row), pl.BlockSpec((tm, IDX_DIM), row),
        pl.BlockSpec((tm, KCAT), row), pl.BlockSpec((tm, LANES), row),
        pl.BlockSpec((tm, LANES), row), pl.BlockSpec((tm, IDX_DIM), row),
        pl.BlockSpec((MLA_HEADS, tm, KCAT), hm),
        pl.BlockSpec((DSA_HEADS, tm, DSA_HEAD_DIM), hm),
        pl.BlockSpec((IDX_HEADS, tm, IDX_DIM), hm),
        pl.BlockSpec((tm, IDX_HEADS), row),
    )
    return pl.pallas_call(
        _inproj_kernel, grid=(n // tm,), in_specs=in_specs, out_specs=out_specs, out_shape=out_shape,
        compiler_params=_cparams(1), name="inproj",
    )(x2d, wts["g_attn"], wts["w1"], wts["g_q"], wts["wuq"], wts["g_kv"], wts["wuk"], *tabs)


def _rep(x, width):
    k = width // LANES
    return x if k == 1 else jnp.concatenate([x] * k, axis=1)


def _mla_prompt_kernel(q_ref, k_ref, wuv_ref, o_ref, m_scr, l_scr, acc_scr):
    j = pl.program_id(1)
    rows = MLA_HEADS * Q_TILE
    q = q_ref[...].reshape(rows, KCAT)
    m_scr[...] = jnp.full(m_scr.shape, NEG, F32)
    l_scr[...] = jnp.zeros(l_scr.shape, F32)
    acc_scr[...] = jnp.zeros(acc_scr.shape, F32)
    t_row = j * Q_TILE + lax.broadcasted_iota(I32, (rows, KV_CHUNK), 0) % Q_TILE
    u_col = lax.broadcasted_iota(I32, (rows, KV_CHUNK), 1)

    def body(c, carry):
        k = k_ref[pl.ds(pl.multiple_of(c * KV_CHUNK, KV_CHUNK), KV_CHUNK), :]
        s = _dot_nt(q, k) * MLA_SCALE
        s = jnp.where(c * KV_CHUNK + u_col <= t_row, s, NEG)
        m_prev = m_scr[...]
        m_new = jnp.maximum(m_prev, jnp.max(s, axis=1, keepdims=True))
        alpha = jnp.exp(m_prev - m_new)
        p = jnp.exp(s - _rep(m_new, KV_CHUNK))
        l_scr[...] = alpha * l_scr[...] + jnp.sum(p, axis=1, keepdims=True)
        acc_scr[...] = acc_scr[...] * _rep(alpha, KV_LORA) + _dot(p.astype(BF16), k[:, :KV_LORA])
        m_scr[...] = m_new
        return carry

    lax.fori_loop(0, (j * Q_TILE) // KV_CHUNK + 1, body, 0)
    o_lat = (acc_scr[...] / _rep(l_scr[...], KV_LORA)).astype(BF16)
    for h in range(MLA_HEADS):
        o = _dot(o_lat[h * Q_TILE:(h + 1) * Q_TILE], wuv_ref[h])
        o_ref[:, h * MLA_V:(h + 1) * MLA_V] = o.astype(BF16)


def _mla_prompt(qcat, kcat, wuv, b, s):
    nq = s // Q_TILE
    rows = MLA_HEADS * Q_TILE
    return pl.pallas_call(
        _mla_prompt_kernel, grid=(b, nq),
        in_specs=[pl.BlockSpec((MLA_HEADS, Q_TILE, KCAT), lambda bi, j: (0, bi * nq + j, 0)),
                  pl.BlockSpec((None, s, KCAT), lambda bi, j: (bi, 0, 0)),
                  pl.BlockSpec(wuv.shape, lambda bi, j: (0, 0, 0))],
        out_specs=pl.BlockSpec((Q_TILE, MLA_HEADS * MLA_V), lambda bi, j: (bi * nq + j, 0)),
        out_shape=jax.ShapeDtypeStruct((b * s, MLA_HEADS * MLA_V), BF16),
        scratch_shapes=[pltpu.VMEM((rows, LANES), F32), pltpu.VMEM((rows, LANES), F32),
                        pltpu.VMEM((rows, KV_LORA), F32)],
        compiler_params=_cparams(2), name="mla_prompt",
    )(qcat, kcat.reshape(b, s, KCAT), wuv)


def _topk_threshold(key_scr, n_chunks, topk, n_keys_pow2_bits):
    lanes = key_scr.shape[1]
    sub = KV_CHUNK // 8

    def count(pred_fn):
        def body(c, acc):
            off = pl.multiple_of(c * KV_CHUNK, KV_CHUNK)
            k = key_scr[pl.ds(off, KV_CHUNK), :]
            hit = pred_fn(k, c).astype(I32)
            return acc + jnp.sum(hit.reshape(sub, 8, lanes), axis=0)

        acc = lax.fori_loop(0, n_chunks, body, jnp.zeros((8, lanes), I32))
        return jnp.sum(acc, axis=0, keepdims=True)

    def bit_body(i, res):
        cand = res | jnp.left_shift(jnp.int32(1), 31 - i)
        cs = cand ^ INT_MIN
        cnt = count(lambda k, c: k >= cs)
        return jnp.where(cnt >= topk, cand, res)

    res = lax.fori_loop(0, 32, bit_body, jnp.zeros((1, lanes), I32))
    tau = res ^ INT_MIN
    cnt_gt = count(lambda k, c: k > tau)
    cnt_eq = count(lambda k, c: k == tau)
    need = topk - cnt_gt
    row0 = lax.broadcasted_iota(I32, (KV_CHUNK, lanes), 0)
    big = jnp.int32(2 ** 30)

    def cut_search():
        def cbody(i, cur):
            cand = cur | jnp.left_shift(jnp.int32(1), n_keys_pow2_bits - 1 - i)
            f = count(lambda k, c: jnp.where(k == tau, row0 + c * KV_CHUNK, big) < cand)
            return jnp.where(f < need, cand, cur)

        return lax.fori_loop(0, n_keys_pow2_bits, cbody, jnp.zeros((1, lanes), I32))

    cut = lax.cond(jnp.max(cnt_eq - need) > 0, cut_search, lambda: jnp.full((1, lanes), big, I32))
    return tau, cut


def _dsa_prompt_kernel(qi_ref, wt_ref, ki_ref, qb_ref, kb_ref, vt_ref, bias_ref, o_ref,
                       key_scr, mb_scr, tc_scr, m_scr, l_scr, acc_scr, *, topk, idx_bits):
    j = pl.program_id(1)
    n_chunks = (j * Q_TILE) // KV_CHUNK + 1
    t_row = j * Q_TILE + lax.broadcasted_iota(I32, (KV_CHUNK, Q_TILE), 1)
    s_loc = lax.broadcasted_iota(I32, (KV_CHUNK, Q_TILE), 0)
    qi = qi_ref[...].reshape(IDX_HEADS * Q_TILE, IDX_DIM)
    wt = wt_ref[...]

    def score_body(c, carry):
        off = pl.multiple_of(c * KV_CHUNK, KV_CHUNK)
        a = _dot_nt(ki_ref[pl.ds(off, KV_CHUNK), :], qi)
        sc = jnp.zeros((KV_CHUNK, Q_TILE), F32)
        for h in range(IDX_HEADS):
            sc = sc + wt[h:h + 1, :] * jnp.maximum(a[:, h * Q_TILE:(h + 1) * Q_TILE], 0.0)
        key = jnp.where(off + s_loc <= t_row, _sort_key(sc), INT_MIN)
        key_scr[pl.ds(off, KV_CHUNK), :] = key
        return carry

    lax.fori_loop(0, n_chunks, score_body, 0)

    @pl.when((j + 1) * Q_TILE <= topk)
    def _():
        tc_scr[0:1, :] = jnp.full((1, Q_TILE), INT_MIN, I32)
        tc_scr[1:2, :] = jnp.full((1, Q_TILE), -1, I32)

    @pl.when((j + 1) * Q_TILE > topk)
    def _():
        tau, cut = _topk_threshold(key_scr, n_chunks, topk, idx_bits)
        tc_scr[0:1, :] = tau
        tc_scr[1:2, :] = cut

    tau = tc_scr[0:1, :]
    cut = tc_scr[1:2, :]

    def mask_body(c, carry):
        off = pl.multiple_of(c * KV_CHUNK, KV_CHUNK)
        k = key_scr[pl.ds(off, KV_CHUNK), :]
        spos = off + s_loc
        v = jnp.where(k > tau, 0.0, jnp.where(k == tau, jnp.where(spos <= cut, 0.0, NEG), NEG))
        mb_scr[pl.ds(off, KV_CHUNK), :] = jnp.where(spos <= t_row, v, NEG)
        return carry

    lax.fori_loop(0, n_chunks, mask_body, 0)

    m_scr[...] = jnp.full(m_scr.shape, NEG, F32)
    l_scr[...] = jnp.zeros(l_scr.shape, F32)
    acc_scr[...] = jnp.zeros(acc_scr.shape, F32)
    qb = qb_ref[...]

    def att_body(c, carry):
        off = pl.multiple_of(c * KV_CHUNK, KV_CHUNK)
        kb = kb_ref[pl.ds(off, KV_CHUNK), :]
        mb = mb_scr[pl.ds(off, KV_CHUNK), :]
        bidx = jnp.minimum((j * Q_TILE - c * KV_CHUNK) // Q_TILE, 3)
        for g in range(DSA_KV_HEADS):
            kg = kb[:, g * DSA_HEAD_DIM:(g + 1) * DSA_HEAD_DIM]
            qg = qb[g * DSA_REP:(g + 1) * DSA_REP].reshape(DSA_REP * Q_TILE, DSA_HEAD_DIM)
            lg4 = _dot_nt(kg, qg) * DSA_SCALE
            vg = vt_ref[c, g * DSA_HEAD_DIM:(g + 1) * DSA_HEAD_DIM, :]
            for r in range(DSA_REP):
                h = g * DSA_REP + r
                lg = lg4[:, r * Q_TILE:(r + 1) * Q_TILE] + bias_ref[bidx, h] + mb
                m_prev = m_scr[h:h + 1, :]
                m_new = jnp.maximum(m_prev, jnp.max(lg, axis=0, keepdims=True))
                alpha = jnp.exp(m_prev - m_new)
                p = jnp.exp(lg - m_new)
                l_scr[h:h + 1, :] = alpha * l_scr[h:h + 1, :] + jnp.sum(p, axis=0, keepdims=True)
                rs = slice(h * DSA_HEAD_DIM, (h + 1) * DSA_HEAD_DIM)
                acc_scr[rs, :] = alpha * acc_scr[rs, :] + _dot(vg, p.astype(BF16))
                m_scr[h:h + 1, :] = m_new
        return carry

    lax.fori_loop(0, n_chunks, att_body, 0)
    inv = 1.0 / l_scr[...]
    parts = [acc_scr[h * DSA_HEAD_DIM:(h + 1) * DSA_HEAD_DIM, :] * inv[h:h + 1, :] for h in range(DSA_HEADS)]
    o_ref[...] = jnp.concatenate(parts, axis=0).T.astype(BF16)


def _dsa_prompt(qi_hm, wi_t, ki_bf, qb_hm, kb_bf, v_t, bias_p, b, s):
    nq = s // Q_TILE
    topk = min(IDX_TOPK_MAX, s // 4)
    idx_bits = max(1, int(math.ceil(math.log2(s))))
    width = DSA_HEADS * DSA_HEAD_DIM
    kern = functools.partial(_dsa_prompt_kernel, topk=topk, idx_bits=idx_bits)
    return pl.pallas_call(
        kern, grid=(b, nq),
        in_specs=[pl.BlockSpec((IDX_HEADS, Q_TILE, IDX_DIM), lambda bi, j: (0, bi * nq + j, 0)),
                  pl.BlockSpec((IDX_HEADS, Q_TILE), lambda bi, j: (0, bi * nq + j)),
                  pl.BlockSpec((None, s, IDX_DIM), lambda bi, j: (bi, 0, 0)),
                  pl.BlockSpec((DSA_HEADS, Q_TILE, DSA_HEAD_DIM), lambda bi, j: (0, bi * nq + j, 0)),
                  pl.BlockSpec((None, s, LANES), lambda bi, j: (bi, 0, 0)),
                  pl.BlockSpec((None, s // KV_CHUNK, LANES, KV_CHUNK), lambda bi, j: (bi, 0, 0, 0)),
                  pl.BlockSpec(bias_p.shape, lambda bi, j: (0, 0, 0, 0))],
        out_specs=pl.BlockSpec((Q_TILE, width), lambda bi, j: (bi * nq + j, 0)),
        out_shape=jax.ShapeDtypeStruct((b * s, width), BF16),
        scratch_shapes=[pltpu.VMEM((s, Q_TILE), I32), pltpu.VMEM((s, Q_TILE), F32), pltpu.VMEM((8, Q_TILE), I32),
                        pltpu.VMEM((DSA_HEADS, Q_TILE), F32), pltpu.VMEM((DSA_HEADS, Q_TILE), F32),
                        pltpu.VMEM((width, Q_TILE), F32)],
        compiler_params=_cparams(2), name="dsa_prompt",
    )(qi_hm, wi_t, ki_bf.reshape(b, s, IDX_DIM), qb_hm, kb_bf.reshape(b, s, LANES), v_t, bias_p)


def _sample1_kernel(pt_ref, ql_ref, qr_ref, qi_ref, wi_ref, ckvn_ref, krn_ref, kin_ref, *rest, pp):
    ckv_pages, kr_pages, ki_pages = rest[0:pp], rest[pp:2 * pp], rest[2 * pp:3 * pp]
    olat_ref, sc_ref, scn_ref = rest[3 * pp:3 * pp + 3]
    m_scr, l_scr, acc_scr = rest[3 * pp + 3:]
    j = pl.program_id(1)
    last = pl.num_programs(1) - 1
    rows = ql_ref.shape[0]
    n_tok = rows // MLA_HEADS

    @pl.when(j == 0)
    def _():
        m_scr[...] = jnp.full(m_scr.shape, NEG, F32)
        l_scr[...] = jnp.zeros(l_scr.shape, F32)
        acc_scr[...] = jnp.zeros(acc_scr.shape, F32)

    ql, qr, qi, wi = ql_ref[...], qr_ref[...], qi_ref[...], wi_ref[...]

    def attend(kcs, krs, mask):
        s = jnp.concatenate([_dot_nt(ql, kc) + _dot(qr, kr) for kc, kr in zip(kcs, krs)], axis=1) * MLA_SCALE
        if mask is not None:
            s = jnp.where(mask, s, NEG)
        m_prev = m_scr[...]
        m_new = jnp.maximum(m_prev, jnp.max(s, axis=1, keepdims=True))
        alpha = jnp.exp(m_prev - m_new)
        p = jnp.exp(s - _rep(m_new, s.shape[1])).astype(BF16)
        l_scr[...] = alpha * l_scr[...] + jnp.sum(p.astype(F32), axis=1, keepdims=True)
        pv = _dot(p[:, 0:PAGE], kcs[0])
        for k in range(1, len(kcs)):
            pv = pv + _dot(p[:, k * PAGE:(k + 1) * PAGE], kcs[k])
        acc_scr[...] = acc_scr[...] * _rep(alpha, KV_LORA) + pv
        m_scr[...] = m_new

    def index(kidx_t):
        a = jnp.maximum(_dot(qi, kidx_t), 0.0) * wi
        return jnp.sum(a.reshape(n_tok, IDX_HEADS, PAGE), axis=1)

    attend([r[...].astype(BF16) for r in ckv_pages], [r[...].astype(BF16) for r in kr_pages], None)
    for k in range(pp):
        sc_ref[:, k * PAGE:(k + 1) * PAGE] = index(ki_pages[k][...].astype(BF16))

    @pl.when(j == last)
    def _():
        t_r = lax.broadcasted_iota(I32, (rows, PAGE), 0) % n_tok
        u_c = lax.broadcasted_iota(I32, (rows, PAGE), 1)
        attend([ckvn_ref[...]], [krn_ref[...]], u_c <= t_r)
        t4 = lax.broadcasted_iota(I32, (n_tok, PAGE), 0)
        u4 = lax.broadcasted_iota(I32, (n_tok, PAGE), 1)
        scn_ref[...] = jnp.where(u4 <= t4, index(kin_ref[...]), -jnp.inf)
        olat_ref[...] = acc_scr[...] / _rep(l_scr[...], KV_LORA)


def _page_specs(shape_tail, n_pages, pp):
    nd = len(shape_tail)

    def make(k):
        return pl.BlockSpec((None,) + shape_tail,
                            lambda bi, j, pt: (pt[bi * n_pages + j * pp + k],) + (0,) * nd)

    return [make(k) for k in range(pp)]


def _sample1(pt_flat, ql, qr, qi, wi, ckvn, krn_t, kin_t, c_ckv, c_kr_t, c_ki_t, n_pages, pp):
    bs, rows = ql.shape[0], ql.shape[1]
    n_tok = rows // MLA_HEADS
    per_b = lambda tail: pl.BlockSpec((None,) + tail, lambda bi, j, pt: (bi,) + (0,) * len(tail))
    in_specs = [per_b((rows, KV_LORA)), per_b((rows, MLA_ROPE)), per_b((rows, IDX_DIM)), per_b((rows, 1)),
                per_b((PAGE, KV_LORA)), per_b((MLA_ROPE, PAGE)), per_b((IDX_DIM, PAGE))]
    in_specs += _page_specs((PAGE, KV_LORA), n_pages, pp) + _page_specs((MLA_ROPE, PAGE), n_pages, pp)
    in_specs += _page_specs((IDX_DIM, PAGE), n_pages, pp)
    out_specs = (per_b((rows, KV_LORA)),
                 pl.BlockSpec((None, n_tok, pp * PAGE), lambda bi, j, pt: (bi, 0, j)),
                 per_b((n_tok, PAGE)))
    out_shape = (jax.ShapeDtypeStruct((bs, rows, KV_LORA), F32),
                 jax.ShapeDtypeStruct((bs, n_tok, n_pages * PAGE), F32),
                 jax.ShapeDtypeStruct((bs, n_tok, PAGE), F32))
    grid_spec = pltpu.PrefetchScalarGridSpec(
        num_scalar_prefetch=1, grid=(bs, n_pages // pp), in_specs=in_specs, out_specs=out_specs,
        scratch_shapes=[pltpu.VMEM((rows, LANES), F32), pltpu.VMEM((rows, LANES), F32),
                        pltpu.VMEM((rows, KV_LORA), F32)])
    return pl.pallas_call(
        functools.partial(_sample1_kernel, pp=pp), grid_spec=grid_spec, out_shape=out_shape,
        compiler_params=_cparams(2), name="sample_mla_index",
    )(pt_flat, ql, qr, qi, wi, ckvn, krn_t, kin_t, *([c_ckv] * pp), *([c_kr_t] * pp), *([c_ki_t] * pp))


def _mla_out_kernel(o_ref, wuv_ref, out_ref):
    for h in range(MLA_HEADS):
        out_ref[:, h * MLA_V:(h + 1) * MLA_V] = _dot(o_ref[h], wuv_ref[h]).astype(BF16)


def _mla_out(olat_hm, wuv):
    n = olat_hm.shape[1]
    return pl.pallas_call(
        _mla_out_kernel, out_shape=jax.ShapeDtypeStruct((n, MLA_HEADS * MLA_V), BF16), name="sample_mla_out",
    )(olat_hm, wuv)


def _sample_select_kernel(sc_ref, tau_ref, cut_ref, key_scr, *, topk, idx_bits):
    n_chunks = sc_ref.shape[0] // KV_CHUNK

    def kbody(c, carry):
        off = pl.multiple_of(c * KV_CHUNK, KV_CHUNK)
        key_scr[pl.ds(off, KV_CHUNK), :] = _sort_key(sc_ref[pl.ds(off, KV_CHUNK), :])
        return carry

    lax.fori_loop(0, n_chunks, kbody, 0)
    tau, cut = _topk_threshold(key_scr, n_chunks, topk, idx_bits)
    tau_ref[...] = tau
    cut_ref[...] = cut


def _sample_select(sc_t, topk):
    kp, ns = sc_t.shape
    lt = min(LANES, ns)
    idx_bits = max(1, int(math.ceil(math.log2(kp))))
    kern = functools.partial(_sample_select_kernel, topk=topk, idx_bits=idx_bits)
    return pl.pallas_call(
        kern, grid=(ns // lt,),
        in_specs=[pl.BlockSpec((kp, lt), lambda i: (0, i))],
        out_specs=(pl.BlockSpec((1, lt), lambda i: (0, i)), pl.BlockSpec((1, lt), lambda i: (0, i))),
        out_shape=(jax.ShapeDtypeStruct((1, ns), I32), jax.ShapeDtypeStruct((1, ns), I32)),
        scratch_shapes=[pltpu.VMEM((kp, lt), I32)],
        compiler_params=_cparams(1), name="sample_select",
    )(sc_t)


def _sample3_kernel(pt_ref, qb_ref, sc_ref, scn_ref, tau_ref, cut_ref, kn_ref, vn_ref, bs_ref, bf_ref, *rest,
                    pp, past):
    k_pages, v_pages = rest[0:pp], rest[pp:2 * pp]
    o_ref = rest[2 * pp]
    m_scr, l_scr, acc_scr = rest[2 * pp + 1:]
    j = pl.program_id(1)
    last = pl.num_programs(1) - 1
    n_tok = sc_ref.shape[0]

    @pl.when(j == 0)
    def _():
        m_scr[...] = jnp.full(m_scr.shape, NEG, F32)
        l_scr[...] = jnp.zeros(l_scr.shape, F32)
        acc_scr[...] = jnp.zeros(acc_scr.shape, F32)

    tau, cut = tau_ref[...], cut_ref[...]

    def mask_bias(sc, base):
        k = _sort_key(sc)
        spos = base + lax.broadcasted_iota(I32, sc.shape, 1)
        return jnp.where(k > tau, 0.0, jnp.where(k == tau, jnp.where(spos <= cut, 0.0, NEG), NEG))

    def attend(kts, vts, mb4, biases):
        mb = jnp.concatenate([mb4] * DSA_REP, axis=0)
        for g in range(DSA_KV_HEADS):
            rs = slice(g * DSA_HEAD_DIM, (g + 1) * DSA_HEAD_DIM)
            lg = jnp.concatenate([_dot(qb_ref[g], kt[rs, :]) for kt in kts], axis=1) * DSA_SCALE
            lg = lg + jnp.concatenate([b[g] for b in biases], axis=1) + mb
            m_prev = m_scr[g]
            m_new = jnp.maximum(m_prev, jnp.max(lg, axis=1, keepdims=True))
            alpha = jnp.exp(m_prev - m_new)
            p = jnp.exp(lg - _rep(m_new, lg.shape[1])).astype(BF16)
            l_scr[g] = alpha * l_scr[g] + jnp.sum(p.astype(F32), axis=1, keepdims=True)
            pv = _dot_nt(p[:, 0:PAGE], vts[0][rs, :])
            for k in range(1, len(kts)):
                pv = pv + _dot_nt(p[:, k * PAGE:(k + 1) * PAGE], vts[k][rs, :])
            acc_scr[g] = acc_scr[g] * alpha[:, :DSA_HEAD_DIM] + pv
            m_scr[g] = m_new

    far = bf_ref[...]
    biases = [far] * (pp - 1) + [jnp.where(j == last, bs_ref[0], far)]
    attend([r[...].astype(BF16) for r in k_pages], [r[...].astype(BF16) for r in v_pages],
           mask_bias(sc_ref[...], j * (pp * PAGE)), biases)

    @pl.when(j == last)
    def _():
        attend([kn_ref[...]], [vn_ref[...]], mask_bias(scn_ref[...], past), [bs_ref[1]])
        for g in range(DSA_KV_HEADS):
            o_ref[g] = acc_scr[g] / l_scr[g][:, :DSA_HEAD_DIM]


def _sample3(pt_flat, qb, sc, scn, tau, cut, kn_t, vn_t, bias_s, bias_f, c_kt, c_vt, n_pages, pp):
    bs, n_tok = sc.shape[0], sc.shape[1]
    rows = DSA_REP * n_tok
    per_b = lambda tail: pl.BlockSpec((None,) + tail, lambda bi, j, pt: (bi,) + (0,) * len(tail))
    const = lambda shape: pl.BlockSpec(shape, lambda bi, j, pt: (0,) * len(shape))
    in_specs = [per_b((DSA_KV_HEADS, rows, DSA_HEAD_DIM)),
                pl.BlockSpec((None, n_tok, pp * PAGE), lambda bi, j, pt: (bi, 0, j)),
                per_b((n_tok, PAGE)), per_b((n_tok, 1)), per_b((n_tok, 1)),
                per_b((LANES, PAGE)), per_b((LANES, PAGE)),
                const(bias_s.shape), const(bias_f.shape)]
    in_specs += _page_specs((LANES, PAGE), n_pages, pp) + _page_specs((LANES, PAGE), n_pages, pp)
    grid_spec = pltpu.PrefetchScalarGridSpec(
        num_scalar_prefetch=1, grid=(bs, n_pages // pp), in_specs=in_specs,
        out_specs=per_b((DSA_KV_HEADS, rows, DSA_HEAD_DIM)),
        scratch_shapes=[pltpu.VMEM((DSA_KV_HEADS, rows, LANES), F32), pltpu.VMEM((DSA_KV_HEADS, rows, LANES), F32),
                        pltpu.VMEM((DSA_KV_HEADS, rows, DSA_HEAD_DIM), F32)])
    kern = functools.partial(_sample3_kernel, pp=pp, past=n_pages * PAGE)
    return pl.pallas_call(
        kern, grid_spec=grid_spec,
        out_shape=jax.ShapeDtypeStruct((bs, DSA_KV_HEADS, rows, DSA_HEAD_DIM), F32),
        compiler_params=_cparams(2), name="sample_dsa",
    )(pt_flat, qb, sc, scn, tau, cut, kn_t, vn_t, bias_s, bias_f, *([c_kt] * pp), *([c_vt] * pp))


_N_EXTRACT = PEER_TOPK + 1


def _extract_top(cur, n):
    vals = []
    for _ in range(n):
        m = jnp.max(cur, axis=0, keepdims=True)
        vals.append(m)
        cur = jnp.where(cur == m, -jnp.inf, cur)
    return vals


def _peer_prep_kernel(x_ref, mla_ref, dsa_ref, wo_ref, g_ref, wpq_ref, keys_ref,
                      h_ref, xnt_ref, thr_ref, a_ref, s2_ref, b_ref):
    half = wo_ref.shape[0] // 2
    h = x_ref[...] + _dot(mla_ref[...], wo_ref[0:half, :]) + _dot(dsa_ref[...], wo_ref[half:, :])
    h_ref[...] = h
    xnt = _rms(h, g_ref[...]).T.astype(BF16)
    xnt_ref[...] = xnt
    tc = xnt.shape[1]
    r8 = lax.broadcasted_iota(I32, (8, tc), 0)
    for hh in range(PEER_HEADS):
        scores, tops = [], []
        for p in range(2):
            hp = hh * 2 + p
            qt = _dot(wpq_ref[hp * PEER_HALF:(hp + 1) * PEER_HALF, :], xnt)
            s = _dot(keys_ref[hp], qt.astype(BF16))
            scores.append(s)
            tops.append(_extract_top(s, _N_EXTRACT))
        sv1, sv2 = tops
        sv2_16 = jnp.concatenate(sv2[:PEER_TOPK], axis=0)
        sv2_8 = sv2_16[:8]
        blocks = [sv1[0] + sv2_16]
        for r1 in range(1, 8):
            blocks.append(jnp.where(r8 < PEER_TOPK // (r1 + 1), sv1[r1] + sv2_8, -jnp.inf))
        blocks.append(jnp.concatenate(sv1[8:PEER_TOPK], axis=0) + sv2[0])
        extra = jnp.where(r8 == 0, sv1[0] + sv2[PEER_TOPK],
                          jnp.where(r8 == 1, sv1[PEER_TOPK] + sv2[0], -jnp.inf))
        blocks.append(extra)
        cand = _extract_top(jnp.concatenate(blocks, axis=0), _N_EXTRACT)
        m0 = sv1[0] + sv2[0]
        z = jnp.zeros_like(m0)
        for r in range(PEER_TOPK):
            z = z + jnp.exp(cand[r] - m0)
        c16, c17 = cand[PEER_TOPK - 1], cand[PEER_TOPK]
        tau = jnp.where(c17 == -jnp.inf, c16, 0.5 * (c16 + c17))
        thr_ref[hh] = tau - scores[0]
        a_ref[hh] = jnp.exp(scores[0] - sv1[0]) / z
        s2_ref[hh] = scores[1]
        b_ref[hh] = jnp.exp(scores[1] - sv2[0])


def _peer_prep(x2d, mla, dsa, wts, tc):
    n = x2d.shape[0]
    row = lambda i: (i, 0)
    const2 = lambda i: (0, 0)
    col3 = lambda i: (0, 0, i)
    gate_shape = jax.ShapeDtypeStruct((PEER_HEADS, PEER_NKEYS, n), F32)
    gate_spec = pl.BlockSpec((PEER_HEADS, PEER_NKEYS, tc), col3)
    mix = mla.shape[1]
    return pl.pallas_call(
        _peer_prep_kernel, grid=(n // tc,),
        in_specs=[pl.BlockSpec((tc, D_MODEL), row), pl.BlockSpec((tc, mix), row), pl.BlockSpec((tc, mix), row),
                  pl.BlockSpec(wts["w_out"].shape, const2), pl.BlockSpec((1, D_MODEL), const2),
                  pl.BlockSpec(wts["wpq_t"].shape, const2), pl.BlockSpec(wts["peer_keys"].shape, lambda i: (0, 0, 0))],
        out_specs=(pl.BlockSpec((tc, D_MODEL), row), pl.BlockSpec((D_MODEL, tc), lambda i: (0, i)),
                   gate_spec, gate_spec, gate_spec, gate_spec),
        out_shape=(jax.ShapeDtypeStruct((n, D_MODEL), F32), jax.ShapeDtypeStruct((D_MODEL, n), BF16),
                   gate_shape, gate_shape, gate_shape, gate_shape),
        compiler_params=_cparams(1), name="peer_prep",
    )(x2d, mla, dsa, wts["w_out"], wts["g_ffn"], wts["wpq_t"], wts["peer_keys"])


def _gelu(x):
    return 0.5 * x * (1.0 + lax.erf(x * np.float32(math.sqrt(0.5))))


def _peer_main_kernel(xnt_ref, thr_ref, a_ref, s2_ref, b_ref, u_ref, vt_ref, h_ref, gf_ref, y_ref,
                      act0_scr, act1_scr, w0_scr, w1_scr, acc_scr, *, ni):
    s = pl.program_id(1)
    n_blocks = 2 * (pl.num_programs(1) - 1)
    eb = ni * PEER_NKEYS

    @pl.when(s == 0)
    def _():
        for ref in (act0_scr, act1_scr, w0_scr, w1_scr, acc_scr):
            ref[...] = jnp.zeros(ref.shape, ref.dtype)

    act_scr, w_scr = (act0_scr, act1_scr), (w0_scr, w1_scr)
    tc = acc_scr.shape[1]
    mm_w = min(tc, MXU_N)
    n_mm = tc // mm_w

    def stage_a(slot, k):
        cs = slice(k * mm_w, (k + 1) * mm_w)
        act_scr[slot][:, cs] = _dot(u_ref[slot * eb:(slot + 1) * eb, :], xnt_ref[:, cs])

    def stage_c(slot, k):
        cs = slice(k * mm_w, (k + 1) * mm_w)
        acc_scr[:, cs] += _dot(vt_ref[:, slot * eb:(slot + 1) * eb], w_scr[slot][:, cs])

    def stage_b(e, slot, ii):
        i1 = jnp.clip(e, 0, n_blocks - 1) * ni + ii
        rows = slice(ii * PEER_NKEYS, (ii + 1) * PEER_NKEYS)
        thr_rows = [thr_ref[hh, pl.ds(i1, 1), :] for hh in range(PEER_HEADS)]
        a_rows = [a_ref[hh, pl.ds(i1, 1), :] for hh in range(PEER_HEADS)]
        for lt in range(tc // LANES):
            ls = slice(lt * LANES, (lt + 1) * LANES)
            gate = None
            for hh in range(PEER_HEADS):
                term = jnp.where(s2_ref[hh, :, ls] >= thr_rows[hh][:, ls], b_ref[hh, :, ls], 0.0)
                term = term * a_rows[hh][:, ls]
                gate = term if gate is None else gate + term
            w_scr[slot][rows, ls] = (gate * _gelu(act_scr[slot][rows, ls])).astype(BF16)

    def half_step(e, slot):
        mm = [functools.partial(stage_c, slot, k) for k in range(n_mm)]
        mm += [functools.partial(stage_a, slot, k) for k in range(n_mm)]
        for ii in range(ni):
            for f in mm[ii * len(mm) // ni:(ii + 1) * len(mm) // ni]:
                f()
            stage_b(e - 1, 1 - slot, ii)

    half_step(2 * s, 0)
    half_step(2 * s + 1, 1)

    @pl.when(s == pl.num_programs(1) - 1)
    def _():
        y_ref[...] = _rms(acc_scr[...].T + h_ref[...], gf_ref[...])


def _peer_main(xnt, thr, a, s2, b, h, wts, tc, ni):
    n = h.shape[0]
    eb = ni * PEER_NKEYS
    n_pairs = PEER_EXPERTS // (2 * eb)
    gate_spec = pl.BlockSpec((PEER_HEADS, PEER_NKEYS, tc), lambda i, e: (0, 0, i))
    u_map = lambda i, e: (jnp.minimum(e, n_pairs - 1), 0)
    vt_map = lambda i, e: (0, jnp.clip(e - 1, 0, n_pairs - 1))
    return pl.pallas_call(
        functools.partial(_peer_main_kernel, ni=ni), grid=(n // tc, n_pairs + 1),
        in_specs=[pl.BlockSpec((D_MODEL, tc), lambda i, e: (0, i)), gate_spec, gate_spec, gate_spec, gate_spec,
                  pl.BlockSpec((2 * eb, D_MODEL), u_map), pl.BlockSpec((D_MODEL, 2 * eb), vt_map),
                  pl.BlockSpec((tc, D_MODEL), lambda i, e: (i, 0)), pl.BlockSpec((1, D_MODEL), lambda i, e: (0, 0))],
        out_specs=pl.BlockSpec((tc, D_MODEL), lambda i, e: (i, 0)),
        out_shape=jax.ShapeDtypeStruct((n, D_MODEL), F32),
        scratch_shapes=[pltpu.VMEM((eb, tc), F32), pltpu.VMEM((eb, tc), F32), pltpu.VMEM((eb, tc), BF16),
                        pltpu.VMEM((eb, tc), BF16), pltpu.VMEM((D_MODEL, tc), F32)],
        compiler_params=_cparams(2), name="peer_main",
    )(xnt, thr, a, s2, b, wts["peer_u"], wts["peer_vt"], h, wts["g_final"])


def _peer(x2d, mla, dsa, wts):
    n = x2d.shape[0]
    tc = min(512, n)
    h, xnt, thr, a, s2, b = _peer_prep(x2d, mla, dsa, wts, tc)
    return _peer_main(xnt, thr, a, s2, b, h, wts, tc, ni=4)


def _pad_cols(w, n):
    return jnp.pad(w, ((0, 0), (0, n - w.shape[1])))


def _swap_halves(w):
    half = w.shape[-1] // 2
    return jnp.concatenate([w[..., half:], w[..., :half]], axis=-1)


def _prep_weights(g_attn, w_in, g_q, w_uq, g_kv, w_uk, w_uv, w_out, g_ffn, w_pq, peer_keys, peer_u, peer_v, g_final):
    w_cq, w_ckv, w_kr = w_in[:, 0:384], w_in[:, 384:640], w_in[:, 640:672]
    w_qb, w_kb, w_vb = w_in[:, 672:1184], w_in[:, 1184:1312], w_in[:, 1312:1440]
    w_qi, w_ki, w_wi = w_in[:, 1440:1952], w_in[:, 1952:2016], w_in[:, 2016:2024]
    qi3 = w_qi.reshape(D_MODEL, IDX_HEADS, IDX_DIM)
    qi_partner = jnp.concatenate([_swap_halves(qi3[..., :IDX_ROPE]), jnp.zeros_like(qi3[..., IDX_ROPE:])], axis=-1)
    w1 = jnp.concatenate([
        w_cq, w_ckv, w_qb, w_kb, w_vb, w_qi, qi_partner.reshape(D_MODEL, IDX_HEADS * IDX_DIM),
        _pad_cols(w_kr, LANES), _pad_cols(_swap_halves(w_kr), LANES),
        _pad_cols(w_ki, LANES), _pad_cols(_swap_halves(w_ki[:, :IDX_ROPE]), LANES),
        _pad_cols(w_wi, LANES)], axis=1).astype(BF16)
    uq3 = w_uq.reshape(Q_LORA, MLA_HEADS, MLA_NOPE + MLA_ROPE)
    rope3 = uq3[..., MLA_NOPE:]
    pad3 = lambda w: jnp.pad(w, ((0, 0), (0, 0), (0, LANES - MLA_ROPE))).reshape(Q_LORA, MLA_HEADS * LANES)
    wuq = jnp.concatenate([uq3[..., :MLA_NOPE].reshape(Q_LORA, MLA_HEADS * MLA_NOPE),
                           pad3(rope3), pad3(_swap_halves(rope3))], axis=1).astype(BF16)
    ukt = jnp.transpose(w_uk, (1, 2, 0))
    zero = jnp.zeros((MLA_NOPE, KV_LORA), F32)
    wuk = jnp.stack([jnp.concatenate([jnp.concatenate([ukt[2 * p], zero], axis=1),
                                      jnp.concatenate([zero, ukt[2 * p + 1]], axis=1)], axis=0)
                     for p in range(MLA_HEADS // 2)]).astype(BF16)
    return dict(
        g_attn=g_attn.reshape(1, -1), w1=w1, g_q=g_q.reshape(1, -1), wuq=wuq, g_kv=g_kv.reshape(1, -1), wuk=wuk,
        wuv=jnp.transpose(w_uv, (1, 0, 2)).astype(BF16), w_out=w_out.astype(BF16), g_ffn=g_ffn.reshape(1, -1),
        wpq_t=w_pq.T.astype(BF16),
        peer_keys=peer_keys.reshape(PEER_HEADS * 2, PEER_NKEYS, PEER_HALF).astype(BF16),
        peer_u=peer_u.astype(BF16), peer_vt=peer_v.T.astype(BF16), g_final=g_final.reshape(1, -1))


def _rope_tables(pos):
    half = MLA_ROPE // 2
    inv = ROPE_THETA ** (-jnp.arange(half, dtype=F32) / half)
    ang = pos.astype(F32)[:, None] * inv
    cos, sin = jnp.cos(ang), jnp.sin(ang)
    c32 = jnp.concatenate([cos, cos], axis=1)
    s32 = jnp.concatenate([-sin, sin], axis=1)
    n = pos.shape[0]
    one, zero = jnp.ones((n, 32), F32), jnp.zeros((n, 32), F32)
    ca = jnp.concatenate([c32, one, one, one], axis=1)
    sa = jnp.concatenate([s32, zero, zero, zero], axis=1)
    cb = jnp.concatenate([c32, one, c32, one], axis=1)
    sb = jnp.concatenate([s32, zero, s32, zero], axis=1)
    return ca, sa, cb, sb


def _pick_tile(n, choices):
    for c in choices:
        if n % c == 0:
            return c
    raise ValueError(f"no tile in {choices} divides {n}")


def kernel(x_prompt, x_sample, cache_ckv, cache_krope, cache_k, cache_v, cache_kidx, page_table, rel_bias, g_attn,
           w_in, g_q, w_uq, g_kv, w_uk, w_uv, w_out, g_ffn, w_pq, peer_keys, peer_u, peer_v, g_final):
    assert g_attn.shape[0] == 1, "single-layer kernel"
    b, s, d = x_prompt.shape
    bs, ts, _ = x_sample.shape
    n_pages = page_table.shape[1]
    past = n_pages * PAGE
    assert s % KV_CHUNK == 0 and ts <= 8 and (bs * ts) % LANES == 0
    wts = _prep_weights(g_attn[0], w_in[0], g_q[0], w_uq[0], g_kv[0], w_uk[0], w_uv[0], w_out[0], g_ffn[0],
                        w_pq[0], peer_keys[0], peer_u[0], peer_v[0], g_final)
    bias_p, bias_s, bias_f = _bias_tables(rel_bias)

    xp = x_prompt.reshape(b * s, d)
    tm = _pick_tile(s, (512, 256))
    (ckv_p, kr_p, kb_p, vb_p, ki_p, kcat, kb_bf, vb_bf, ki_bf, qcat, qb_hm, qi_hm, wi_p) = _inproj(
        xp, _rope_tables(jnp.arange(s)), s // tm, wts, tm)
    mla_p = _mla_prompt(qcat, kcat, wts["wuv"], b, s)
    v_t = jnp.transpose(vb_bf.reshape(b, s // KV_CHUNK, KV_CHUNK, LANES), (0, 1, 3, 2))
    dsa_p = _dsa_prompt(qi_hm, wi_p.T, ki_bf, qb_hm, kb_bf, v_t, bias_p, b, s)
    y_p = _peer(xp, mla_p, dsa_p, wts)

    ns = bs * ts
    xs = x_sample.reshape(ns, d)
    pos_s = past + jnp.tile(jnp.arange(ts), bs)
    (ckv_s, kr_s, kb_s, vb_s, ki_s, kcat_s, kb_sbf, vb_sbf, ki_sbf, qcat_s, qb_shm, qi_shm, wi_s) = _inproj(
        xs, _rope_tables(pos_s), 1, wts, _pick_tile(ns, (512, 256, 128)))
    pt_flat = page_table.reshape(-1).astype(I32)
    pp = _pick_tile(n_pages, (16, 8, 4, 2, 1))
    q5 = qcat_s.reshape(MLA_HEADS, bs, ts, KCAT).transpose(1, 0, 2, 3).reshape(bs, MLA_HEADS * ts, KCAT)
    qi_s = qi_shm.reshape(IDX_HEADS, bs, ts, IDX_DIM).transpose(1, 2, 0, 3).reshape(bs, ts * IDX_HEADS, IDX_DIM)
    wi_col = wi_s.reshape(bs, ts * IDX_HEADS, 1)
    pad_new = lambda a: jnp.pad(a.reshape(bs, ts, a.shape[-1]), ((0, 0), (0, PAGE - ts), (0, 0)))
    pad_new_t = lambda a: jnp.swapaxes(pad_new(a), 1, 2)
    olat, sc_past, sc_new = _sample1(
        pt_flat, q5[..., :KV_LORA], q5[..., KV_LORA:KV_LORA + MLA_ROPE], qi_s, wi_col,
        pad_new(kcat_s[:, :KV_LORA]), pad_new_t(kcat_s[:, KV_LORA:KV_LORA + MLA_ROPE]), pad_new_t(ki_sbf),
        cache_ckv[0], jnp.swapaxes(cache_krope[0], 1, 2), jnp.swapaxes(cache_kidx[0], 1, 2), n_pages, pp)
    olat_hm = olat.reshape(bs, MLA_HEADS, ts, KV_LORA).transpose(1, 0, 2, 3).reshape(MLA_HEADS, ns, KV_LORA)
    mla_s = _mla_out(olat_hm.astype(BF16), wts["wuv"])
    topk_s = min(IDX_TOPK_MAX, (past + ts) // 4)
    kp = -(-(past + PAGE) // KV_CHUNK) * KV_CHUNK
    sc_all = jnp.concatenate([sc_past, sc_new], axis=2).reshape(ns, past + PAGE)
    sc_t = jnp.pad(sc_all, ((0, 0), (0, kp - past - PAGE)), constant_values=-jnp.inf).T
    tau_s, cut_s = _sample_select(sc_t, topk_s)
    qb_s = qb_shm.reshape(DSA_KV_HEADS, DSA_REP, bs, ts, DSA_HEAD_DIM).transpose(2, 0, 1, 3, 4)
    qb_s = qb_s.reshape(bs, DSA_KV_HEADS, DSA_REP * ts, DSA_HEAD_DIM)
    bias_s4 = bias_s[:, :, :ts, :].reshape(2, DSA_KV_HEADS, DSA_REP * ts, PAGE)
    bias_f4 = bias_f[:, :ts, :].reshape(DSA_KV_HEADS, DSA_REP * ts, PAGE)
    n_pool = cache_k.shape[1]
    page_t = lambda c: jnp.transpose(c[0], (0, 2, 3, 1)).reshape(n_pool, LANES, PAGE)
    o_s = _sample3(pt_flat, qb_s, sc_past, sc_new, tau_s.reshape(bs, ts, 1), cut_s.reshape(bs, ts, 1),
                   pad_new_t(kb_sbf), pad_new_t(vb_sbf), bias_s4, bias_f4,
                   page_t(cache_k), page_t(cache_v), n_pages, pp)
    dsa_s = o_s.reshape(bs, DSA_KV_HEADS, DSA_REP, ts, DSA_HEAD_DIM).transpose(0, 3, 1, 2, 4)
    dsa_s = dsa_s.reshape(ns, DSA_HEADS * DSA_HEAD_DIM).astype(BF16)
    y_s = _peer(xs, mla_s, dsa_s, wts)

    kvh = (DSA_KV_HEADS, DSA_HEAD_DIM)
    return (y_p.reshape(b, s, d), y_s.reshape(bs, ts, d),
            ckv_p.reshape(1, b, s, KV_LORA), kr_p.reshape(1, b, s, MLA_ROPE),
            kb_p.reshape((1, b, s) + kvh), vb_p.reshape((1, b, s) + kvh), ki_p.reshape(1, b, s, IDX_DIM),
            ckv_s.reshape(1, bs, ts, KV_LORA), kr_s.reshape(1, bs, ts, MLA_ROPE),
            kb_s.reshape((1, bs, ts) + kvh), vb_s.reshape((1, bs, ts) + kvh), ki_s.reshape(1, bs, ts, IDX_DIM))
```

```python
import functools
import math

import jax
import jax.numpy as jnp
import numpy as np
from jax import lax
from jax.experimental import pallas as pl
from jax.experimental.pallas import tpu as pltpu

F32 = jnp.float32
BF16 = jnp.bfloat16
I32 = jnp.int32

D_MODEL = 1024
PAGE = 128
MLA_HEADS = 8
MLA_NOPE = 64
MLA_ROPE = 32
MLA_V = 64
Q_LORA = 384
KV_LORA = 256
MLA_SCALE = (MLA_NOPE + MLA_ROPE) ** -0.5
DSA_HEADS = 8
DSA_KV_HEADS = 2
DSA_REP = DSA_HEADS // DSA_KV_HEADS
DSA_HEAD_DIM = 64
DSA_SCALE = DSA_HEAD_DIM ** -0.5
IDX_HEADS = 8
IDX_DIM = 64
IDX_ROPE = 32
IDX_TOPK_MAX = 256
IDX_W_SCALE = (IDX_HEADS * IDX_DIM) ** -0.5
REL_BUCKETS = 32
REL_MAX_DIST = 128
PEER_HEADS = 8
PEER_NKEYS = 128
PEER_EXPERTS = PEER_NKEYS * PEER_NKEYS
PEER_HALF = 128
PEER_TOPK = 16
ROPE_THETA = 10000.0
NORM_EPS = 1e-6

LANES = 128
MXU_N = 256
NEG = -1e30
INT_MIN = -(2 ** 31)
KCAT = KV_LORA + LANES
KV_CHUNK = 256
MLA_CHUNK = 512
Q_TILE = 128

_C_CQ, _C_CKV, _C_QB, _C_KB, _C_VB = 0, 384, 640, 1152, 1280
_C_QI, _C_QIP, _C_KR, _C_KRP, _C_KI, _C_KIP, _C_WI, _C_END = 1408, 1920, 2432, 2560, 2688, 2816, 2944, 3072
_VMEM_LIMIT = 56 * 1024 * 1024


def _cparams(n_axes):
    return pltpu.CompilerParams(dimension_semantics=("arbitrary",) * n_axes, vmem_limit_bytes=_VMEM_LIMIT)


def _dot(a, b):
    return jnp.dot(a, b, preferred_element_type=F32)


def _dot_nt(a, b):
    return lax.dot_general(a, b, (((1,), (1,)), ((), ())), preferred_element_type=F32)


def _rms(x, g):
    return x * lax.rsqrt(jnp.mean(x * x, axis=-1, keepdims=True) + NORM_EPS) * g


def _sort_key(x):
    x = jnp.where(x == 0.0, 0.0, x)
    bits = pltpu.bitcast(x, I32)
    return bits ^ ((bits >> 31) & 0x7FFFFFFF)


def _bucket_starts():
    max_exact = REL_BUCKETS // 2
    n = np.arange(0, 2 * REL_MAX_DIST, dtype=np.int64)
    nf = np.maximum(n, 1).astype(np.float32)
    large = max_exact + (np.log(nf / np.float32(max_exact)) / np.float32(math.log(REL_MAX_DIST / max_exact))
                         * np.float32(REL_BUCKETS - max_exact)).astype(np.int32)
    large = np.minimum(large, REL_BUCKETS - 1)
    bucket = np.where(n < max_exact, n, large)
    starts = []
    for k in range(REL_BUCKETS):
        hit = np.nonzero(bucket >= k)[0]
        starts.append(int(hit[0]) if hit.size else int(n[-1]) + 1)
    return starts


_BUCKET_START = _bucket_starts()


def _bias_kernel(rb_ref, bp_ref, bs_ref, bf_ref):
    def bias_of(n, h):
        b = jnp.full(n.shape, rb_ref[REL_BUCKETS - 1, h], F32)
        for k in range(REL_BUCKETS - 2, -1, -1):
            b = jnp.where(n < _BUCKET_START[k + 1], rb_ref[k, h], b)
        return b

    s_i = lax.broadcasted_iota(I32, (KV_CHUNK, Q_TILE), 0)
    t_i = lax.broadcasted_iota(I32, (KV_CHUNK, Q_TILE), 1)
    for w in range(4):
        n = jnp.maximum(w * Q_TILE + t_i - s_i, 0)
        for h in range(DSA_HEADS):
            bp_ref[w, h] = bias_of(n, h)
    t_s = lax.broadcasted_iota(I32, (8, PAGE), 0)
    u_s = lax.broadcasted_iota(I32, (8, PAGE), 1)
    for w in range(2):
        n = jnp.maximum((1 - w) * PAGE + t_s - u_s, 0)
        for h in range(DSA_HEADS):
            bs_ref[w, h] = bias_of(n, h)
    for h in range(DSA_HEADS):
        bf_ref[h] = jnp.full((8, PAGE), rb_ref[REL_BUCKETS - 1, h], F32)


def _bias_tables(rel_bias):
    return pl.pallas_call(
        _bias_kernel,
        out_shape=(jax.ShapeDtypeStruct((4, DSA_HEADS, KV_CHUNK, Q_TILE), F32),
                   jax.ShapeDtypeStruct((2, DSA_HEADS, 8, PAGE), F32),
                   jax.ShapeDtypeStruct((DSA_HEADS, 8, PAGE), F32)),
        in_specs=[pl.BlockSpec(memory_space=pltpu.SMEM)],
        name="bias_tables",
    )(rel_bias)


def _inproj_kernel(x_ref, ga_ref, w1_ref, gq_ref, wuq_ref, gkv_ref, wuk_ref, ca_ref, sa_ref, cb_ref, sb_ref,
                   ckv_ref, krope_ref, kb_ref, vb_ref, ki_ref,
                   kcat_ref, kbbf_ref, vbbf_ref, kibf_ref, qcat_ref, qb_ref, qi_ref, wi_ref):
    xn = _rms(x_ref[...], ga_ref[...]).astype(BF16)

    def proj(lo, hi):
        return _dot(xn, w1_ref[:, lo:hi])

    ca, sa, cb, sb = ca_ref[...], sa_ref[...], cb_ref[...], sb_ref[...]

    ckv = _rms(proj(_C_CKV, _C_QB), gkv_ref[...])
    ckv_ref[...] = ckv
    kcat_ref[:, 0:KV_LORA] = ckv.astype(BF16)
    kr = proj(_C_KR, _C_KRP) * ca + proj(_C_KRP, _C_KI) * sa
    krope_ref[...] = kr[:, :MLA_ROPE]
    kcat_ref[:, KV_LORA:KCAT] = kr.astype(BF16)
    ki = proj(_C_KI, _C_KIP) * ca + proj(_C_KIP, _C_WI) * sa
    ki_ref[...] = ki[:, :IDX_DIM]
    kibf_ref[...] = ki[:, :IDX_DIM].astype(BF16)
    kb = proj(_C_KB, _C_VB)
    kb_ref[...] = kb
    kbbf_ref[...] = kb.astype(BF16)
    vb = proj(_C_VB, _C_QI)
    vb_ref[...] = vb
    vbbf_ref[...] = vb.astype(BF16)
    wi_ref[...] = proj(_C_WI, _C_END)[:, :IDX_HEADS] * IDX_W_SCALE

    qb = proj(_C_QB, _C_KB)
    for h in range(DSA_HEADS):
        qb_ref[h] = qb[:, h * DSA_HEAD_DIM:(h + 1) * DSA_HEAD_DIM].astype(BF16)
    qi = proj(_C_QI, _C_QIP)
    qip = proj(_C_QIP, _C_KR)
    for s in range(4):
        slab = qi[:, s * LANES:(s + 1) * LANES] * cb + qip[:, s * LANES:(s + 1) * LANES] * sb
        qi_ref[2 * s] = slab[:, :IDX_DIM].astype(BF16)
        qi_ref[2 * s + 1] = slab[:, IDX_DIM:].astype(BF16)

    cq = _rms(proj(_C_CQ, _C_CKV), gq_ref[...]).astype(BF16)
    n_nope = MLA_HEADS * MLA_NOPE
    n_pad = MLA_HEADS * LANES
    nope = _dot(cq, wuq_ref[:, 0:n_nope]).astype(BF16)
    for p in range(MLA_HEADS // 2):
        ql = _dot(nope[:, p * LANES:(p + 1) * LANES], wuk_ref[p])
        qcat_ref[2 * p, :, 0:KV_LORA] = ql[:, :KV_LORA].astype(BF16)
        qcat_ref[2 * p + 1, :, 0:KV_LORA] = ql[:, KV_LORA:].astype(BF16)
    for h in range(MLA_HEADS):
        lo = n_nope + h * LANES
        qr = _dot(cq, wuq_ref[:, lo:lo + LANES]) * ca + _dot(cq, wuq_ref[:, lo + n_pad:lo + n_pad + LANES]) * sa
        qcat_ref[h, :, KV_LORA:KCAT] = qr.astype(BF16)


def _inproj(x2d, tabs, tab_blocks, wts, tm):
    n = x2d.shape[0]
    const2 = lambda i: (0, 0)
    const3 = lambda i: (0, 0, 0)
    row = lambda i: (i, 0)
    tab = lambda i: (i % tab_blocks, 0)
    hm = lambda i: (0, i, 0)
    in_specs = [
        pl.BlockSpec((tm, D_MODEL), row),
        pl.BlockSpec((1, D_MODEL), const2),
        pl.BlockSpec((D_MODEL, _C_END), const2),
        pl.BlockSpec((1, Q_LORA), const2),
        pl.BlockSpec(wts["wuq"].shape, const2),
        pl.BlockSpec((1, KV_LORA), const2),
        pl.BlockSpec(wts["wuk"].shape, const3),
    ] + [pl.BlockSpec((tm, LANES), tab)] * 4
    out_shape = (
        jax.ShapeDtypeStruct((n, KV_LORA), F32), jax.ShapeDtypeStruct((n, MLA_ROPE), F32),
        jax.ShapeDtypeStruct((n, LANES), F32), jax.ShapeDtypeStruct((n, LANES), F32),
        jax.ShapeDtypeStruct((n, IDX_DIM), F32),
        jax.ShapeDtypeStruct((n, KCAT), BF16), jax.ShapeDtypeStruct((n, LANES), BF16),
        jax.ShapeDtypeStruct((n, LANES), BF16), jax.ShapeDtypeStruct((n, IDX_DIM), BF16),
        jax.ShapeDtypeStruct((MLA_HEADS, n, KCAT), BF16),
        jax.ShapeDtypeStruct((DSA_HEADS, n, DSA_HEAD_DIM), BF16),
        jax.ShapeDtypeStruct((IDX_HEADS, n, IDX_DIM), BF16),
        jax.ShapeDtypeStruct((n, IDX_HEADS), F32),
    )
    out_specs = (
        pl.BlockSpec((tm, KV_LORA), row), pl.BlockSpec((tm, MLA_ROPE), row),
        pl.BlockSpec((tm, LANES), row), pl.BlockSpec((tm, LANES), row), pl.BlockSpec((tm, IDX_DIM), row),
        pl.BlockSpec((tm, KCAT), row), pl.BlockSpec((tm, LANES), row),
        pl.BlockSpec((tm, LANES), row), pl.BlockSpec((tm, IDX_DIM), row),
        pl.BlockSpec((MLA_HEADS, tm, KCAT), hm),
        pl.BlockSpec((DSA_HEADS, tm, DSA_HEAD_DIM), hm),
        pl.BlockSpec((IDX_HEADS, tm, IDX_DIM), hm),
        pl.BlockSpec((tm, IDX_HEADS), row),
    )
    return pl.pallas_call(
        _inproj_kernel, grid=(n // tm,), in_specs=in_specs, out_specs=out_specs, out_shape=out_shape,
        compiler_params=_cparams(1), name="inproj",
    )(x2d, wts["g_attn"], wts["w1"], wts["g_q"], wts["wuq"], wts["g_kv"], wts["wuk"], *tabs)


def _rep(x, width):
    k = width // LANES
    return x if k == 1 else jnp.concatenate([x] * k, axis=1)


def _mla_prompt_kernel(q_ref, k_ref, wuv_ref, o_ref, m_scr, l_scr, acc_scr):
    j = pl.program_id(1)
    rows = MLA_HEADS * Q_TILE
    q = q_ref[...].reshape(rows, KCAT)
    m_scr[...] = jnp.full(m_scr.shape, NEG, F32)
    l_scr[...] = jnp.zeros(l_scr.shape, F32)
    acc_scr[...] = jnp.zeros(acc_scr.shape, F32)
    n_full = (j * Q_TILE) // MLA_CHUNK

    def body(c, masked):
        k = k_ref[pl.ds(pl.multiple_of(c * MLA_CHUNK, MLA_CHUNK), MLA_CHUNK), :]
        s = _dot_nt(q, k) * MLA_SCALE
        if masked:
            t_row = j * Q_TILE + lax.broadcasted_iota(I32, (rows, MLA_CHUNK), 0) % Q_TILE
            u_col = lax.broadcasted_iota(I32, (rows, MLA_CHUNK), 1)
            s = jnp.where(c * MLA_CHUNK + u_col <= t_row, s, NEG)
        m_prev = m_scr[...]
        m_new = jnp.maximum(m_prev, jnp.max(s, axis=1, keepdims=True))
        alpha = jnp.exp(m_prev - m_new)
        p = jnp.exp(s - _rep(m_new, MLA_CHUNK))
        l_scr[...] = alpha * l_scr[...] + jnp.sum(p, axis=1, keepdims=True)
        acc_scr[...] = acc_scr[...] * _rep(alpha, KV_LORA) + _dot(p.astype(BF16), k[:, :KV_LORA])
        m_scr[...] = m_new

    lax.fori_loop(0, n_full, lambda c, carry: (body(c, False), carry)[1], 0)
    body(n_full, True)
    o_lat = (acc_scr[...] / _rep(l_scr[...], KV_LORA)).astype(BF16)
    for h in range(MLA_HEADS):
        o = _dot(o_lat[h * Q_TILE:(h + 1) * Q_TILE], wuv_ref[h])
        o_ref[:, h * MLA_V:(h + 1) * MLA_V] = o.astype(BF16)


def _mla_prompt(qcat, kcat, wuv, b, s):
    nq = s // Q_TILE
    rows = MLA_HEADS * Q_TILE
    return pl.pallas_call(
        _mla_prompt_kernel, grid=(b, nq),
        in_specs=[pl.BlockSpec((MLA_HEADS, Q_TILE, KCAT), lambda bi, j: (0, bi * nq + j, 0)),
                  pl.BlockSpec((None, s, KCAT), lambda bi, j: (bi, 0, 0)),
                  pl.BlockSpec(wuv.shape, lambda bi, j: (0, 0, 0))],
        out_specs=pl.BlockSpec((Q_TILE, MLA_HEADS * MLA_V), lambda bi, j: (bi * nq + j, 0)),
        out_shape=jax.ShapeDtypeStruct((b * s, MLA_HEADS * MLA_V), BF16),
        scratch_shapes=[pltpu.VMEM((rows, LANES), F32), pltpu.VMEM((rows, LANES), F32),
                        pltpu.VMEM((rows, KV_LORA), F32)],
        compiler_params=_cparams(2), name="mla_prompt",
    )(qcat, kcat.reshape(b, s, KCAT), wuv)


def _topk_threshold(key_scr, n_chunks, topk, n_keys_pow2_bits):
    lanes = key_scr.shape[1]
    sub = KV_CHUNK // 8

    def count(pred_fn):
        def body(c, acc):
            off = pl.multiple_of(c * KV_CHUNK, KV_CHUNK)
            k = key_scr[pl.ds(off, KV_CHUNK), :]
            hit = pred_fn(k, c).astype(I32)
            return acc + jnp.sum(hit.reshape(sub, 8, lanes), axis=0)

        acc = lax.fori_loop(0, n_chunks, body, jnp.zeros((8, lanes), I32))
        return jnp.sum(acc, axis=0, keepdims=True)

    def bit_body(i, res):
        cand = res | jnp.left_shift(jnp.int32(1), 31 - i)
        cs = cand ^ INT_MIN
        cnt = count(lambda k, c: k >= cs)
        return jnp.where(cnt >= topk, cand, res)

    res = lax.fori_loop(0, 32, bit_body, jnp.zeros((1, lanes), I32))
    tau = res ^ INT_MIN
    cnt_gt = count(lambda k, c: k > tau)
    cnt_eq = count(lambda k, c: k == tau)
    need = topk - cnt_gt
    row0 = lax.broadcasted_iota(I32, (KV_CHUNK, lanes), 0)
    big = jnp.int32(2 ** 30)

    def cut_search():
        def cbody(i, cur):
            cand = cur | jnp.left_shift(jnp.int32(1), n_keys_pow2_bits - 1 - i)
            f = count(lambda k, c: jnp.where(k == tau, row0 + c * KV_CHUNK, big) < cand)
            return jnp.where(f < need, cand, cur)

        return lax.fori_loop(0, n_keys_pow2_bits, cbody, jnp.zeros((1, lanes), I32))

    cut = lax.cond(jnp.max(cnt_eq - need) > 0, cut_search, lambda: jnp.full((1, lanes), big, I32))
    return tau, cut


def _dsa_prompt_kernel(qi_ref, wt_ref, ki_ref, qb_ref, kb_ref, vt_ref, bias_ref, o_ref,
                       key_scr, mb_scr, tc_scr, m_scr, l_scr, acc_scr, *, topk, idx_bits):
    j = pl.program_id(1)
    n_chunks = (j * Q_TILE) // KV_CHUNK + 1
    t_row = j * Q_TILE + lax.broadcasted_iota(I32, (KV_CHUNK, Q_TILE), 1)
    s_loc = lax.broadcasted_iota(I32, (KV_CHUNK, Q_TILE), 0)
    qi = qi_ref[...].reshape(IDX_HEADS * Q_TILE, IDX_DIM)
    wt = wt_ref[...]

    def score_body(c, carry):
        off = pl.multiple_of(c * KV_CHUNK, KV_CHUNK)
        a = _dot_nt(ki_ref[pl.ds(off, KV_CHUNK), :], qi)
        sc = jnp.zeros((KV_CHUNK, Q_TILE), F32)
        for h in range(IDX_HEADS):
            sc = sc + wt[h:h + 1, :] * jnp.maximum(a[:, h * Q_TILE:(h + 1) * Q_TILE], 0.0)
        key = jnp.where(off + s_loc <= t_row, _sort_key(sc), INT_MIN)
        key_scr[pl.ds(off, KV_CHUNK), :] = key
        return carry

    lax.fori_loop(0, n_chunks, score_body, 0)

    @pl.when((j + 1) * Q_TILE <= topk)
    def _():
        tc_scr[0:1, :] = jnp.full((1, Q_TILE), INT_MIN, I32)
        tc_scr[1:2, :] = jnp.full((1, Q_TILE), -1, I32)

    @pl.when((j + 1) * Q_TILE > topk)
    def _():
        tau, cut = _topk_threshold(key_scr, n_chunks, topk, idx_bits)
        tc_scr[0:1, :] = tau
        tc_scr[1:2, :] = cut

    tau = tc_scr[0:1, :]
    cut = tc_scr[1:2, :]

    def mask_body(c, carry):
        off = pl.multiple_of(c * KV_CHUNK, KV_CHUNK)
        k = key_scr[pl.ds(off, KV_CHUNK), :]
        spos = off + s_loc
        v = jnp.where(k > tau, 0.0, jnp.where(k == tau, jnp.where(spos <= cut, 0.0, NEG), NEG))
        mb_scr[pl.ds(off, KV_CHUNK), :] = jnp.where(spos <= t_row, v, NEG)
        return carry

    lax.fori_loop(0, n_chunks, mask_body, 0)

    m_scr[...] = jnp.full(m_scr.shape, NEG, F32)
    l_scr[...] = jnp.zeros(l_scr.shape, F32)
    acc_scr[...] = jnp.zeros(acc_scr.shape, F32)
    qb = qb_ref[...]

    def att_body(c, carry):
        off = pl.multiple_of(c * KV_CHUNK, KV_CHUNK)
        kb = kb_ref[pl.ds(off, KV_CHUNK), :]
        mb = mb_scr[pl.ds(off, KV_CHUNK), :]
        bidx = jnp.minimum((j * Q_TILE - c * KV_CHUNK) // Q_TILE, 3)
        for g in range(DSA_KV_HEADS):
            kg = kb[:, g * DSA_HEAD_DIM:(g + 1) * DSA_HEAD_DIM]
            qg = qb[g * DSA_REP:(g + 1) * DSA_REP].reshape(DSA_REP * Q_TILE, DSA_HEAD_DIM)
            lg4 = _dot_nt(kg, qg) * DSA_SCALE
            vg = vt_ref[c, g * DSA_HEAD_DIM:(g + 1) * DSA_HEAD_DIM, :]
            for r in range(DSA_REP):
                h = g * DSA_REP + r
                lg = lg4[:, r * Q_TILE:(r + 1) * Q_TILE] + bias_ref[bidx, h] + mb
                m_prev = m_scr[h:h + 1, :]
                m_new = jnp.maximum(m_prev, jnp.max(lg, axis=0, keepdims=True))
                alpha = jnp.exp(m_prev - m_new)
                p = jnp.exp(lg - m_new)
                l_scr[h:h + 1, :] = alpha * l_scr[h:h + 1, :] + jnp.sum(p, axis=0, keepdims=True)
                rs = slice(h * DSA_HEAD_DIM, (h + 1) * DSA_HEAD_DIM)
                acc_scr[rs, :] = alpha * acc_scr[rs, :] + _dot(vg, p.astype(BF16))
                m_scr[h:h + 1, :] = m_new
        return carry

    lax.fori_loop(0, n_chunks, att_body, 0)
    inv = 1.0 / l_scr[...]
    parts = [acc_scr[h * DSA_HEAD_DIM:(h + 1) * DSA_HEAD_DIM, :] * inv[h:h + 1, :] for h in range(DSA_HEADS)]
    o_ref[...] = jnp.concatenate(parts, axis=0).T.astype(BF16)


def _dsa_prompt(qi_hm, wi_t, ki_bf, qb_hm, kb_bf, v_t, bias_p, b, s):
    nq = s // Q_TILE
    topk = min(IDX_TOPK_MAX, s // 4)
    idx_bits = max(1, int(math.ceil(math.log2(s))))
    width = DSA_HEADS * DSA_HEAD_DIM
    kern = functools.partial(_dsa_prompt_kernel, topk=topk, idx_bits=idx_bits)
    return pl.pallas_call(
        kern, grid=(b, nq),
        in_specs=[pl.BlockSpec((IDX_HEADS, Q_TILE, IDX_DIM), lambda bi, j: (0, bi * nq + j, 0)),
                  pl.BlockSpec((IDX_HEADS, Q_TILE), lambda bi, j: (0, bi * nq + j)),
                  pl.BlockSpec((None, s, IDX_DIM), lambda bi, j: (bi, 0, 0)),
                  pl.BlockSpec((DSA_HEADS, Q_TILE, DSA_HEAD_DIM), lambda bi, j: (0, bi * nq + j, 0)),
                  pl.BlockSpec((None, s, LANES), lambda bi, j: (bi, 0, 0)),
                  pl.BlockSpec((None, s // KV_CHUNK, LANES, KV_CHUNK), lambda bi, j: (bi, 0, 0, 0)),
                  pl.BlockSpec(bias_p.shape, lambda bi, j: (0, 0, 0, 0))],
        out_specs=pl.BlockSpec((Q_TILE, width), lambda bi, j: (bi * nq + j, 0)),
        out_shape=jax.ShapeDtypeStruct((b * s, width), BF16),
        scratch_shapes=[pltpu.VMEM((s, Q_TILE), I32), pltpu.VMEM((s, Q_TILE), F32), pltpu.VMEM((8, Q_TILE), I32),
                        pltpu.VMEM((DSA_HEADS, Q_TILE), F32), pltpu.VMEM((DSA_HEADS, Q_TILE), F32),
                        pltpu.VMEM((width, Q_TILE), F32)],
        compiler_params=_cparams(2), name="dsa_prompt",
    )(qi_hm, wi_t, ki_bf.reshape(b, s, IDX_DIM), qb_hm, kb_bf.reshape(b, s, LANES), v_t, bias_p)


def _sample1_kernel(pt_ref, ql_ref, qr_ref, qi_ref, wi_ref, ckvn_ref, krn_ref, kin_ref, *rest, pp):
    ckv_pages, kr_pages, ki_pages = rest[0:pp], rest[pp:2 * pp], rest[2 * pp:3 * pp]
    olat_ref, sc_ref, scn_ref = rest[3 * pp:3 * pp + 3]
    m_scr, l_scr, acc_scr = rest[3 * pp + 3:]
    j = pl.program_id(1)
    last = pl.num_programs(1) - 1
    rows = ql_ref.shape[0]
    n_tok = rows // MLA_HEADS

    @pl.when(j == 0)
    def _():
        m_scr[...] = jnp.full(m_scr.shape, NEG, F32)
        l_scr[...] = jnp.zeros(l_scr.shape, F32)
        acc_scr[...] = jnp.zeros(acc_scr.shape, F32)

    ql, qr, qi, wi = ql_ref[...], qr_ref[...], qi_ref[...], wi_ref[...]

    def attend(kcs, krs, mask):
        s = jnp.concatenate([_dot_nt(ql, kc) + _dot(qr, kr) for kc, kr in zip(kcs, krs)], axis=1) * MLA_SCALE
        if mask is not None:
            s = jnp.where(mask, s, NEG)
        m_prev = m_scr[...]
        m_new = jnp.maximum(m_prev, jnp.max(s, axis=1, keepdims=True))
        alpha = jnp.exp(m_prev - m_new)
        p = jnp.exp(s - _rep(m_new, s.shape[1])).astype(BF16)
        l_scr[...] = alpha * l_scr[...] + jnp.sum(p.astype(F32), axis=1, keepdims=True)
        pv = _dot(p[:, 0:PAGE], kcs[0])
        for k in range(1, len(kcs)):
            pv = pv + _dot(p[:, k * PAGE:(k + 1) * PAGE], kcs[k])
        acc_scr[...] = acc_scr[...] * _rep(alpha, KV_LORA) + pv
        m_scr[...] = m_new

    def index(kidx_t):
        a = jnp.maximum(_dot(qi, kidx_t), 0.0) * wi
        return jnp.sum(a.reshape(n_tok, IDX_HEADS, PAGE), axis=1)

    attend([r[...].astype(BF16) for r in ckv_pages], [r[...].astype(BF16) for r in kr_pages], None)
    for k in range(pp):
        sc_ref[:, k * PAGE:(k + 1) * PAGE] = index(ki_pages[k][...].astype(BF16))

    @pl.when(j == last)
    def _():
        t_r = lax.broadcasted_iota(I32, (rows, PAGE), 0) % n_tok
        u_c = lax.broadcasted_iota(I32, (rows, PAGE), 1)
        attend([ckvn_ref[...]], [krn_ref[...]], u_c <= t_r)
        t4 = lax.broadcasted_iota(I32, (n_tok, PAGE), 0)
        u4 = lax.broadcasted_iota(I32, (n_tok, PAGE), 1)
        scn_ref[...] = jnp.where(u4 <= t4, index(kin_ref[...]), -jnp.inf)
        olat_ref[...] = acc_scr[...] / _rep(l_scr[...], KV_LORA)


def _page_specs(shape_tail, n_pages, pp):
    nd = len(shape_tail)

    def make(k):
        return pl.BlockSpec((None,) + shape_tail,
                            lambda bi, j, pt: (pt[bi * n_pages + j * pp + k],) + (0,) * nd)

    return [make(k) for k in range(pp)]


def _sample1(pt_flat, ql, qr, qi, wi, ckvn, krn_t, kin_t, c_ckv, c_kr_t, c_ki_t, n_pages, pp):
    bs, rows = ql.shape[0], ql.shape[1]
    n_tok = rows // MLA_HEADS
    per_b = lambda tail: pl.BlockSpec((None,) + tail, lambda bi, j, pt: (bi,) + (0,) * len(tail))
    in_specs = [per_b((rows, KV_LORA)), per_b((rows, MLA_ROPE)), per_b((rows, IDX_DIM)), per_b((rows, 1)),
                per_b((PAGE, KV_LORA)), per_b((MLA_ROPE, PAGE)), per_b((IDX_DIM, PAGE))]
    in_specs += _page_specs((PAGE, KV_LORA), n_pages, pp) + _page_specs((MLA_ROPE, PAGE), n_pages, pp)
    in_specs += _page_specs((IDX_DIM, PAGE), n_pages, pp)
    out_specs = (per_b((rows, KV_LORA)),
                 pl.BlockSpec((None, n_tok, pp * PAGE), lambda bi, j, pt: (bi, 0, j)),
                 per_b((n_tok, PAGE)))
    out_shape = (jax.ShapeDtypeStruct((bs, rows, KV_LORA), F32),
                 jax.ShapeDtypeStruct((bs, n_tok, n_pages * PAGE), F32),
                 jax.ShapeDtypeStruct((bs, n_tok, PAGE), F32))
    grid_spec = pltpu.PrefetchScalarGridSpec(
        num_scalar_prefetch=1, grid=(bs, n_pages // pp), in_specs=in_specs, out_specs=out_specs,
        scratch_shapes=[pltpu.VMEM((rows, LANES), F32), pltpu.VMEM((rows, LANES), F32),
                        pltpu.VMEM((rows, KV_LORA), F32)])
    return pl.pallas_call(
        functools.partial(_sample1_kernel, pp=pp), grid_spec=grid_spec, out_shape=out_shape,
        compiler_params=_cparams(2), name="sample_mla_index",
    )(pt_flat, ql, qr, qi, wi, ckvn, krn_t, kin_t, *([c_ckv] * pp), *([c_kr_t] * pp), *([c_ki_t] * pp))


def _mla_out_kernel(o_ref, wuv_ref, out_ref):
    for h in range(MLA_HEADS):
        out_ref[:, h * MLA_V:(h + 1) * MLA_V] = _dot(o_ref[h], wuv_ref[h]).astype(BF16)


def _mla_out(olat_hm, wuv):
    n = olat_hm.shape[1]
    return pl.pallas_call(
        _mla_out_kernel, out_shape=jax.ShapeDtypeStruct((n, MLA_HEADS * MLA_V), BF16), name="sample_mla_out",
    )(olat_hm, wuv)


def _sample_select_kernel(sc_ref, tau_ref, cut_ref, key_scr, *, topk, idx_bits):
    n_chunks = sc_ref.shape[0] // KV_CHUNK

    def kbody(c, carry):
        off = pl.multiple_of(c * KV_CHUNK, KV_CHUNK)
        key_scr[pl.ds(off, KV_CHUNK), :] = _sort_key(sc_ref[pl.ds(off, KV_CHUNK), :])
        return carry

    lax.fori_loop(0, n_chunks, kbody, 0)
    tau, cut = _topk_threshold(key_scr, n_chunks, topk, idx_bits)
    tau_ref[...] = tau
    cut_ref[...] = cut


def _sample_select(sc_t, topk):
    kp, ns = sc_t.shape
    lt = min(LANES, ns)
    idx_bits = max(1, int(math.ceil(math.log2(kp))))
    kern = functools.partial(_sample_select_kernel, topk=topk, idx_bits=idx_bits)
    return pl.pallas_call(
        kern, grid=(ns // lt,),
        in_specs=[pl.BlockSpec((kp, lt), lambda i: (0, i))],
        out_specs=(pl.BlockSpec((1, lt), lambda i: (0, i)), pl.BlockSpec((1, lt), lambda i: (0, i))),
        out_shape=(jax.ShapeDtypeStruct((1, ns), I32), jax.ShapeDtypeStruct((1, ns), I32)),
        scratch_shapes=[pltpu.VMEM((kp, lt), I32)],
        compiler_params=_cparams(1), name="sample_select",
    )(sc_t)


def _sample3_kernel(pt_ref, qb_ref, sc_ref, scn_ref, tau_ref, cut_ref, kn_ref, vn_ref, bs_ref, bf_ref, *rest,
                    pp, past):
    k_pages, v_pages = rest[0:pp], rest[pp:2 * pp]
    o_ref = rest[2 * pp]
    m_scr, l_scr, acc_scr = rest[2 * pp + 1:]
    j = pl.program_id(1)
    last = pl.num_programs(1) - 1
    n_tok = sc_ref.shape[0]

    @pl.when(j == 0)
    def _():
        m_scr[...] = jnp.full(m_scr.shape, NEG, F32)
        l_scr[...] = jnp.zeros(l_scr.shape, F32)
        acc_scr[...] = jnp.zeros(acc_scr.shape, F32)

    tau, cut = tau_ref[...], cut_ref[...]

    def mask_bias(sc, base):
        k = _sort_key(sc)
        spos = base + lax.broadcasted_iota(I32, sc.shape, 1)
        return jnp.where(k > tau, 0.0, jnp.where(k == tau, jnp.where(spos <= cut, 0.0, NEG), NEG))

    def attend(kts, vts, mb4, biases):
        mb = jnp.concatenate([mb4] * DSA_REP, axis=0)
        for g in range(DSA_KV_HEADS):
            rs = slice(g * DSA_HEAD_DIM, (g + 1) * DSA_HEAD_DIM)
            lg = jnp.concatenate([_dot(qb_ref[g], kt[rs, :]) for kt in kts], axis=1) * DSA_SCALE
            lg = lg + jnp.concatenate([b[g] for b in biases], axis=1) + mb
            m_prev = m_scr[g]
            m_new = jnp.maximum(m_prev, jnp.max(lg, axis=1, keepdims=True))
            alpha = jnp.exp(m_prev - m_new)
            p = jnp.exp(lg - _rep(m_new, lg.shape[1])).astype(BF16)
            l_scr[g] = alpha * l_scr[g] + jnp.sum(p.astype(F32), axis=1, keepdims=True)
            pv = _dot_nt(p[:, 0:PAGE], vts[0][rs, :])
            for k in range(1, len(kts)):
                pv = pv + _dot_nt(p[:, k * PAGE:(k + 1) * PAGE], vts[k][rs, :])
            acc_scr[g] = acc_scr[g] * alpha[:, :DSA_HEAD_DIM] + pv
            m_scr[g] = m_new

    far = bf_ref[...]
    biases = [far] * (pp - 1) + [jnp.where(j == last, bs_ref[0], far)]
    attend([r[...].astype(BF16) for r in k_pages], [r[...].astype(BF16) for r in v_pages],
           mask_bias(sc_ref[...], j * (pp * PAGE)), biases)

    @pl.when(j == last)
    def _():
        attend([kn_ref[...]], [vn_ref[...]], mask_bias(scn_ref[...], past), [bs_ref[1]])
        for g in range(DSA_KV_HEADS):
            o_ref[g] = acc_scr[g] / l_scr[g][:, :DSA_HEAD_DIM]


def _sample3(pt_flat, qb, sc, scn, tau, cut, kn_t, vn_t, bias_s, bias_f, c_kt, c_vt, n_pages, pp):
    bs, n_tok = sc.shape[0], sc.shape[1]
    rows = DSA_REP * n_tok
    per_b = lambda tail: pl.BlockSpec((None,) + tail, lambda bi, j, pt: (bi,) + (0,) * len(tail))
    const = lambda shape: pl.BlockSpec(shape, lambda bi, j, pt: (0,) * len(shape))
    in_specs = [per_b((DSA_KV_HEADS, rows, DSA_HEAD_DIM)),
                pl.BlockSpec((None, n_tok, pp * PAGE), lambda bi, j, pt: (bi, 0, j)),
                per_b((n_tok, PAGE)), per_b((n_tok, 1)), per_b((n_tok, 1)),
                per_b((LANES, PAGE)), per_b((LANES, PAGE)),
                const(bias_s.shape), const(bias_f.shape)]
    in_specs += _page_specs((LANES, PAGE), n_pages, pp) + _page_specs((LANES, PAGE), n_pages, pp)
    grid_spec = pltpu.PrefetchScalarGridSpec(
        num_scalar_prefetch=1, grid=(bs, n_pages // pp), in_specs=in_specs,
        out_specs=per_b((DSA_KV_HEADS, rows, DSA_HEAD_DIM)),
        scratch_shapes=[pltpu.VMEM((DSA_KV_HEADS, rows, LANES), F32), pltpu.VMEM((DSA_KV_HEADS, rows, LANES), F32),
                        pltpu.VMEM((DSA_KV_HEADS, rows, DSA_HEAD_DIM), F32)])
    kern = functools.partial(_sample3_kernel, pp=pp, past=n_pages * PAGE)
    return pl.pallas_call(
        kern, grid_spec=grid_spec,
        out_shape=jax.ShapeDtypeStruct((bs, DSA_KV_HEADS, rows, DSA_HEAD_DIM), F32),
        compiler_params=_cparams(2), name="sample_dsa",
    )(pt_flat, qb, sc, scn, tau, cut, kn_t, vn_t, bias_s, bias_f, *([c_kt] * pp), *([c_vt] * pp))


_N_EXTRACT = PEER_TOPK + 1


def _extract_top(cur, n):
    vals = []
    for _ in range(n):
        m = jnp.max(cur, axis=0, keepdims=True)
        vals.append(m)
        cur = jnp.where(cur == m, -jnp.inf, cur)
    return vals


def _peer_prep_kernel(x_ref, mla_ref, dsa_ref, wo_ref, g_ref, wpq_ref, keys_ref,
                      h_ref, xnt_ref, thr_ref, a_ref, s2_ref, b_ref):
    half = wo_ref.shape[0] // 2
    h = x_ref[...] + _dot(mla_ref[...], wo_ref[0:half, :]) + _dot(dsa_ref[...], wo_ref[half:, :])
    h_ref[...] = h
    xnt = _rms(h, g_ref[...]).T.astype(BF16)
    xnt_ref[...] = xnt
    tc = xnt.shape[1]
    r8 = lax.broadcasted_iota(I32, (8, tc), 0)
    for hh in range(PEER_HEADS):
        scores, tops = [], []
        for p in range(2):
            hp = hh * 2 + p
            qt = _dot(wpq_ref[hp * PEER_HALF:(hp + 1) * PEER_HALF, :], xnt)
            s = _dot(keys_ref[hp], qt.astype(BF16))
            scores.append(s)
            tops.append(_extract_top(s, _N_EXTRACT))
        sv1, sv2 = tops
        sv2_16 = jnp.concatenate(sv2[:PEER_TOPK], axis=0)
        sv2_8 = sv2_16[:8]
        blocks = [sv1[0] + sv2_16]
        for r1 in range(1, 8):
            blocks.append(jnp.where(r8 < PEER_TOPK // (r1 + 1), sv1[r1] + sv2_8, -jnp.inf))
        blocks.append(jnp.concatenate(sv1[8:PEER_TOPK], axis=0) + sv2[0])
        extra = jnp.where(r8 == 0, sv1[0] + sv2[PEER_TOPK],
                          jnp.where(r8 == 1, sv1[PEER_TOPK] + sv2[0], -jnp.inf))
        blocks.append(extra)
        cand = _extract_top(jnp.concatenate(blocks, axis=0), _N_EXTRACT)
        m0 = sv1[0] + sv2[0]
        z = jnp.zeros_like(m0)
        for r in range(PEER_TOPK):
            z = z + jnp.exp(cand[r] - m0)
        c16, c17 = cand[PEER_TOPK - 1], cand[PEER_TOPK]
        tau = jnp.where(c17 == -jnp.inf, c16, 0.5 * (c16 + c17))
        thr_ref[hh] = tau - scores[0]
        a_ref[hh] = jnp.exp(scores[0] - sv1[0]) / z
        s2_ref[hh] = scores[1]
        b_ref[hh] = jnp.exp(scores[1] - sv2[0])


def _peer_prep(x2d, mla, dsa, wts, tc):
    n = x2d.shape[0]
    row = lambda i: (i, 0)
    const2 = lambda i: (0, 0)
    col3 = lambda i: (0, 0, i)
    gate_shape = jax.ShapeDtypeStruct((PEER_HEADS, PEER_NKEYS, n), F32)
    gate_spec = pl.BlockSpec((PEER_HEADS, PEER_NKEYS, tc), col3)
    mix = mla.shape[1]
    return pl.pallas_call(
        _peer_prep_kernel, grid=(n // tc,),
        in_specs=[pl.BlockSpec((tc, D_MODEL), row), pl.BlockSpec((tc, mix), row), pl.BlockSpec((tc, mix), row),
                  pl.BlockSpec(wts["w_out"].shape, const2), pl.BlockSpec((1, D_MODEL), const2),
                  pl.BlockSpec(wts["wpq_t"].shape, const2), pl.BlockSpec(wts["peer_keys"].shape, lambda i: (0, 0, 0))],
        out_specs=(pl.BlockSpec((tc, D_MODEL), row), pl.BlockSpec((D_MODEL, tc), lambda i: (0, i)),
                   gate_spec, gate_spec, gate_spec, gate_spec),
        out_shape=(jax.ShapeDtypeStruct((n, D_MODEL), F32), jax.ShapeDtypeStruct((D_MODEL, n), BF16),
                   gate_shape, gate_shape, gate_shape, gate_shape),
        compiler_params=_cparams(1), name="peer_prep",
    )(x2d, mla, dsa, wts["w_out"], wts["g_ffn"], wts["wpq_t"], wts["peer_keys"])


def _gelu(x):
    return 0.5 * x * (1.0 + lax.erf(x * np.float32(math.sqrt(0.5))))


def _peer_main_kernel(xnt_ref, thr_ref, a_ref, s2_ref, b_ref, u_ref, vt_ref, h_ref, gf_ref, y_ref,
                      act0_scr, act1_scr, w0_scr, w1_scr, acc_scr, *, ni):
    s = pl.program_id(1)
    n_blocks = 2 * (pl.num_programs(1) - 1)
    eb = ni * PEER_NKEYS

    @pl.when(s == 0)
    def _():
        for ref in (act0_scr, act1_scr, w0_scr, w1_scr, acc_scr):
            ref[...] = jnp.zeros(ref.shape, ref.dtype)

    act_scr, w_scr = (act0_scr, act1_scr), (w0_scr, w1_scr)
    tc = acc_scr.shape[1]
    mm_w = min(tc, MXU_N)
    n_mm = tc // mm_w

    mm_m = 256

    def stage_a(slot, k, r):
        cs = slice(k * mm_w, (k + 1) * mm_w)
        rs = slice(r * mm_m, (r + 1) * mm_m)
        act_scr[slot][rs, cs] = _dot(u_ref[slot * eb + r * mm_m:slot * eb + (r + 1) * mm_m, :], xnt_ref[:, cs])

    def stage_c(slot, k, r):
        cs = slice(k * mm_w, (k + 1) * mm_w)
        rs = slice(r * mm_m, (r + 1) * mm_m)
        acc_scr[rs, cs] += _dot(vt_ref[rs, slot * eb:(slot + 1) * eb], w_scr[slot][:, cs])

    def stage_b(e, slot, ii):
        i1 = jnp.clip(e, 0, n_blocks - 1) * ni + ii
        rows = slice(ii * PEER_NKEYS, (ii + 1) * PEER_NKEYS)
        thr_rows = [thr_ref[hh, pl.ds(i1, 1), :] for hh in range(PEER_HEADS)]
        a_rows = [a_ref[hh, pl.ds(i1, 1), :] for hh in range(PEER_HEADS)]
        for lt in range(tc // LANES):
            ls = slice(lt * LANES, (lt + 1) * LANES)
            gate = None
            for hh in range(PEER_HEADS):
                term = jnp.where(s2_ref[hh, :, ls] >= thr_rows[hh][:, ls], b_ref[hh, :, ls], 0.0)
                term = term * a_rows[hh][:, ls]
                gate = term if gate is None else gate + term
            w_scr[slot][rows, ls] = (gate * _gelu(act_scr[slot][rows, ls])).astype(BF16)

    def half_step(e, slot):
        mm = [functools.partial(stage_c, slot, k, r) for k in range(n_mm) for r in range(D_MODEL // mm_m)]
        mm += [functools.partial(stage_a, slot, k, r) for k in range(n_mm) for r in range(eb // mm_m)]
        for ii in range(ni):
            for f in mm[ii * len(mm) // ni:(ii + 1) * len(mm) // ni]:
                f()
            stage_b(e - 1, 1 - slot, ii)

    half_step(2 * s, 0)
    half_step(2 * s + 1, 1)

    @pl.when(s == pl.num_programs(1) - 1)
    def _():
        y_ref[...] = _rms(acc_scr[...].T + h_ref[...], gf_ref[...])


def _peer_main(xnt, thr, a, s2, b, h, wts, tc, ni):
    n = h.shape[0]
    eb = ni * PEER_NKEYS
    n_pairs = PEER_EXPERTS // (2 * eb)
    gate_spec = pl.BlockSpec((PEER_HEADS, PEER_NKEYS, tc), lambda i, e: (0, 0, i))
    u_map = lambda i, e: (jnp.minimum(e, n_pairs - 1), 0)
    vt_map = lambda i, e: (0, jnp.clip(e - 1, 0, n_pairs - 1))
    return pl.pallas_call(
        functools.partial(_peer_main_kernel, ni=ni), grid=(n // tc, n_pairs + 1),
        in_specs=[pl.BlockSpec((D_MODEL, tc), lambda i, e: (0, i)), gate_spec, gate_spec, gate_spec, gate_spec,
                  pl.BlockSpec((2 * eb, D_MODEL), u_map), pl.BlockSpec((D_MODEL, 2 * eb), vt_map),
                  pl.BlockSpec((tc, D_MODEL), lambda i, e: (i, 0)), pl.BlockSpec((1, D_MODEL), lambda i, e: (0, 0))],
        out_specs=pl.BlockSpec((tc, D_MODEL), lambda i, e: (i, 0)),
        out_shape=jax.ShapeDtypeStruct((n, D_MODEL), F32),
        scratch_shapes=[pltpu.VMEM((eb, tc), F32), pltpu.VMEM((eb, tc), F32), pltpu.VMEM((eb, tc), BF16),
                        pltpu.VMEM((eb, tc), BF16), pltpu.VMEM((D_MODEL, tc), F32)],
        compiler_params=_cparams(2), name="peer_main",
    )(xnt, thr, a, s2, b, wts["peer_u"], wts["peer_vt"], h, wts["g_final"])


def _peer_chain_kernel(xnt_ref, thr_ref, a_ref, s2_ref, b_ref, u_ref, vt_ref, h_ref, gf_ref, y_ref, acc_scr, *, ni):
    e = pl.program_id(1)

    @pl.when(e == 0)
    def _():
        acc_scr[...] = jnp.zeros(acc_scr.shape, F32)

    tc = acc_scr.shape[1]
    tw = min(tc, MXU_N)
    per_slice = MXU_N // PEER_NKEYS
    chains = [(ks, k) for ks in range(ni // per_slice) for k in range(tc // tw)]

    def gates(ks, k):
        out = []
        for i2 in range(per_slice):
            i1 = e * ni + ks * per_slice + i2
            thr_rows = [thr_ref[hh, pl.ds(i1, 1), :] for hh in range(PEER_HEADS)]
            a_rows = [a_ref[hh, pl.ds(i1, 1), :] for hh in range(PEER_HEADS)]
            for lt in range(tw // LANES):
                ls = slice(k * tw + lt * LANES, k * tw + (lt + 1) * LANES)
                gate = None
                for hh in range(PEER_HEADS):
                    term = jnp.where(s2_ref[hh, :, ls] >= thr_rows[hh][:, ls], b_ref[hh, :, ls], 0.0)
                    term = term * a_rows[hh][:, ls]
                    gate = term if gate is None else gate + term
                out.append(gate)
        return out

    g_next = gates(*chains[0])
    for c, (ks, k) in enumerate(chains):
        es = slice(ks * MXU_N, (ks + 1) * MXU_N)
        cs = slice(k * tw, (k + 1) * tw)
        g_cur = g_next
        act = _gelu(_dot(u_ref[es, :], xnt_ref[:, cs]))
        if c + 1 < len(chains):
            g_next = gates(*chains[c + 1])
        n_lt = tw // LANES
        parts = []
        for i2 in range(per_slice):
            tiles = [(g_cur[i2 * n_lt + lt] * act[i2 * PEER_NKEYS:(i2 + 1) * PEER_NKEYS,
                                                   lt * LANES:(lt + 1) * LANES]).astype(BF16) for lt in range(n_lt)]
            parts.append(tiles[0] if n_lt == 1 else jnp.concatenate(tiles, axis=1))
        w = jnp.concatenate(parts, axis=0)
        for r in range(D_MODEL // MXU_N):
            rs = slice(r * MXU_N, (r + 1) * MXU_N)
            acc_scr[rs, cs] += _dot(vt_ref[rs, es], w)

    @pl.when(e == pl.num_programs(1) - 1)
    def _():
        y_ref[...] = _rms(acc_scr[...].T + h_ref[...], gf_ref[...])


def _peer_chain(xnt, thr, a, s2, b, h, wts, tc, ni):
    n = h.shape[0]
    eb = ni * PEER_NKEYS
    gate_spec = pl.BlockSpec((PEER_HEADS, PEER_NKEYS, tc), lambda i, e: (0, 0, i))
    return pl.pallas_call(
        functools.partial(_peer_chain_kernel, ni=ni), grid=(n // tc, PEER_EXPERTS // eb),
        in_specs=[pl.BlockSpec((D_MODEL, tc), lambda i, e: (0, i)), gate_spec, gate_spec, gate_spec, gate_spec,
                  pl.BlockSpec((eb, D_MODEL), lambda i, e: (e, 0)), pl.BlockSpec((D_MODEL, eb), lambda i, e: (0, e)),
                  pl.BlockSpec((tc, D_MODEL), lambda i, e: (i, 0)), pl.BlockSpec((1, D_MODEL), lambda i, e: (0, 0))],
        out_specs=pl.BlockSpec((tc, D_MODEL), lambda i, e: (i, 0)),
        out_shape=jax.ShapeDtypeStruct((n, D_MODEL), F32),
        scratch_shapes=[pltpu.VMEM((D_MODEL, tc), F32)],
        compiler_params=_cparams(2), name="peer_main",
    )(xnt, thr, a, s2, b, wts["peer_u"], wts["peer_vt"], h, wts["g_final"])


def _peer(x2d, mla, dsa, wts):
    n = x2d.shape[0]
    tc = min(512, n)
    h, xnt, thr, a, s2, b = _peer_prep(x2d, mla, dsa, wts, tc)
    return _peer_chain(xnt, thr, a, s2, b, h, wts, tc, ni=8)


def _pad_cols(w, n):
    return jnp.pad(w, ((0, 0), (0, n - w.shape[1])))


def _swap_halves(w):
    half = w.shape[-1] // 2
    return jnp.concatenate([w[..., half:], w[..., :half]], axis=-1)


def _prep_weights(g_attn, w_in, g_q, w_uq, g_kv, w_uk, w_uv, w_out, g_ffn, w_pq, peer_keys, peer_u, peer_v, g_final):
    w_cq, w_ckv, w_kr = w_in[:, 0:384], w_in[:, 384:640], w_in[:, 640:672]
    w_qb, w_kb, w_vb = w_in[:, 672:1184], w_in[:, 1184:1312], w_in[:, 1312:1440]
    w_qi, w_ki, w_wi = w_in[:, 1440:1952], w_in[:, 1952:2016], w_in[:, 2016:2024]
    qi3 = w_qi.reshape(D_MODEL, IDX_HEADS, IDX_DIM)
    qi_partner = jnp.concatenate([_swap_halves(qi3[..., :IDX_ROPE]), jnp.zeros_like(qi3[..., IDX_ROPE:])], axis=-1)
    w1 = jnp.concatenate([
        w_cq, w_ckv, w_qb, w_kb, w_vb, w_qi, qi_partner.reshape(D_MODEL, IDX_HEADS * IDX_DIM),
        _pad_cols(w_kr, LANES), _pad_cols(_swap_halves(w_kr), LANES),
        _pad_cols(w_ki, LANES), _pad_cols(_swap_halves(w_ki[:, :IDX_ROPE]), LANES),
        _pad_cols(w_wi, LANES)], axis=1).astype(BF16)
    uq3 = w_uq.reshape(Q_LORA, MLA_HEADS, MLA_NOPE + MLA_ROPE)
    rope3 = uq3[..., MLA_NOPE:]
    pad3 = lambda w: jnp.pad(w, ((0, 0), (0, 0), (0, LANES - MLA_ROPE))).reshape(Q_LORA, MLA_HEADS * LANES)
    wuq = jnp.concatenate([uq3[..., :MLA_NOPE].reshape(Q_LORA, MLA_HEADS * MLA_NOPE),
                           pad3(rope3), pad3(_swap_halves(rope3))], axis=1).astype(BF16)
    ukt = jnp.transpose(w_uk, (1, 2, 0))
    zero = jnp.zeros((MLA_NOPE, KV_LORA), F32)
    wuk = jnp.stack([jnp.concatenate([jnp.concatenate([ukt[2 * p], zero], axis=1),
                                      jnp.concatenate([zero, ukt[2 * p + 1]], axis=1)], axis=0)
                     for p in range(MLA_HEADS // 2)]).astype(BF16)
    return dict(
        g_attn=g_attn.reshape(1, -1), w1=w1, g_q=g_q.reshape(1, -1), wuq=wuq, g_kv=g_kv.reshape(1, -1), wuk=wuk,
        wuv=jnp.transpose(w_uv, (1, 0, 2)).astype(BF16), w_out=w_out.astype(BF16), g_ffn=g_ffn.reshape(1, -1),
        wpq_t=w_pq.T.astype(BF16),
        peer_keys=peer_keys.reshape(PEER_HEADS * 2, PEER_NKEYS, PEER_HALF).astype(BF16),
        peer_u=peer_u.astype(BF16), peer_vt=peer_v.T.astype(BF16), g_final=g_final.reshape(1, -1))


def _rope_tables(pos):
    half = MLA_ROPE // 2
    inv = ROPE_THETA ** (-jnp.arange(half, dtype=F32) / half)
    ang = pos.astype(F32)[:, None] * inv
    cos, sin = jnp.cos(ang), jnp.sin(ang)
    c32 = jnp.concatenate([cos, cos], axis=1)
    s32 = jnp.concatenate([-sin, sin], axis=1)
    n = pos.shape[0]
    one, zero = jnp.ones((n, 32), F32), jnp.zeros((n, 32), F32)
    ca = jnp.concatenate([c32, one, one, one], axis=1)
    sa = jnp.concatenate([s32, zero, zero, zero], axis=1)
    cb = jnp.concatenate([c32, one, c32, one], axis=1)
    sb = jnp.concatenate([s32, zero, s32, zero], axis=1)
    return ca, sa, cb, sb


def _pick_tile(n, choices):
    for c in choices:
        if n % c == 0:
            return c
    raise ValueError(f"no tile in {choices} divides {n}")


def kernel(x_prompt, x_sample, cache_ckv, cache_krope, cache_k, cache_v, cache_kidx, page_table, rel_bias, g_attn,
           w_in, g_q, w_uq, g_kv, w_uk, w_uv, w_out, g_ffn, w_pq, peer_keys, peer_u, peer_v, g_final):
    assert g_attn.shape[0] == 1, "single-layer kernel"
    b, s, d = x_prompt.shape
    bs, ts, _ = x_sample.shape
    n_pages = page_table.shape[1]
    past = n_pages * PAGE
    assert s % MLA_CHUNK == 0 and s % KV_CHUNK == 0 and ts <= 8 and (bs * ts) % LANES == 0
    wts = _prep_weights(g_attn[0], w_in[0], g_q[0], w_uq[0], g_kv[0], w_uk[0], w_uv[0], w_out[0], g_ffn[0],
                        w_pq[0], peer_keys[0], peer_u[0], peer_v[0], g_final)
    bias_p, bias_s, bias_f = _bias_tables(rel_bias)

    xp = x_prompt.reshape(b * s, d)
    tm = _pick_tile(s, (512, 256))
    (ckv_p, kr_p, kb_p, vb_p, ki_p, kcat, kb_bf, vb_bf, ki_bf, qcat, qb_hm, qi_hm, wi_p) = _inproj(
        xp, _rope_tables(jnp.arange(s)), s // tm, wts, tm)
    mla_p = _mla_prompt(qcat, kcat, wts["wuv"], b, s)
    v_t = jnp.transpose(vb_bf.reshape(b, s // KV_CHUNK, KV_CHUNK, LANES), (0, 1, 3, 2))
    dsa_p = _dsa_prompt(qi_hm, wi_p.T, ki_bf, qb_hm, kb_bf, v_t, bias_p, b, s)
    y_p = _peer(xp, mla_p, dsa_p, wts)

    ns = bs * ts
    xs = x_sample.reshape(ns, d)
    pos_s = past + jnp.tile(jnp.arange(ts), bs)
    (ckv_s, kr_s, kb_s, vb_s, ki_s, kcat_s, kb_sbf, vb_sbf, ki_sbf, qcat_s, qb_shm, qi_shm, wi_s) = _inproj(
        xs, _rope_tables(pos_s), 1, wts, _pick_tile(ns, (512, 256, 128)))
    pt_flat = page_table.reshape(-1).astype(I32)
    pp = _pick_tile(n_pages, (16, 8, 4, 2, 1))
    q5 = qcat_s.reshape(MLA_HEADS, bs, ts, KCAT).transpose(1, 0, 2, 3).reshape(bs, MLA_HEADS * ts, KCAT)
    qi_s = qi_shm.reshape(IDX_HEADS, bs, ts, IDX_DIM).transpose(1, 2, 0, 3).reshape(bs, ts * IDX_HEADS, IDX_DIM)
    wi_col = wi_s.reshape(bs, ts * IDX_HEADS, 1)
    pad_new = lambda a: jnp.pad(a.reshape(bs, ts, a.shape[-1]), ((0, 0), (0, PAGE - ts), (0, 0)))
    pad_new_t = lambda a: jnp.swapaxes(pad_new(a), 1, 2)
    olat, sc_past, sc_new = _sample1(
        pt_flat, q5[..., :KV_LORA], q5[..., KV_LORA:KV_LORA + MLA_ROPE], qi_s, wi_col,
        pad_new(kcat_s[:, :KV_LORA]), pad_new_t(kcat_s[:, KV_LORA:KV_LORA + MLA_ROPE]), pad_new_t(ki_sbf),
        cache_ckv[0], jnp.swapaxes(cache_krope[0], 1, 2), jnp.swapaxes(cache_kidx[0], 1, 2), n_pages, pp)
    olat_hm = olat.reshape(bs, MLA_HEADS, ts, KV_LORA).transpose(1, 0, 2, 3).reshape(MLA_HEADS, ns, KV_LORA)
    mla_s = _mla_out(olat_hm.astype(BF16), wts["wuv"])
    topk_s = min(IDX_TOPK_MAX, (past + ts) // 4)
    kp = -(-(past + PAGE) // KV_CHUNK) * KV_CHUNK
    sc_all = jnp.concatenate([sc_past, sc_new], axis=2).reshape(ns, past + PAGE)
    sc_t = jnp.pad(sc_all, ((0, 0), (0, kp - past - PAGE)), constant_values=-jnp.inf).T
    tau_s, cut_s = _sample_select(sc_t, topk_s)
    qb_s = qb_shm.reshape(DSA_KV_HEADS, DSA_REP, bs, ts, DSA_HEAD_DIM).transpose(2, 0, 1, 3, 4)
    qb_s = qb_s.reshape(bs, DSA_KV_HEADS, DSA_REP * ts, DSA_HEAD_DIM)
    bias_s4 = bias_s[:, :, :ts, :].reshape(2, DSA_KV_HEADS, DSA_REP * ts, PAGE)
    bias_f4 = bias_f[:, :ts, :].reshape(DSA_KV_HEADS, DSA_REP * ts, PAGE)
    n_pool = cache_k.shape[1]
    page_t = lambda c: jnp.transpose(c[0], (0, 2, 3, 1)).reshape(n_pool, LANES, PAGE)
    o_s = _sample3(pt_flat, qb_s, sc_past, sc_new, tau_s.reshape(bs, ts, 1), cut_s.reshape(bs, ts, 1),
                   pad_new_t(kb_sbf), pad_new_t(vb_sbf), bias_s4, bias_f4,
                   page_t(cache_k), page_t(cache_v), n_pages, pp)
    dsa_s = o_s.reshape(bs, DSA_KV_HEADS, DSA_REP, ts, DSA_HEAD_DIM).transpose(0, 3, 1, 2, 4)
    dsa_s = dsa_s.reshape(ns, DSA_HEADS * DSA_HEAD_DIM).astype(BF16)
    y_s = _peer(xs, mla_s, dsa_s, wts)

    kvh = (DSA_KV_HEADS, DSA_HEAD_DIM)
    return (y_p.reshape(b, s, d), y_s.reshape(bs, ts, d),
            ckv_p.reshape(1, b, s, KV_LORA), kr_p.reshape(1, b, s, MLA_ROPE),
            kb_p.reshape((1, b, s) + kvh), vb_p.reshape((1, b, s) + kvh), ki_p.reshape(1, b, s, IDX_DIM),
            ckv_s.reshape(1, bs, ts, KV_LORA), kr_s.reshape(1, bs, ts, MLA_ROPE),
            kb_s.reshape((1, bs, ts) + kvh), vb_s.reshape((1, bs, ts) + kvh), ki_s.reshape(1, bs, ts, IDX_DIM))
```

```python
import functools
import math

import jax
import jax.numpy as jnp
import numpy as np
from jax import lax
from jax.experimental import pallas as pl
from jax.experimental.pallas import tpu as pltpu

F32 = jnp.float32
BF16 = jnp.bfloat16
I32 = jnp.int32

D_MODEL = 1024
PAGE = 128
MLA_HEADS = 8
MLA_NOPE = 64
MLA_ROPE = 32
MLA_V = 64
Q_LORA = 384
KV_LORA = 256
MLA_SCALE = (MLA_NOPE + MLA_ROPE) ** -0.5
DSA_HEADS = 8
DSA_KV_HEADS = 2
DSA_REP = DSA_HEADS // DSA_KV_HEADS
DSA_HEAD_DIM = 64
DSA_SCALE = DSA_HEAD_DIM ** -0.5
IDX_HEADS = 8
IDX_DIM = 64
IDX_ROPE = 32
IDX_TOPK_MAX = 256
IDX_W_SCALE = (IDX_HEADS * IDX_DIM) ** -0.5
REL_BUCKETS = 32
REL_MAX_DIST = 128
PEER_HEADS = 8
PEER_NKEYS = 128
PEER_EXPERTS = PEER_NKEYS * PEER_NKEYS
PEER_HALF = 128
PEER_TOPK = 16
ROPE_THETA = 10000.0
NORM_EPS = 1e-6

LANES = 128
MXU_N = 256
NEG = -1e30
INT_MIN = -(2 ** 31)
KCAT = KV_LORA + LANES
KV_CHUNK = 256
MLA_CHUNK = 512
Q_TILE = 128

_C_CQ, _C_CKV, _C_QB, _C_KB, _C_VB = 0, 384, 640, 1152, 1280
_C_QI, _C_QIP, _C_KR, _C_KRP, _C_KI, _C_KIP, _C_WI, _C_END = 1408, 1920, 2432, 2560, 2688, 2816, 2944, 3072
_VMEM_LIMIT = 56 * 1024 * 1024


def _cparams(n_axes):
    return pltpu.CompilerParams(dimension_semantics=("arbitrary",) * n_axes, vmem_limit_bytes=_VMEM_LIMIT)


def _dot(a, b):
    return jnp.dot(a, b, preferred_element_type=F32)


def _dot_nt(a, b):
    return lax.dot_general(a, b, (((1,), (1,)), ((), ())), preferred_element_type=F32)


def _rms(x, g):
    return x * lax.rsqrt(jnp.mean(x * x, axis=-1, keepdims=True) + NORM_EPS) * g


def _sort_key(x):
    x = jnp.where(x == 0.0, 0.0, x)
    bits = pltpu.bitcast(x, I32)
    return bits ^ ((bits >> 31) & 0x7FFFFFFF)


def _bucket_starts():
    max_exact = REL_BUCKETS // 2
    n = np.arange(0, 2 * REL_MAX_DIST, dtype=np.int64)
    nf = np.maximum(n, 1).astype(np.float32)
    large = max_exact + (np.log(nf / np.float32(max_exact)) / np.float32(math.log(REL_MAX_DIST / max_exact))
                         * np.float32(REL_BUCKETS - max_exact)).astype(np.int32)
    large = np.minimum(large, REL_BUCKETS - 1)
    bucket = np.where(n < max_exact, n, large)
    starts = []
    for k in range(REL_BUCKETS):
        hit = np.nonzero(bucket >= k)[0]
        starts.append(int(hit[0]) if hit.size else int(n[-1]) + 1)
    return starts


_BUCKET_START = _bucket_starts()


def _bias_kernel(rb_ref, bp_ref, bs_ref, bf_ref):
    def bias_of(n, h):
        b = jnp.full(n.shape, rb_ref[REL_BUCKETS - 1, h], F32)
        for k in range(REL_BUCKETS - 2, -1, -1):
            b = jnp.where(n < _BUCKET_START[k + 1], rb_ref[k, h], b)
        return b

    s_i = lax.broadcasted_iota(I32, (KV_CHUNK, Q_TILE), 0)
    t_i = lax.broadcasted_iota(I32, (KV_CHUNK, Q_TILE), 1)
    for w in range(4):
        n = jnp.maximum(w * Q_TILE + t_i - s_i, 0)
        for h in range(DSA_HEADS):
            bp_ref[w, h] = bias_of(n, h)
    t_s = lax.broadcasted_iota(I32, (8, PAGE), 0)
    u_s = lax.broadcasted_iota(I32, (8, PAGE), 1)
    for w in range(2):
        n = jnp.maximum((1 - w) * PAGE + t_s - u_s, 0)
        for h in range(DSA_HEADS):
            bs_ref[w, h] = bias_of(n, h)
    for h in range(DSA_HEADS):
        bf_ref[h] = jnp.full((8, PAGE), rb_ref[REL_BUCKETS - 1, h], F32)


def _bias_tables(rel_bias):
    return pl.pallas_call(
        _bias_kernel,
        out_shape=(jax.ShapeDtypeStruct((4, DSA_HEADS, KV_CHUNK, Q_TILE), F32),
                   jax.ShapeDtypeStruct((2, DSA_HEADS, 8, PAGE), F32),
                   jax.ShapeDtypeStruct((DSA_HEADS, 8, PAGE), F32)),
        in_specs=[pl.BlockSpec(memory_space=pltpu.SMEM)],
        name="bias_tables",
    )(rel_bias)


def _inproj_kernel(x_ref, ga_ref, w1_ref, gq_ref, wuq_ref, gkv_ref, wuk_ref, ca_ref, sa_ref, cb_ref, sb_ref,
                   ckv_ref, krope_ref, kb_ref, vb_ref, ki_ref,
                   kcat_ref, kbbf_ref, vbbf_ref, kibf_ref, qcat_ref, qb_ref, qi_ref, wi_ref):
    xn = _rms(x_ref[...], ga_ref[...]).astype(BF16)

    def proj(lo, hi):
        return _dot(xn, w1_ref[:, lo:hi])

    ca, sa, cb, sb = ca_ref[...], sa_ref[...], cb_ref[...], sb_ref[...]

    ckv = _rms(proj(_C_CKV, _C_QB), gkv_ref[...])
    ckv_ref[...] = ckv
    kcat_ref[:, 0:KV_LORA] = ckv.astype(BF16)
    kr = proj(_C_KR, _C_KRP) * ca + proj(_C_KRP, _C_KI) * sa
    krope_ref[...] = kr.T[:MLA_ROPE]
    kcat_ref[:, KV_LORA:KCAT] = kr.astype(BF16)
    ki = proj(_C_KI, _C_KIP) * ca + proj(_C_KIP, _C_WI) * sa
    ki_ref[...] = ki.T[:IDX_DIM]
    kibf_ref[...] = ki[:, :IDX_DIM].astype(BF16)
    kb = proj(_C_KB, _C_VB)
    kb_ref[...] = kb.T
    kbbf_ref[...] = kb.astype(BF16)
    vb = proj(_C_VB, _C_QI)
    vb_ref[...] = vb.T
    vbbf_ref[...] = vb.astype(BF16)
    wi_ref[...] = proj(_C_WI, _C_END)[:, :IDX_HEADS] * IDX_W_SCALE

    qb = proj(_C_QB, _C_KB)
    for h in range(DSA_HEADS):
        qb_ref[h] = qb[:, h * DSA_HEAD_DIM:(h + 1) * DSA_HEAD_DIM].astype(BF16)
    qi = proj(_C_QI, _C_QIP)
    qip = proj(_C_QIP, _C_KR)
    for s in range(4):
        slab = qi[:, s * LANES:(s + 1) * LANES] * cb + qip[:, s * LANES:(s + 1) * LANES] * sb
        qi_ref[2 * s] = slab[:, :IDX_DIM].astype(BF16)
        qi_ref[2 * s + 1] = slab[:, IDX_DIM:].astype(BF16)

    cq = _rms(proj(_C_CQ, _C_CKV), gq_ref[...]).astype(BF16)
    n_nope = MLA_HEADS * MLA_NOPE
    n_pad = MLA_HEADS * LANES
    nope = _dot(cq, wuq_ref[:, 0:n_nope]).astype(BF16)
    for p in range(MLA_HEADS // 2):
        ql = _dot(nope[:, p * LANES:(p + 1) * LANES], wuk_ref[p])
        qcat_ref[2 * p, :, 0:KV_LORA] = ql[:, :KV_LORA].astype(BF16)
        qcat_ref[2 * p + 1, :, 0:KV_LORA] = ql[:, KV_LORA:].astype(BF16)
    for h in range(MLA_HEADS):
        lo = n_nope + h * LANES
        qr = _dot(cq, wuq_ref[:, lo:lo + LANES]) * ca + _dot(cq, wuq_ref[:, lo + n_pad:lo + n_pad + LANES]) * sa
        qcat_ref[h, :, KV_LORA:KCAT] = qr.astype(BF16)


def _inproj(x2d, tabs, seq, wts, tm):
    n = x2d.shape[0]
    tab_blocks = seq // tm
    n_seq = n // seq
    const2 = lambda i: (0, 0)
    const3 = lambda i: (0, 0, 0)
    row = lambda i: (i, 0)
    tab = lambda i: (i % tab_blocks, 0)
    hm = lambda i: (0, i, 0)
    col = lambda i: (i // tab_blocks, 0, i % tab_blocks)
    t_shape = lambda width: jax.ShapeDtypeStruct((n_seq, width, seq), F32)
    t_spec = lambda width: pl.BlockSpec((None, width, tm), col)
    in_specs = [
        pl.BlockSpec((tm, D_MODEL), row),
        pl.BlockSpec((1, D_MODEL), const2),
        pl.BlockSpec((D_MODEL, _C_END), const2),
        pl.BlockSpec((1, Q_LORA), const2),
        pl.BlockSpec(wts["wuq"].shape, const2),
        pl.BlockSpec((1, KV_LORA), const2),
        pl.BlockSpec(wts["wuk"].shape, const3),
    ] + [pl.BlockSpec((tm, LANES), tab)] * 4
    out_shape = (
        jax.ShapeDtypeStruct((n, KV_LORA), F32), t_shape(MLA_ROPE), t_shape(LANES), t_shape(LANES), t_shape(IDX_DIM),
        jax.ShapeDtypeStruct((n, KCAT), BF16), jax.ShapeDtypeStruct((n, LANES), BF16),
        jax.ShapeDtypeStruct((n, LANES), BF16), jax.ShapeDtypeStruct((n, IDX_DIM), BF16),
        jax.ShapeDtypeStruct((MLA_HEADS, n, KCAT), BF16),
        jax.ShapeDtypeStruct((DSA_HEADS, n, DSA_HEAD_DIM), BF16),
        jax.ShapeDtypeStruct((IDX_HEADS, n, IDX_DIM), BF16),
        jax.ShapeDtypeStruct((n, IDX_HEADS), F32),
    )
    out_specs = (
        pl.BlockSpec((tm, KV_LORA), row), t_spec(MLA_ROPE), t_spec(LANES), t_spec(LANES), t_spec(IDX_DIM),
        pl.BlockSpec((tm, KCAT), row), pl.BlockSpec((tm, LANES), row),
        pl.BlockSpec((tm, LANES), row), pl.BlockSpec((tm, IDX_DIM), row),
        pl.BlockSpec((MLA_HEADS, tm, KCAT), hm),
        pl.BlockSpec((DSA_HEADS, tm, DSA_HEAD_DIM), hm),
        pl.BlockSpec((IDX_HEADS, tm, IDX_DIM), hm),
        pl.BlockSpec((tm, IDX_HEADS), row),
    )
    return pl.pallas_call(
        _inproj_kernel, grid=(n // tm,), in_specs=in_specs, out_specs=out_specs, out_shape=out_shape,
        compiler_params=_cparams(1), name="inproj",
    )(x2d, wts["g_attn"], wts["w1"], wts["g_q"], wts["wuq"], wts["g_kv"], wts["wuk"], *tabs)


def _rep(x, width):
    k = width // LANES
    return x if k == 1 else jnp.concatenate([x] * k, axis=1)


def _mla_prompt_kernel(q_ref, k_ref, wuv_ref, o_ref, m_scr, l_scr, acc_scr):
    j = pl.program_id(1)
    rows = MLA_HEADS * Q_TILE
    q = q_ref[...].reshape(rows, KCAT)
    m_scr[...] = jnp.full(m_scr.shape, NEG, F32)
    l_scr[...] = jnp.zeros(l_scr.shape, F32)
    acc_scr[...] = jnp.zeros(acc_scr.shape, F32)
    n_full = (j * Q_TILE) // MLA_CHUNK

    def body(c, masked):
        k = k_ref[pl.ds(pl.multiple_of(c * MLA_CHUNK, MLA_CHUNK), MLA_CHUNK), :]
        s = _dot_nt(q, k) * MLA_SCALE
        if masked:
            t_row = j * Q_TILE + lax.broadcasted_iota(I32, (rows, MLA_CHUNK), 0) % Q_TILE
            u_col = lax.broadcasted_iota(I32, (rows, MLA_CHUNK), 1)
            s = jnp.where(c * MLA_CHUNK + u_col <= t_row, s, NEG)
        m_prev = m_scr[...]
        m_new = jnp.maximum(m_prev, jnp.max(s, axis=1, keepdims=True))
        alpha = jnp.exp(m_prev - m_new)
        p = jnp.exp(s - _rep(m_new, MLA_CHUNK))
        l_scr[...] = alpha * l_scr[...] + jnp.sum(p, axis=1, keepdims=True)
        acc_scr[...] = acc_scr[...] * _rep(alpha, KV_LORA) + _dot(p.astype(BF16), k[:, :KV_LORA])
        m_scr[...] = m_new

    lax.fori_loop(0, n_full, lambda c, carry: (body(c, False), carry)[1], 0)
    body(n_full, True)
    o_lat = (acc_scr[...] / _rep(l_scr[...], KV_LORA)).astype(BF16)
    for h in range(MLA_HEADS):
        o = _dot(o_lat[h * Q_TILE:(h + 1) * Q_TILE], wuv_ref[h])
        o_ref[:, h * MLA_V:(h + 1) * MLA_V] = o.astype(BF16)


def _mla_prompt(qcat, kcat, wuv, b, s):
    nq = s // Q_TILE
    rows = MLA_HEADS * Q_TILE
    return pl.pallas_call(
        _mla_prompt_kernel, grid=(b, nq),
        in_specs=[pl.BlockSpec((MLA_HEADS, Q_TILE, KCAT), lambda bi, j: (0, bi * nq + j, 0)),
                  pl.BlockSpec((None, s, KCAT), lambda bi, j: (bi, 0, 0)),
                  pl.BlockSpec(wuv.shape, lambda bi, j: (0, 0, 0))],
        out_specs=pl.BlockSpec((Q_TILE, MLA_HEADS * MLA_V), lambda bi, j: (bi * nq + j, 0)),
        out_shape=jax.ShapeDtypeStruct((b * s, MLA_HEADS * MLA_V), BF16),
        scratch_shapes=[pltpu.VMEM((rows, LANES), F32), pltpu.VMEM((rows, LANES), F32),
                        pltpu.VMEM((rows, KV_LORA), F32)],
        compiler_params=_cparams(2), name="mla_prompt",
    )(qcat, kcat.reshape(b, s, KCAT), wuv)


def _topk_threshold(key_scr, n_chunks, topk, n_keys_pow2_bits):
    lanes = key_scr.shape[1]
    sub = KV_CHUNK // 8

    def count(pred_fn):
        def body(c, acc):
            off = pl.multiple_of(c * KV_CHUNK, KV_CHUNK)
            k = key_scr[pl.ds(off, KV_CHUNK), :]
            hit = pred_fn(k, c).astype(I32)
            return acc + jnp.sum(hit.reshape(sub, 8, lanes), axis=0)

        acc = lax.fori_loop(0, n_chunks, body, jnp.zeros((8, lanes), I32))
        return jnp.sum(acc, axis=0, keepdims=True)

    def bit_body(i, res):
        cand = res | jnp.left_shift(jnp.int32(1), 31 - i)
        cs = cand ^ INT_MIN
        cnt = count(lambda k, c: k >= cs)
        return jnp.where(cnt >= topk, cand, res)

    res = lax.fori_loop(0, 32, bit_body, jnp.zeros((1, lanes), I32))
    tau = res ^ INT_MIN
    cnt_gt = count(lambda k, c: k > tau)
    cnt_eq = count(lambda k, c: k == tau)
    need = topk - cnt_gt
    row0 = lax.broadcasted_iota(I32, (KV_CHUNK, lanes), 0)
    big = jnp.int32(2 ** 30)

    def cut_search():
        def cbody(i, cur):
            cand = cur | jnp.left_shift(jnp.int32(1), n_keys_pow2_bits - 1 - i)
            f = count(lambda k, c: jnp.where(k == tau, row0 + c * KV_CHUNK, big) < cand)
            return jnp.where(f < need, cand, cur)

        return lax.fori_loop(0, n_keys_pow2_bits, cbody, jnp.zeros((1, lanes), I32))

    cut = lax.cond(jnp.max(cnt_eq - need) > 0, cut_search, lambda: jnp.full((1, lanes), big, I32))
    return tau, cut


def _dsa_prompt_kernel(qi_ref, wt_ref, ki_ref, qb_ref, kb_ref, vt_ref, bias_ref, o_ref,
                       key_scr, mb_scr, tc_scr, m_scr, l_scr, acc_scr, *, topk, idx_bits):
    j = pl.program_id(1)
    n_chunks = (j * Q_TILE) // KV_CHUNK + 1
    t_row = j * Q_TILE + lax.broadcasted_iota(I32, (KV_CHUNK, Q_TILE), 1)
    s_loc = lax.broadcasted_iota(I32, (KV_CHUNK, Q_TILE), 0)
    qi = qi_ref[...].reshape(IDX_HEADS * Q_TILE, IDX_DIM)
    wt = wt_ref[...]

    def score_body(c, carry):
        off = pl.multiple_of(c * KV_CHUNK, KV_CHUNK)
        a = _dot_nt(ki_ref[pl.ds(off, KV_CHUNK), :], qi)
        sc = jnp.zeros((KV_CHUNK, Q_TILE), F32)
        for h in range(IDX_HEADS):
            sc = sc + wt[h:h + 1, :] * jnp.maximum(a[:, h * Q_TILE:(h + 1) * Q_TILE], 0.0)
        key = jnp.where(off + s_loc <= t_row, _sort_key(sc), INT_MIN)
        key_scr[pl.ds(off, KV_CHUNK), :] = key
        return carry

    lax.fori_loop(0, n_chunks, score_body, 0)

    @pl.when((j + 1) * Q_TILE <= topk)
    def _():
        tc_scr[0:1, :] = jnp.full((1, Q_TILE), INT_MIN, I32)
        tc_scr[1:2, :] = jnp.full((1, Q_TILE), -1, I32)

    @pl.when((j + 1) * Q_TILE > topk)
    def _():
        tau, cut = _topk_threshold(key_scr, n_chunks, topk, idx_bits)
        tc_scr[0:1, :] = tau
        tc_scr[1:2, :] = cut

    tau = tc_scr[0:1, :]
    cut = tc_scr[1:2, :]

    def mask_body(c, carry):
        off = pl.multiple_of(c * KV_CHUNK, KV_CHUNK)
        k = key_scr[pl.ds(off, KV_CHUNK), :]
        spos = off + s_loc
        v = jnp.where(k > tau, 0.0, jnp.where(k == tau, jnp.where(spos <= cut, 0.0, NEG), NEG))
        mb_scr[pl.ds(off, KV_CHUNK), :] = jnp.where(spos <= t_row, v, NEG)
        return carry

    lax.fori_loop(0, n_chunks, mask_body, 0)

    m_scr[...] = jnp.full(m_scr.shape, NEG, F32)
    l_scr[...] = jnp.zeros(l_scr.shape, F32)
    acc_scr[...] = jnp.zeros(acc_scr.shape, F32)
    qb = qb_ref[...]

    def att_body(c, carry):
        off = pl.multiple_of(c * KV_CHUNK, KV_CHUNK)
        kb = kb_ref[pl.ds(off, KV_CHUNK), :]
        mb = mb_scr[pl.ds(off, KV_CHUNK), :]
        bidx = jnp.minimum((j * Q_TILE - c * KV_CHUNK) // Q_TILE, 3)
        for g in range(DSA_KV_HEADS):
            kg = kb[:, g * DSA_HEAD_DIM:(g + 1) * DSA_HEAD_DIM]
            qg = qb[g * DSA_REP:(g + 1) * DSA_REP].reshape(DSA_REP * Q_TILE, DSA_HEAD_DIM)
            lg4 = _dot_nt(kg, qg) * DSA_SCALE
            vg = vt_ref[c, g * DSA_HEAD_DIM:(g + 1) * DSA_HEAD_DIM, :]
            for r in range(DSA_REP):
                h = g * DSA_REP + r
                lg = lg4[:, r * Q_TILE:(r + 1) * Q_TILE] + bias_ref[bidx, h] + mb
                m_prev = m_scr[h:h + 1, :]
                m_new = jnp.maximum(m_prev, jnp.max(lg, axis=0, keepdims=True))
                alpha = jnp.exp(m_prev - m_new)
                p = jnp.exp(lg - m_new)
                l_scr[h:h + 1, :] = alpha * l_scr[h:h + 1, :] + jnp.sum(p, axis=0, keepdims=True)
                rs = slice(h * DSA_HEAD_DIM, (h + 1) * DSA_HEAD_DIM)
                acc_scr[rs, :] = alpha * acc_scr[rs, :] + _dot(vg, p.astype(BF16))
                m_scr[h:h + 1, :] = m_new
        return carry

    lax.fori_loop(0, n_chunks, att_body, 0)
    inv = 1.0 / l_scr[...]
    parts = [acc_scr[h * DSA_HEAD_DIM:(h + 1) * DSA_HEAD_DIM, :] * inv[h:h + 1, :] for h in range(DSA_HEADS)]
    o_ref[...] = jnp.concatenate(parts, axis=0).T.astype(BF16)


def _dsa_prompt(qi_hm, wi_t, ki_bf, qb_hm, kb_bf, v_t, bias_p, b, s):
    nq = s // Q_TILE
    topk = min(IDX_TOPK_MAX, s // 4)
    idx_bits = max(1, int(math.ceil(math.log2(s))))
    width = DSA_HEADS * DSA_HEAD_DIM
    kern = functools.partial(_dsa_prompt_kernel, topk=topk, idx_bits=idx_bits)
    return pl.pallas_call(
        kern, grid=(b, nq),
        in_specs=[pl.BlockSpec((IDX_HEADS, Q_TILE, IDX_DIM), lambda bi, j: (0, bi * nq + j, 0)),
                  pl.BlockSpec((IDX_HEADS, Q_TILE), lambda bi, j: (0, bi * nq + j)),
                  pl.BlockSpec((None, s, IDX_DIM), lambda bi, j: (bi, 0, 0)),
                  pl.BlockSpec((DSA_HEADS, Q_TILE, DSA_HEAD_DIM), lambda bi, j: (0, bi * nq + j, 0)),
                  pl.BlockSpec((None, s, LANES), lambda bi, j: (bi, 0, 0)),
                  pl.BlockSpec((None, s // KV_CHUNK, LANES, KV_CHUNK), lambda bi, j: (bi, 0, 0, 0)),
                  pl.BlockSpec(bias_p.shape, lambda bi, j: (0, 0, 0, 0))],
        out_specs=pl.BlockSpec((Q_TILE, width), lambda bi, j: (bi * nq + j, 0)),
        out_shape=jax.ShapeDtypeStruct((b * s, width), BF16),
        scratch_shapes=[pltpu.VMEM((s, Q_TILE), I32), pltpu.VMEM((s, Q_TILE), F32), pltpu.VMEM((8, Q_TILE), I32),
                        pltpu.VMEM((DSA_HEADS, Q_TILE), F32), pltpu.VMEM((DSA_HEADS, Q_TILE), F32),
                        pltpu.VMEM((width, Q_TILE), F32)],
        compiler_params=_cparams(2), name="dsa_prompt",
    )(qi_hm, wi_t, ki_bf.reshape(b, s, IDX_DIM), qb_hm, kb_bf.reshape(b, s, LANES), v_t, bias_p)


def _sample1_kernel(pt_ref, ql_ref, qr_ref, qi_ref, wi_ref, ckvn_ref, krn_ref, kin_ref, *rest, pp):
    ckv_pages, kr_pages, ki_pages = rest[0:pp], rest[pp:2 * pp], rest[2 * pp:3 * pp]
    olat_ref, sc_ref, scn_ref = rest[3 * pp:3 * pp + 3]
    m_scr, l_scr, acc_scr = rest[3 * pp + 3:]
    j = pl.program_id(1)
    last = pl.num_programs(1) - 1
    rows = ql_ref.shape[0]
    n_tok = rows // MLA_HEADS

    @pl.when(j == 0)
    def _():
        m_scr[...] = jnp.full(m_scr.shape, NEG, F32)
        l_scr[...] = jnp.zeros(l_scr.shape, F32)
        acc_scr[...] = jnp.zeros(acc_scr.shape, F32)

    ql, qr, qi, wi = ql_ref[...], qr_ref[...], qi_ref[...], wi_ref[...]

    def attend(kcs, krs, mask):
        s = jnp.concatenate([_dot_nt(ql, kc) + _dot(qr, kr) for kc, kr in zip(kcs, krs)], axis=1) * MLA_SCALE
        if mask is not None:
            s = jnp.where(mask, s, NEG)
        m_prev = m_scr[...]
        m_new = jnp.maximum(m_prev, jnp.max(s, axis=1, keepdims=True))
        alpha = jnp.exp(m_prev - m_new)
        p = jnp.exp(s - _rep(m_new, s.shape[1])).astype(BF16)
        l_scr[...] = alpha * l_scr[...] + jnp.sum(p.astype(F32), axis=1, keepdims=True)
        pv = _dot(p[:, 0:PAGE], kcs[0])
        for k in range(1, len(kcs)):
            pv = pv + _dot(p[:, k * PAGE:(k + 1) * PAGE], kcs[k])
        acc_scr[...] = acc_scr[...] * _rep(alpha, KV_LORA) + pv
        m_scr[...] = m_new

    def index(kidx_t):
        a = jnp.maximum(_dot(qi, kidx_t), 0.0) * wi
        return jnp.sum(a.reshape(n_tok, IDX_HEADS, PAGE), axis=1)

    attend([r[...].astype(BF16) for r in ckv_pages], [r[...].astype(BF16) for r in kr_pages], None)
    for k in range(pp):
        sc_ref[:, k * PAGE:(k + 1) * PAGE] = index(ki_pages[k][...].astype(BF16))

    @pl.when(j == last)
    def _():
        t_r = lax.broadcasted_iota(I32, (rows, PAGE), 0) % n_tok
        u_c = lax.broadcasted_iota(I32, (rows, PAGE), 1)
        attend([ckvn_ref[...]], [krn_ref[...]], u_c <= t_r)
        t4 = lax.broadcasted_iota(I32, (n_tok, PAGE), 0)
        u4 = lax.broadcasted_iota(I32, (n_tok, PAGE), 1)
        scn_ref[...] = jnp.where(u4 <= t4, index(kin_ref[...]), -jnp.inf)
        olat_ref[...] = acc_scr[...] / _rep(l_scr[...], KV_LORA)


def _page_specs(shape_tail, n_pages, pp):
    nd = len(shape_tail)

    def make(k):
        return pl.BlockSpec((None,) + shape_tail,
                            lambda bi, j, pt: (pt[bi * n_pages + j * pp + k],) + (0,) * nd)

    return [make(k) for k in range(pp)]


def _sample1(pt_flat, ql, qr, qi, wi, ckvn, krn_t, kin_t, c_ckv, c_kr_t, c_ki_t, n_pages, pp):
    bs, rows = ql.shape[0], ql.shape[1]
    n_tok = rows // MLA_HEADS
    per_b = lambda tail: pl.BlockSpec((None,) + tail, lambda bi, j, pt: (bi,) + (0,) * len(tail))
    in_specs = [per_b((rows, KV_LORA)), per_b((rows, MLA_ROPE)), per_b((rows, IDX_DIM)), per_b((rows, 1)),
                per_b((PAGE, KV_LORA)), per_b((MLA_ROPE, PAGE)), per_b((IDX_DIM, PAGE))]
    in_specs += _page_specs((PAGE, KV_LORA), n_pages, pp) + _page_specs((MLA_ROPE, PAGE), n_pages, pp)
    in_specs += _page_specs((IDX_DIM, PAGE), n_pages, pp)
    out_specs = (per_b((rows, KV_LORA)),
                 pl.BlockSpec((None, n_tok, pp * PAGE), lambda bi, j, pt: (bi, 0, j)),
                 per_b((n_tok, PAGE)))
    out_shape = (jax.ShapeDtypeStruct((bs, rows, KV_LORA), F32),
                 jax.ShapeDtypeStruct((bs, n_tok, n_pages * PAGE), F32),
                 jax.ShapeDtypeStruct((bs, n_tok, PAGE), F32))
    grid_spec = pltpu.PrefetchScalarGridSpec(
        num_scalar_prefetch=1, grid=(bs, n_pages // pp), in_specs=in_specs, out_specs=out_specs,
        scratch_shapes=[pltpu.VMEM((rows, LANES), F32), pltpu.VMEM((rows, LANES), F32),
                        pltpu.VMEM((rows, KV_LORA), F32)])
    return pl.pallas_call(
        functools.partial(_sample1_kernel, pp=pp), grid_spec=grid_spec, out_shape=out_shape,
        compiler_params=_cparams(2), name="sample_mla_index",
    )(pt_flat, ql, qr, qi, wi, ckvn, krn_t, kin_t, *([c_ckv] * pp), *([c_kr_t] * pp), *([c_ki_t] * pp))


def _mla_out_kernel(o_ref, wuv_ref, out_ref):
    for h in range(MLA_HEADS):
        out_ref[:, h * MLA_V:(h + 1) * MLA_V] = _dot(o_ref[h], wuv_ref[h]).astype(BF16)


def _mla_out(olat_hm, wuv):
    n = olat_hm.shape[1]
    return pl.pallas_call(
        _mla_out_kernel, out_shape=jax.ShapeDtypeStruct((n, MLA_HEADS * MLA_V), BF16), name="sample_mla_out",
    )(olat_hm, wuv)


def _sample_select_kernel(sc_ref, tau_ref, cut_ref, key_scr, *, topk, idx_bits):
    n_chunks = sc_ref.shape[0] // KV_CHUNK

    def kbody(c, carry):
        off = pl.multiple_of(c * KV_CHUNK, KV_CHUNK)
        key_scr[pl.ds(off, KV_CHUNK), :] = _sort_key(sc_ref[pl.ds(off, KV_CHUNK), :])
        return carry

    lax.fori_loop(0, n_chunks, kbody, 0)
    tau, cut = _topk_threshold(key_scr, n_chunks, topk, idx_bits)
    tau_ref[...] = tau
    cut_ref[...] = cut


def _sample_select(sc_t, topk):
    kp, ns = sc_t.shape
    lt = min(LANES, ns)
    idx_bits = max(1, int(math.ceil(math.log2(kp))))
    kern = functools.partial(_sample_select_kernel, topk=topk, idx_bits=idx_bits)
    return pl.pallas_call(
        kern, grid=(ns // lt,),
        in_specs=[pl.BlockSpec((kp, lt), lambda i: (0, i))],
        out_specs=(pl.BlockSpec((1, lt), lambda i: (0, i)), pl.BlockSpec((1, lt), lambda i: (0, i))),
        out_shape=(jax.ShapeDtypeStruct((1, ns), I32), jax.ShapeDtypeStruct((1, ns), I32)),
        scratch_shapes=[pltpu.VMEM((kp, lt), I32)],
        compiler_params=_cparams(1), name="sample_select",
    )(sc_t)


def _sample3_kernel(pt_ref, qb_ref, sc_ref, scn_ref, tau_ref, cut_ref, kn_ref, vn_ref, bs_ref, bf_ref, *rest,
                    pp, past):
    k_pages, v_pages = rest[0:pp], rest[pp:2 * pp]
    o_ref = rest[2 * pp]
    m_scr, l_scr, acc_scr = rest[2 * pp + 1:]
    j = pl.program_id(1)
    last = pl.num_programs(1) - 1
    n_tok = sc_ref.shape[0]

    @pl.when(j == 0)
    def _():
        m_scr[...] = jnp.full(m_scr.shape, NEG, F32)
        l_scr[...] = jnp.zeros(l_scr.shape, F32)
        acc_scr[...] = jnp.zeros(acc_scr.shape, F32)

    tau, cut = tau_ref[...], cut_ref[...]

    def mask_bias(sc, base):
        k = _sort_key(sc)
        spos = base + lax.broadcasted_iota(I32, sc.shape, 1)
        return jnp.where(k > tau, 0.0, jnp.where(k == tau, jnp.where(spos <= cut, 0.0, NEG), NEG))

    def attend(kts, vts, mb4, biases):
        mb = jnp.concatenate([mb4] * DSA_REP, axis=0)
        for g in range(DSA_KV_HEADS):
            rs = slice(g * DSA_HEAD_DIM, (g + 1) * DSA_HEAD_DIM)
            lg = jnp.concatenate([_dot(qb_ref[g], kt[rs, :]) for kt in kts], axis=1) * DSA_SCALE
            lg = lg + jnp.concatenate([b[g] for b in biases], axis=1) + mb
            m_prev = m_scr[g]
            m_new = jnp.maximum(m_prev, jnp.max(lg, axis=1, keepdims=True))
            alpha = jnp.exp(m_prev - m_new)
            p = jnp.exp(lg - _rep(m_new, lg.shape[1])).astype(BF16)
            l_scr[g] = alpha * l_scr[g] + jnp.sum(p.astype(F32), axis=1, keepdims=True)
            pv = _dot_nt(p[:, 0:PAGE], vts[0][rs, :])
            for k in range(1, len(kts)):
                pv = pv + _dot_nt(p[:, k * PAGE:(k + 1) * PAGE], vts[k][rs, :])
            acc_scr[g] = acc_scr[g] * alpha[:, :DSA_HEAD_DIM] + pv
            m_scr[g] = m_new

    far = bf_ref[...]
    biases = [far] * (pp - 1) + [jnp.where(j == last, bs_ref[0], far)]
    attend([r[...].astype(BF16) for r in k_pages], [r[...].astype(BF16) for r in v_pages],
           mask_bias(sc_ref[...], j * (pp * PAGE)), biases)

    @pl.when(j == last)
    def _():
        attend([kn_ref[...]], [vn_ref[...]], mask_bias(scn_ref[...], past), [bs_ref[1]])
        for g in range(DSA_KV_HEADS):
            o_ref[g] = acc_scr[g] / l_scr[g][:, :DSA_HEAD_DIM]


def _sample3(pt_flat, qb, sc, scn, tau, cut, kn_t, vn_t, bias_s, bias_f, c_kt, c_vt, n_pages, pp):
    bs, n_tok = sc.shape[0], sc.shape[1]
    rows = DSA_REP * n_tok
    per_b = lambda tail: pl.BlockSpec((None,) + tail, lambda bi, j, pt: (bi,) + (0,) * len(tail))
    const = lambda shape: pl.BlockSpec(shape, lambda bi, j, pt: (0,) * len(shape))
    in_specs = [per_b((DSA_KV_HEADS, rows, DSA_HEAD_DIM)),
                pl.BlockSpec((None, n_tok, pp * PAGE), lambda bi, j, pt: (bi, 0, j)),
                per_b((n_tok, PAGE)), per_b((n_tok, 1)), per_b((n_tok, 1)),
                per_b((LANES, PAGE)), per_b((LANES, PAGE)),
                const(bias_s.shape), const(bias_f.shape)]
    in_specs += _page_specs((LANES, PAGE), n_pages, pp) + _page_specs((LANES, PAGE), n_pages, pp)
    grid_spec = pltpu.PrefetchScalarGridSpec(
        num_scalar_prefetch=1, grid=(bs, n_pages // pp), in_specs=in_specs,
        out_specs=per_b((DSA_KV_HEADS, rows, DSA_HEAD_DIM)),
        scratch_shapes=[pltpu.VMEM((DSA_KV_HEADS, rows, LANES), F32), pltpu.VMEM((DSA_KV_HEADS, rows, LANES), F32),
                        pltpu.VMEM((DSA_KV_HEADS, rows, DSA_HEAD_DIM), F32)])
    kern = functools.partial(_sample3_kernel, pp=pp, past=n_pages * PAGE)
    return pl.pallas_call(
        kern, grid_spec=grid_spec,
        out_shape=jax.ShapeDtypeStruct((bs, DSA_KV_HEADS, rows, DSA_HEAD_DIM), F32),
        compiler_params=_cparams(2), name="sample_dsa",
    )(pt_flat, qb, sc, scn, tau, cut, kn_t, vn_t, bias_s, bias_f, *([c_kt] * pp), *([c_vt] * pp))


_N_EXTRACT = PEER_TOPK + 1


def _extract_top(cur, n):
    vals = []
    for _ in range(n):
        m = jnp.max(cur, axis=0, keepdims=True)
        vals.append(m)
        cur = jnp.where(cur == m, -jnp.inf, cur)
    return vals


def _peer_prep_kernel(x_ref, mla_ref, dsa_ref, wo_ref, g_ref, wpq_ref, keys_ref,
                      h_ref, xnt_ref, thr_ref, a_ref, s2_ref, b_ref):
    half = wo_ref.shape[0] // 2
    h = x_ref[...] + _dot(mla_ref[...], wo_ref[0:half, :]) + _dot(dsa_ref[...], wo_ref[half:, :])
    h_ref[...] = h
    xnt = _rms(h, g_ref[...]).T.astype(BF16)
    xnt_ref[...] = xnt
    tc = xnt.shape[1]
    r8 = lax.broadcasted_iota(I32, (8, LANES), 0)
    for hh in range(PEER_HEADS):
        for p, ref in ((0, thr_ref), (1, s2_ref)):
            hp = hh * 2 + p
            qt = _dot(wpq_ref[hp * PEER_HALF:(hp + 1) * PEER_HALF, :], xnt)
            ref[hh] = _dot(keys_ref[hp], qt.astype(BF16))
        for lt in range(tc // LANES):
            ls = slice(lt * LANES, (lt + 1) * LANES)
            s1, s2 = thr_ref[hh, :, ls], s2_ref[hh, :, ls]
            sv1 = _extract_top(s1, _N_EXTRACT)
            sv2 = _extract_top(s2, _N_EXTRACT)
            sv2_16 = jnp.concatenate(sv2[:PEER_TOPK], axis=0)
            sv2_8 = sv2_16[:8]
            blocks = [sv1[0] + sv2_16]
            for r1 in range(1, 8):
                blocks.append(jnp.where(r8 < PEER_TOPK // (r1 + 1), sv1[r1] + sv2_8, -jnp.inf))
            blocks.append(jnp.concatenate(sv1[8:PEER_TOPK], axis=0) + sv2[0])
            extra = jnp.where(r8 == 0, sv1[0] + sv2[PEER_TOPK],
                              jnp.where(r8 == 1, sv1[PEER_TOPK] + sv2[0], -jnp.inf))
            blocks.append(extra)
            cand = _extract_top(jnp.concatenate(blocks, axis=0), _N_EXTRACT)
            m0 = sv1[0] + sv2[0]
            z = jnp.zeros_like(m0)
            for r in range(PEER_TOPK):
                z = z + jnp.exp(cand[r] - m0)
            c16, c17 = cand[PEER_TOPK - 1], cand[PEER_TOPK]
            tau = jnp.where(c17 == -jnp.inf, c16, 0.5 * (c16 + c17))
            thr_ref[hh, :, ls] = tau - s1
            a_ref[hh, :, ls] = jnp.exp(s1 - sv1[0]) / z
            b_ref[hh, :, ls] = jnp.exp(s2 - sv2[0])


def _peer_prep(x2d, mla, dsa, wts, tc):
    n = x2d.shape[0]
    row = lambda i: (i, 0)
    const2 = lambda i: (0, 0)
    col3 = lambda i: (0, 0, i)
    gate_shape = jax.ShapeDtypeStruct((PEER_HEADS, PEER_NKEYS, n), F32)
    gate_spec = pl.BlockSpec((PEER_HEADS, PEER_NKEYS, tc), col3)
    mix = mla.shape[1]
    return pl.pallas_call(
        _peer_prep_kernel, grid=(n // tc,),
        in_specs=[pl.BlockSpec((tc, D_MODEL), row), pl.BlockSpec((tc, mix), row), pl.BlockSpec((tc, mix), row),
                  pl.BlockSpec(wts["w_out"].shape, const2), pl.BlockSpec((1, D_MODEL), const2),
                  pl.BlockSpec(wts["wpq_t"].shape, const2), pl.BlockSpec(wts["peer_keys"].shape, lambda i: (0, 0, 0))],
        out_specs=(pl.BlockSpec((tc, D_MODEL), row), pl.BlockSpec((D_MODEL, tc), lambda i: (0, i)),
                   gate_spec, gate_spec, gate_spec, gate_spec),
        out_shape=(jax.ShapeDtypeStruct((n, D_MODEL), F32), jax.ShapeDtypeStruct((D_MODEL, n), BF16),
                   gate_shape, gate_shape, gate_shape, gate_shape),
        compiler_params=_cparams(1), name="peer_prep",
    )(x2d, mla, dsa, wts["w_out"], wts["g_ffn"], wts["wpq_t"], wts["peer_keys"])


def _gelu(x):
    return 0.5 * x * (1.0 + lax.erf(x * np.float32(math.sqrt(0.5))))


def _peer_main_kernel(xnt_ref, thr_ref, a_ref, s2_ref, b_ref, u_ref, vt_ref, h_ref, gf_ref, y_ref,
                      act0_scr, act1_scr, w0_scr, w1_scr, acc_scr, *, ni):
    s = pl.program_id(1)
    n_blocks = 2 * (pl.num_programs(1) - 1)
    eb = ni * PEER_NKEYS

    @pl.when(s == 0)
    def _():
        for ref in (act0_scr, act1_scr, w0_scr, w1_scr, acc_scr):
            ref[...] = jnp.zeros(ref.shape, ref.dtype)

    act_scr, w_scr = (act0_scr, act1_scr), (w0_scr, w1_scr)
    tc = acc_scr.shape[1]
    mm_w = min(tc, MXU_N)
    n_mm = tc // mm_w

    mm_m = 256

    def stage_a(slot, k, r):
        cs = slice(k * mm_w, (k + 1) * mm_w)
        rs = slice(r * mm_m, (r + 1) * mm_m)
        act_scr[slot][rs, cs] = _dot(u_ref[slot * eb + r * mm_m:slot * eb + (r + 1) * mm_m, :], xnt_ref[:, cs])

    def stage_c(slot, k, r):
        cs = slice(k * mm_w, (k + 1) * mm_w)
        rs = slice(r * mm_m, (r + 1) * mm_m)
        acc_scr[rs, cs] += _dot(vt_ref[rs, slot * eb:(slot + 1) * eb], w_scr[slot][:, cs])

    def stage_b(e, slot, ii):
        i1 = jnp.clip(e, 0, n_blocks - 1) * ni + ii
        rows = slice(ii * PEER_NKEYS, (ii + 1) * PEER_NKEYS)
        thr_rows = [thr_ref[hh, pl.ds(i1, 1), :] for hh in range(PEER_HEADS)]
        a_rows = [a_ref[hh, pl.ds(i1, 1), :] for hh in range(PEER_HEADS)]
        for lt in range(tc // LANES):
            ls = slice(lt * LANES, (lt + 1) * LANES)
            gate = None
            for hh in range(PEER_HEADS):
                term = jnp.where(s2_ref[hh, :, ls] >= thr_rows[hh][:, ls], b_ref[hh, :, ls], 0.0)
                term = term * a_rows[hh][:, ls]
                gate = term if gate is None else gate + term
            w_scr[slot][rows, ls] = (gate * _gelu(act_scr[slot][rows, ls])).astype(BF16)

    def half_step(e, slot):
        mm = [functools.partial(stage_c, slot, k, r) for k in range(n_mm) for r in range(D_MODEL // mm_m)]
        mm += [functools.partial(stage_a, slot, k, r) for k in range(n_mm) for r in range(eb // mm_m)]
        for ii in range(ni):
            for f in mm[ii * len(mm) // ni:(ii + 1) * len(mm) // ni]:
                f()
            stage_b(e - 1, 1 - slot, ii)

    half_step(2 * s, 0)
    half_step(2 * s + 1, 1)

    @pl.when(s == pl.num_programs(1) - 1)
    def _():
        y_ref[...] = _rms(acc_scr[...].T + h_ref[...], gf_ref[...])


def _peer_main(xnt, thr, a, s2, b, h, wts, tc, ni):
    n = h.shape[0]
    eb = ni * PEER_NKEYS
    n_pairs = PEER_EXPERTS // (2 * eb)
    gate_spec = pl.BlockSpec((PEER_HEADS, PEER_NKEYS, tc), lambda i, e: (0, 0, i))
    u_map = lambda i, e: (jnp.minimum(e, n_pairs - 1), 0)
    vt_map = lambda i, e: (0, jnp.clip(e - 1, 0, n_pairs - 1))
    return pl.pallas_call(
        functools.partial(_peer_main_kernel, ni=ni), grid=(n // tc, n_pairs + 1),
        in_specs=[pl.BlockSpec((D_MODEL, tc), lambda i, e: (0, i)), gate_spec, gate_spec, gate_spec, gate_spec,
                  pl.BlockSpec((2 * eb, D_MODEL), u_map), pl.BlockSpec((D_MODEL, 2 * eb), vt_map),
                  pl.BlockSpec((tc, D_MODEL), lambda i, e: (i, 0)), pl.BlockSpec((1, D_MODEL), lambda i, e: (0, 0))],
        out_specs=pl.BlockSpec((tc, D_MODEL), lambda i, e: (i, 0)),
        out_shape=jax.ShapeDtypeStruct((n, D_MODEL), F32),
        scratch_shapes=[pltpu.VMEM((eb, tc), F32), pltpu.VMEM((eb, tc), F32), pltpu.VMEM((eb, tc), BF16),
                        pltpu.VMEM((eb, tc), BF16), pltpu.VMEM((D_MODEL, tc), F32)],
        compiler_params=_cparams(2), name="peer_main",
    )(xnt, thr, a, s2, b, wts["peer_u"], wts["peer_vt"], h, wts["g_final"])


def _peer_chain_kernel(xnt_ref, thr_ref, a_ref, s2_ref, b_ref, u_ref, vt_ref, h_ref, gf_ref, y_ref, acc_scr, *, ni):
    e = pl.program_id(1)

    @pl.when(e == 0)
    def _():
        acc_scr[...] = jnp.zeros(acc_scr.shape, F32)

    tc = acc_scr.shape[1]
    tw = min(tc, MXU_N)
    per_slice = MXU_N // PEER_NKEYS
    chains = [(ks, k) for ks in range(ni // per_slice) for k in range(tc // tw)]

    def gates(ks, k):
        out = []
        for i2 in range(per_slice):
            i1 = e * ni + ks * per_slice + i2
            thr_rows = [thr_ref[hh, pl.ds(i1, 1), :] for hh in range(PEER_HEADS)]
            a_rows = [a_ref[hh, pl.ds(i1, 1), :] for hh in range(PEER_HEADS)]
            for lt in range(tw // LANES):
                ls = slice(k * tw + lt * LANES, k * tw + (lt + 1) * LANES)
                gate = None
                for hh in range(PEER_HEADS):
                    term = jnp.where(s2_ref[hh, :, ls] >= thr_rows[hh][:, ls], b_ref[hh, :, ls], 0.0)
                    term = term * a_rows[hh][:, ls]
                    gate = term if gate is None else gate + term
                out.append(gate)
        return out

    g_next = gates(*chains[0])
    for c, (ks, k) in enumerate(chains):
        es = slice(ks * MXU_N, (ks + 1) * MXU_N)
        cs = slice(k * tw, (k + 1) * tw)
        g_cur = g_next
        act = _gelu(_dot(u_ref[es, :], xnt_ref[:, cs]))
        if c + 1 < len(chains):
            g_next = gates(*chains[c + 1])
        n_lt = tw // LANES
        parts = []
        for i2 in range(per_slice):
            tiles = [(g_cur[i2 * n_lt + lt] * act[i2 * PEER_NKEYS:(i2 + 1) * PEER_NKEYS,
                                                   lt * LANES:(lt + 1) * LANES]).astype(BF16) for lt in range(n_lt)]
            parts.append(tiles[0] if n_lt == 1 else jnp.concatenate(tiles, axis=1))
        w = jnp.concatenate(parts, axis=0)
        for r in range(D_MODEL // MXU_N):
            rs = slice(r * MXU_N, (r + 1) * MXU_N)
            acc_scr[rs, cs] += _dot(vt_ref[rs, es], w)

    @pl.when(e == pl.num_programs(1) - 1)
    def _():
        y_ref[...] = _rms(acc_scr[...].T + h_ref[...], gf_ref[...])


def _peer_chain(xnt, thr, a, s2, b, h, wts, tc, ni):
    n = h.shape[0]
    eb = ni * PEER_NKEYS
    gate_spec = pl.BlockSpec((PEER_HEADS, PEER_NKEYS, tc), lambda i, e: (0, 0, i))
    return pl.pallas_call(
        functools.partial(_peer_chain_kernel, ni=ni), grid=(n // tc, PEER_EXPERTS // eb),
        in_specs=[pl.BlockSpec((D_MODEL, tc), lambda i, e: (0, i)), gate_spec, gate_spec, gate_spec, gate_spec,
                  pl.BlockSpec((eb, D_MODEL), lambda i, e: (e, 0)), pl.BlockSpec((D_MODEL, eb), lambda i, e: (0, e)),
                  pl.BlockSpec((tc, D_MODEL), lambda i, e: (i, 0)), pl.BlockSpec((1, D_MODEL), lambda i, e: (0, 0))],
        out_specs=pl.BlockSpec((tc, D_MODEL), lambda i, e: (i, 0)),
        out_shape=jax.ShapeDtypeStruct((n, D_MODEL), F32),
        scratch_shapes=[pltpu.VMEM((D_MODEL, tc), F32)],
        compiler_params=_cparams(2), name="peer_main",
    )(xnt, thr, a, s2, b, wts["peer_u"], wts["peer_vt"], h, wts["g_final"])


def _peer(x2d, mla, dsa, wts):
    n = x2d.shape[0]
    tc = min(512, n)
    h, xnt, thr, a, s2, b = _peer_prep(x2d, mla, dsa, wts, tc)
    return _peer_chain(xnt, thr, a, s2, b, h, wts, tc, ni=8)


def _pad_cols(w, n):
    return jnp.pad(w, ((0, 0), (0, n - w.shape[1])))


def _swap_halves(w):
    half = w.shape[-1] // 2
    return jnp.concatenate([w[..., half:], w[..., :half]], axis=-1)


def _prep_weights(g_attn, w_in, g_q, w_uq, g_kv, w_uk, w_uv, w_out, g_ffn, w_pq, peer_keys, peer_u, peer_v, g_final):
    w_cq, w_ckv, w_kr = w_in[:, 0:384], w_in[:, 384:640], w_in[:, 640:672]
    w_qb, w_kb, w_vb = w_in[:, 672:1184], w_in[:, 1184:1312], w_in[:, 1312:1440]
    w_qi, w_ki, w_wi = w_in[:, 1440:1952], w_in[:, 1952:2016], w_in[:, 2016:2024]
    qi3 = w_qi.reshape(D_MODEL, IDX_HEADS, IDX_DIM)
    qi_partner = jnp.concatenate([_swap_halves(qi3[..., :IDX_ROPE]), jnp.zeros_like(qi3[..., IDX_ROPE:])], axis=-1)
    w1 = jnp.concatenate([
        w_cq, w_ckv, w_qb, w_kb, w_vb, w_qi, qi_partner.reshape(D_MODEL, IDX_HEADS * IDX_DIM),
        _pad_cols(w_kr, LANES), _pad_cols(_swap_halves(w_kr), LANES),
        _pad_cols(w_ki, LANES), _pad_cols(_swap_halves(w_ki[:, :IDX_ROPE]), LANES),
        _pad_cols(w_wi, LANES)], axis=1).astype(BF16)
    uq3 = w_uq.reshape(Q_LORA, MLA_HEADS, MLA_NOPE + MLA_ROPE)
    rope3 = uq3[..., MLA_NOPE:]
    pad3 = lambda w: jnp.pad(w, ((0, 0), (0, 0), (0, LANES - MLA_ROPE))).reshape(Q_LORA, MLA_HEADS * LANES)
    wuq = jnp.concatenate([uq3[..., :MLA_NOPE].reshape(Q_LORA, MLA_HEADS * MLA_NOPE),
                           pad3(rope3), pad3(_swap_halves(rope3))], axis=1).astype(BF16)
    ukt = jnp.transpose(w_uk, (1, 2, 0))
    zero = jnp.zeros((MLA_NOPE, KV_LORA), F32)
    wuk = jnp.stack([jnp.concatenate([jnp.concatenate([ukt[2 * p], zero], axis=1),
                                      jnp.concatenate([zero, ukt[2 * p + 1]], axis=1)], axis=0)
                     for p in range(MLA_HEADS // 2)]).astype(BF16)
    return dict(
        g_attn=g_attn.reshape(1, -1), w1=w1, g_q=g_q.reshape(1, -1), wuq=wuq, g_kv=g_kv.reshape(1, -1), wuk=wuk,
        wuv=jnp.transpose(w_uv, (1, 0, 2)).astype(BF16), w_out=w_out.astype(BF16), g_ffn=g_ffn.reshape(1, -1),
        wpq_t=w_pq.T.astype(BF16),
        peer_keys=peer_keys.reshape(PEER_HEADS * 2, PEER_NKEYS, PEER_HALF).astype(BF16),
        peer_u=peer_u.astype(BF16), peer_vt=peer_v.T.astype(BF16), g_final=g_final.reshape(1, -1))


def _rope_tables(pos):
    half = MLA_ROPE // 2
    inv = ROPE_THETA ** (-jnp.arange(half, dtype=F32) / half)
    ang = pos.astype(F32)[:, None] * inv
    cos, sin = jnp.cos(ang), jnp.sin(ang)
    c32 = jnp.concatenate([cos, cos], axis=1)
    s32 = jnp.concatenate([-sin, sin], axis=1)
    n = pos.shape[0]
    one, zero = jnp.ones((n, 32), F32), jnp.zeros((n, 32), F32)
    ca = jnp.concatenate([c32, one, one, one], axis=1)
    sa = jnp.concatenate([s32, zero, zero, zero], axis=1)
    cb = jnp.concatenate([c32, one, c32, one], axis=1)
    sb = jnp.concatenate([s32, zero, s32, zero], axis=1)
    return ca, sa, cb, sb


def _pick_tile(n, choices):
    for c in choices:
        if n % c == 0:
            return c
    raise ValueError(f"no tile in {choices} divides {n}")


def kernel(x_prompt, x_sample, cache_ckv, cache_krope, cache_k, cache_v, cache_kidx, page_table, rel_bias, g_attn,
           w_in, g_q, w_uq, g_kv, w_uk, w_uv, w_out, g_ffn, w_pq, peer_keys, peer_u, peer_v, g_final):
    assert g_attn.shape[0] == 1, "single-layer kernel"
    b, s, d = x_prompt.shape
    bs, ts, _ = x_sample.shape
    n_pages = page_table.shape[1]
    past = n_pages * PAGE
    assert s % MLA_CHUNK == 0 and s % KV_CHUNK == 0 and ts <= 8 and (bs * ts) % LANES == 0
    wts = _prep_weights(g_attn[0], w_in[0], g_q[0], w_uq[0], g_kv[0], w_uk[0], w_uv[0], w_out[0], g_ffn[0],
                        w_pq[0], peer_keys[0], peer_u[0], peer_v[0], g_final)
    bias_p, bias_s, bias_f = _bias_tables(rel_bias)

    xp = x_prompt.reshape(b * s, d)
    tm = _pick_tile(s, (512, 256))
    (ckv_p, kr_p, kb_p, vb_p, ki_p, kcat, kb_bf, vb_bf, ki_bf, qcat, qb_hm, qi_hm, wi_p) = _inproj(
        xp, _rope_tables(jnp.arange(s)), s, wts, tm)
    mla_p = _mla_prompt(qcat, kcat, wts["wuv"], b, s)
    v_t = jnp.transpose(vb_bf.reshape(b, s // KV_CHUNK, KV_CHUNK, LANES), (0, 1, 3, 2))
    dsa_p = _dsa_prompt(qi_hm, wi_p.T, ki_bf, qb_hm, kb_bf, v_t, bias_p, b, s)
    y_p = _peer(xp, mla_p, dsa_p, wts)

    ns = bs * ts
    xs = x_sample.reshape(ns, d)
    pos_s = past + jnp.tile(jnp.arange(ts), bs)
    (ckv_s, kr_s, kb_s, vb_s, ki_s, kcat_s, kb_sbf, vb_sbf, ki_sbf, qcat_s, qb_shm, qi_shm, wi_s) = _inproj(
        xs, _rope_tables(pos_s), ns, wts, ns)
    pt_flat = page_table.reshape(-1).astype(I32)
    pp = _pick_tile(n_pages, (16, 8, 4, 2, 1))
    q5 = qcat_s.reshape(MLA_HEADS, bs, ts, KCAT).transpose(1, 0, 2, 3).reshape(bs, MLA_HEADS * ts, KCAT)
    qi_s = qi_shm.reshape(IDX_HEADS, bs, ts, IDX_DIM).transpose(1, 2, 0, 3).reshape(bs, ts * IDX_HEADS, IDX_DIM)
    wi_col = wi_s.reshape(bs, ts * IDX_HEADS, 1)
    pad_new = lambda a: jnp.pad(a.reshape(bs, ts, a.shape[-1]), ((0, 0), (0, PAGE - ts), (0, 0)))
    pad_new_t = lambda a: jnp.swapaxes(pad_new(a), 1, 2)
    olat, sc_past, sc_new = _sample1(
        pt_flat, q5[..., :KV_LORA], q5[..., KV_LORA:KV_LORA + MLA_ROPE], qi_s, wi_col,
        pad_new(kcat_s[:, :KV_LORA]), pad_new_t(kcat_s[:, KV_LORA:KV_LORA + MLA_ROPE]), pad_new_t(ki_sbf),
        cache_ckv[0], jnp.swapaxes(cache_krope[0], 1, 2), jnp.swapaxes(cache_kidx[0], 1, 2), n_pages, pp)
    olat_hm = olat.reshape(bs, MLA_HEADS, ts, KV_LORA).transpose(1, 0, 2, 3).reshape(MLA_HEADS, ns, KV_LORA)
    mla_s = _mla_out(olat_hm.astype(BF16), wts["wuv"])
    topk_s = min(IDX_TOPK_MAX, (past + ts) // 4)
    kp = -(-(past + PAGE) // KV_CHUNK) * KV_CHUNK
    sc_all = jnp.concatenate([sc_past, sc_new], axis=2).reshape(ns, past + PAGE)
    sc_t = jnp.pad(sc_all, ((0, 0), (0, kp - past - PAGE)), constant_values=-jnp.inf).T
    tau_s, cut_s = _sample_select(sc_t, topk_s)
    qb_s = qb_shm.reshape(DSA_KV_HEADS, DSA_REP, bs, ts, DSA_HEAD_DIM).transpose(2, 0, 1, 3, 4)
    qb_s = qb_s.reshape(bs, DSA_KV_HEADS, DSA_REP * ts, DSA_HEAD_DIM)
    bias_s4 = bias_s[:, :, :ts, :].reshape(2, DSA_KV_HEADS, DSA_REP * ts, PAGE)
    bias_f4 = bias_f[:, :ts, :].reshape(DSA_KV_HEADS, DSA_REP * ts, PAGE)
    n_pool = cache_k.shape[1]
    page_t = lambda c: jnp.transpose(c[0], (0, 2, 3, 1)).reshape(n_pool, LANES, PAGE)
    o_s = _sample3(pt_flat, qb_s, sc_past, sc_new, tau_s.reshape(bs, ts, 1), cut_s.reshape(bs, ts, 1),
                   pad_new_t(kb_sbf), pad_new_t(vb_sbf), bias_s4, bias_f4,
                   page_t(cache_k), page_t(cache_v), n_pages, pp)
    dsa_s = o_s.reshape(bs, DSA_KV_HEADS, DSA_REP, ts, DSA_HEAD_DIM).transpose(0, 3, 1, 2, 4)
    dsa_s = dsa_s.reshape(ns, DSA_HEADS * DSA_HEAD_DIM).astype(BF16)
    y_s = _peer(xs, mla_s, dsa_s, wts)

    def rows(a_t, nb, nt):
        return jnp.swapaxes(a_t, 1, 2).reshape(1, nb, nt, a_t.shape[1])

    kv5 = lambda a: a.reshape(a.shape[:3] + (DSA_KV_HEADS, DSA_HEAD_DIM))
    return (y_p.reshape(b, s, d), y_s.reshape(bs, ts, d),
            ckv_p.reshape(1, b, s, KV_LORA), rows(kr_p, b, s), kv5(rows(kb_p, b, s)), kv5(rows(vb_p, b, s)),
            rows(ki_p, b, s),
            ckv_s.reshape(1, bs, ts, KV_LORA), rows(kr_s, bs, ts), kv5(rows(kb_s, bs, ts)), kv5(rows(vb_s, bs, ts)),
            rows(ki_s, bs, ts))
```

```python
import functools
import math

import jax
import jax.numpy as jnp
import numpy as np
from jax import lax
from jax.experimental import pallas as pl
from jax.experimental.pallas import tpu as pltpu

F32 = jnp.float32
BF16 = jnp.bfloat16
I32 = jnp.int32

D_MODEL = 1024
PAGE = 128
MLA_HEADS = 8
MLA_NOPE = 64
MLA_ROPE = 32
MLA_V = 64
Q_LORA = 384
KV_LORA = 256
MLA_SCALE = (MLA_NOPE + MLA_ROPE) ** -0.5
DSA_HEADS = 8
DSA_KV_HEADS = 2
DSA_REP = DSA_HEADS // DSA_KV_HEADS
DSA_HEAD_DIM = 64
DSA_SCALE = DSA_HEAD_DIM ** -0.5
IDX_HEADS = 8
IDX_DIM = 64
IDX_ROPE = 32
IDX_TOPK_MAX = 256
IDX_W_SCALE = (IDX_HEADS * IDX_DIM) ** -0.5
REL_BUCKETS = 32
REL_MAX_DIST = 128
PEER_HEADS = 8
PEER_NKEYS = 128
PEER_EXPERTS = PEER_NKEYS * PEER_NKEYS
PEER_HALF = 128
PEER_TOPK = 16
ROPE_THETA = 10000.0
NORM_EPS = 1e-6

LANES = 128
MXU_N = 256
NEG = -1e30
INT_MIN = -(2 ** 31)
KCAT = KV_LORA + LANES
KV_CHUNK = 256
MLA_CHUNK = 512
Q_TILE = 128

_C_CQ, _C_CKV, _C_QB, _C_KB, _C_VB = 0, 384, 640, 1152, 1280
_C_QI, _C_QIP, _C_KR, _C_KRP, _C_KI, _C_KIP, _C_WI, _C_END = 1408, 1920, 2432, 2560, 2688, 2816, 2944, 3072
_VMEM_LIMIT = 56 * 1024 * 1024


def _cparams(n_axes):
    return pltpu.CompilerParams(dimension_semantics=("arbitrary",) * n_axes, vmem_limit_bytes=_VMEM_LIMIT)


def _dot(a, b):
    return jnp.dot(a, b, preferred_element_type=F32)


def _dot_nt(a, b):
    return lax.dot_general(a, b, (((1,), (1,)), ((), ())), preferred_element_type=F32)


def _rms(x, g):
    return x * lax.rsqrt(jnp.mean(x * x, axis=-1, keepdims=True) + NORM_EPS) * g


def _sort_key(x):
    x = jnp.where(x == 0.0, 0.0, x)
    bits = pltpu.bitcast(x, I32)
    return bits ^ ((bits >> 31) & 0x7FFFFFFF)


def _bucket_starts():
    max_exact = REL_BUCKETS // 2
    n = np.arange(0, 2 * REL_MAX_DIST, dtype=np.int64)
    nf = np.maximum(n, 1).astype(np.float32)
    large = max_exact + (np.log(nf / np.float32(max_exact)) / np.float32(math.log(REL_MAX_DIST / max_exact))
                         * np.float32(REL_BUCKETS - max_exact)).astype(np.int32)
    large = np.minimum(large, REL_BUCKETS - 1)
    bucket = np.where(n < max_exact, n, large)
    starts = []
    for k in range(REL_BUCKETS):
        hit = np.nonzero(bucket >= k)[0]
        starts.append(int(hit[0]) if hit.size else int(n[-1]) + 1)
    return starts


_BUCKET_START = _bucket_starts()


def _bias_kernel(rb_ref, bp_ref, bs_ref, bf_ref):
    def bias_of(n, h):
        b = jnp.full(n.shape, rb_ref[REL_BUCKETS - 1, h], F32)
        for k in range(REL_BUCKETS - 2, -1, -1):
            b = jnp.where(n < _BUCKET_START[k + 1], rb_ref[k, h], b)
        return b

    s_i = lax.broadcasted_iota(I32, (KV_CHUNK, Q_TILE), 0)
    t_i = lax.broadcasted_iota(I32, (KV_CHUNK, Q_TILE), 1)
    for w in range(4):
        n = jnp.maximum(w * Q_TILE + t_i - s_i, 0)
        for h in range(DSA_HEADS):
            bp_ref[w, h] = bias_of(n, h)
    t_s = lax.broadcasted_iota(I32, (8, PAGE), 0)
    u_s = lax.broadcasted_iota(I32, (8, PAGE), 1)
    for w in range(2):
        n = jnp.maximum((1 - w) * PAGE + t_s - u_s, 0)
        for h in range(DSA_HEADS):
            bs_ref[w, h] = bias_of(n, h)
    for h in range(DSA_HEADS):
        bf_ref[h] = jnp.full((8, PAGE), rb_ref[REL_BUCKETS - 1, h], F32)


def _bias_tables(rel_bias):
    return pl.pallas_call(
        _bias_kernel,
        out_shape=(jax.ShapeDtypeStruct((4, DSA_HEADS, KV_CHUNK, Q_TILE), F32),
                   jax.ShapeDtypeStruct((2, DSA_HEADS, 8, PAGE), F32),
                   jax.ShapeDtypeStruct((DSA_HEADS, 8, PAGE), F32)),
        in_specs=[pl.BlockSpec(memory_space=pltpu.SMEM)],
        name="bias_tables",
    )(rel_bias)


def _inproj_kernel(x_ref, ga_ref, w1_ref, gq_ref, wuq_ref, gkv_ref, wuk_ref, ca_ref, sa_ref, cb_ref, sb_ref,
                   ckv_ref, krope_ref, kb_ref, vb_ref, ki_ref,
                   kcat_ref, kbbf_ref, vbbf_ref, kibf_ref, qcat_ref, qb_ref, qi_ref, wi_ref):
    xn = _rms(x_ref[...], ga_ref[...]).astype(BF16)

    def proj(lo, hi):
        return _dot(xn, w1_ref[:, lo:hi])

    ca, sa, cb, sb = ca_ref[...], sa_ref[...], cb_ref[...], sb_ref[...]

    ckv = _rms(proj(_C_CKV, _C_QB), gkv_ref[...])
    ckv_ref[...] = ckv
    kcat_ref[:, 0:KV_LORA] = ckv.astype(BF16)
    kr = proj(_C_KR, _C_KRP) * ca + proj(_C_KRP, _C_KI) * sa
    krope_ref[...] = kr.T[:MLA_ROPE]
    kcat_ref[:, KV_LORA:KCAT] = kr.astype(BF16)
    ki = proj(_C_KI, _C_KIP) * ca + proj(_C_KIP, _C_WI) * sa
    ki_ref[...] = ki.T[:IDX_DIM]
    kibf_ref[...] = ki[:, :IDX_DIM].astype(BF16)
    kb = proj(_C_KB, _C_VB)
    kb_ref[...] = kb.T
    kbbf_ref[...] = kb.astype(BF16)
    vb = proj(_C_VB, _C_QI)
    vb_ref[...] = vb.T
    vbbf_ref[...] = vb.astype(BF16)
    wi_ref[...] = proj(_C_WI, _C_END)[:, :IDX_HEADS] * IDX_W_SCALE

    qb = proj(_C_QB, _C_KB)
    for h in range(DSA_HEADS):
        qb_ref[h] = qb[:, h * DSA_HEAD_DIM:(h + 1) * DSA_HEAD_DIM].astype(BF16)
    qi = proj(_C_QI, _C_QIP)
    qip = proj(_C_QIP, _C_KR)
    for s in range(4):
        slab = qi[:, s * LANES:(s + 1) * LANES] * cb + qip[:, s * LANES:(s + 1) * LANES] * sb
        qi_ref[2 * s] = slab[:, :IDX_DIM].astype(BF16)
        qi_ref[2 * s + 1] = slab[:, IDX_DIM:].astype(BF16)

    cq = _rms(proj(_C_CQ, _C_CKV), gq_ref[...]).astype(BF16)
    n_nope = MLA_HEADS * MLA_NOPE
    n_pad = MLA_HEADS * LANES
    nope = _dot(cq, wuq_ref[:, 0:n_nope]).astype(BF16)
    for p in range(MLA_HEADS // 2):
        ql = _dot(nope[:, p * LANES:(p + 1) * LANES], wuk_ref[p])
        qcat_ref[2 * p, :, 0:KV_LORA] = ql[:, :KV_LORA].astype(BF16)
        qcat_ref[2 * p + 1, :, 0:KV_LORA] = ql[:, KV_LORA:].astype(BF16)
    for h in range(MLA_HEADS):
        lo = n_nope + h * LANES
        qr = _dot(cq, wuq_ref[:, lo:lo + LANES]) * ca + _dot(cq, wuq_ref[:, lo + n_pad:lo + n_pad + LANES]) * sa
        qcat_ref[h, :, KV_LORA:KCAT] = qr.astype(BF16)


def _inproj(x2d, tabs, seq, wts, tm):
    n = x2d.shape[0]
    tab_blocks = seq // tm
    n_seq = n // seq
    const2 = lambda i: (0, 0)
    const3 = lambda i: (0, 0, 0)
    row = lambda i: (i, 0)
    tab = lambda i: (i % tab_blocks, 0)
    hm = lambda i: (0, i, 0)
    col = lambda i: (i // tab_blocks, 0, i % tab_blocks)
    t_shape = lambda width: jax.ShapeDtypeStruct((n_seq, width, seq), F32)
    t_spec = lambda width: pl.BlockSpec((None, width, tm), col)
    in_specs = [
        pl.BlockSpec((tm, D_MODEL), row),
        pl.BlockSpec((1, D_MODEL), const2),
        pl.BlockSpec((D_MODEL, _C_END), const2),
        pl.BlockSpec((1, Q_LORA), const2),
        pl.BlockSpec(wts["wuq"].shape, const2),
        pl.BlockSpec((1, KV_LORA), const2),
        pl.BlockSpec(wts["wuk"].shape, const3),
    ] + [pl.BlockSpec((tm, LANES), tab)] * 4
    out_shape = (
        jax.ShapeDtypeStruct((n, KV_LORA), F32), t_shape(MLA_ROPE), t_shape(LANES), t_shape(LANES), t_shape(IDX_DIM),
        jax.ShapeDtypeStruct((n, KCAT), BF16), jax.ShapeDtypeStruct((n, LANES), BF16),
        jax.ShapeDtypeStruct((n, LANES), BF16), jax.ShapeDtypeStruct((n, IDX_DIM), BF16),
        jax.ShapeDtypeStruct((MLA_HEADS, n, KCAT), BF16),
        jax.ShapeDtypeStruct((DSA_HEADS, n, DSA_HEAD_DIM), BF16),
        jax.ShapeDtypeStruct((IDX_HEADS, n, IDX_DIM), BF16),
        jax.ShapeDtypeStruct((n, IDX_HEADS), F32),
    )
    out_specs = (
        pl.BlockSpec((tm, KV_LORA), row), t_spec(MLA_ROPE), t_spec(LANES), t_spec(LANES), t_spec(IDX_DIM),
        pl.BlockSpec((tm, KCAT), row), pl.BlockSpec((tm, LANES), row),
        pl.BlockSpec((tm, LANES), row), pl.BlockSpec((tm, IDX_DIM), row),
        pl.BlockSpec((MLA_HEADS, tm, KCAT), hm),
        pl.BlockSpec((DSA_HEADS, tm, DSA_HEAD_DIM), hm),
        pl.BlockSpec((IDX_HEADS, tm, IDX_DIM), hm),
        pl.BlockSpec((tm, IDX_HEADS), row),
    )
    return pl.pallas_call(
        _inproj_kernel, grid=(n // tm,), in_specs=in_specs, out_specs=out_specs, out_shape=out_shape,
        compiler_params=_cparams(1), name="inproj",
    )(x2d, wts["g_attn"], wts["w1"], wts["g_q"], wts["wuq"], wts["g_kv"], wts["wuk"], *tabs)


def _rep(x, width):
    k = width // LANES
    return x if k == 1 else jnp.concatenate([x] * k, axis=1)


def _mla_prompt_kernel(q_ref, k_ref, wuv_ref, o_ref, m_scr, l_scr, acc_scr):
    j = pl.program_id(1)
    rows = MLA_HEADS * Q_TILE
    q = q_ref[...].reshape(rows, KCAT)
    m_scr[...] = jnp.full(m_scr.shape, NEG, F32)
    l_scr[...] = jnp.zeros(l_scr.shape, F32)
    acc_scr[...] = jnp.zeros(acc_scr.shape, F32)
    n_full = (j * Q_TILE) // MLA_CHUNK

    def body(c, masked):
        k = k_ref[pl.ds(pl.multiple_of(c * MLA_CHUNK, MLA_CHUNK), MLA_CHUNK), :]
        s = _dot_nt(q, k) * MLA_SCALE
        if masked:
            t_row = j * Q_TILE + lax.broadcasted_iota(I32, (rows, MLA_CHUNK), 0) % Q_TILE
            u_col = lax.broadcasted_iota(I32, (rows, MLA_CHUNK), 1)
            s = jnp.where(c * MLA_CHUNK + u_col <= t_row, s, NEG)
        m_prev = m_scr[...]
        m_new = jnp.maximum(m_prev, jnp.max(s, axis=1, keepdims=True))
        alpha = jnp.exp(m_prev - m_new)
        p = jnp.exp(s - _rep(m_new, MLA_CHUNK))
        l_scr[...] = alpha * l_scr[...] + jnp.sum(p, axis=1, keepdims=True)
        acc_scr[...] = acc_scr[...] * _rep(alpha, KV_LORA) + _dot(p.astype(BF16), k[:, :KV_LORA])
        m_scr[...] = m_new

    lax.fori_loop(0, n_full, lambda c, carry: (body(c, False), carry)[1], 0)
    body(n_full, True)
    o_lat = (acc_scr[...] / _rep(l_scr[...], KV_LORA)).astype(BF16)
    for h in range(MLA_HEADS):
        o = _dot(o_lat[h * Q_TILE:(h + 1) * Q_TILE], wuv_ref[h])
        o_ref[:, h * MLA_V:(h + 1) * MLA_V] = o.astype(BF16)


def _mla_prompt(qcat, kcat, wuv, b, s):
    nq = s // Q_TILE
    rows = MLA_HEADS * Q_TILE
    return pl.pallas_call(
        _mla_prompt_kernel, grid=(b, nq),
        in_specs=[pl.BlockSpec((MLA_HEADS, Q_TILE, KCAT), lambda bi, j: (0, bi * nq + j, 0)),
                  pl.BlockSpec((None, s, KCAT), lambda bi, j: (bi, 0, 0)),
                  pl.BlockSpec(wuv.shape, lambda bi, j: (0, 0, 0))],
        out_specs=pl.BlockSpec((Q_TILE, MLA_HEADS * MLA_V), lambda bi, j: (bi * nq + j, 0)),
        out_shape=jax.ShapeDtypeStruct((b * s, MLA_HEADS * MLA_V), BF16),
        scratch_shapes=[pltpu.VMEM((rows, LANES), F32), pltpu.VMEM((rows, LANES), F32),
                        pltpu.VMEM((rows, KV_LORA), F32)],
        compiler_params=_cparams(2), name="mla_prompt",
    )(qcat, kcat.reshape(b, s, KCAT), wuv)


def _topk_threshold(key_scr, n_chunks, topk, n_keys_pow2_bits):
    lanes = key_scr.shape[1]
    sub = KV_CHUNK // 8

    def count(pred_fn):
        def body(c, acc):
            off = pl.multiple_of(c * KV_CHUNK, KV_CHUNK)
            k = key_scr[pl.ds(off, KV_CHUNK), :]
            hit = pred_fn(k, c).astype(I32)
            return acc + jnp.sum(hit.reshape(sub, 8, lanes), axis=0)

        acc = lax.fori_loop(0, n_chunks, body, jnp.zeros((8, lanes), I32))
        return jnp.sum(acc, axis=0, keepdims=True)

    def bit_body(i, res):
        cand = res | jnp.left_shift(jnp.int32(1), 31 - i)
        cs = cand ^ INT_MIN
        cnt = count(lambda k, c: k >= cs)
        return jnp.where(cnt >= topk, cand, res)

    res = lax.fori_loop(0, 32, bit_body, jnp.zeros((1, lanes), I32))
    tau = res ^ INT_MIN
    cnt_gt = count(lambda k, c: k > tau)
    cnt_eq = count(lambda k, c: k == tau)
    need = topk - cnt_gt
    row0 = lax.broadcasted_iota(I32, (KV_CHUNK, lanes), 0)
    big = jnp.int32(2 ** 30)

    def cut_search():
        def cbody(i, cur):
            cand = cur | jnp.left_shift(jnp.int32(1), n_keys_pow2_bits - 1 - i)
            f = count(lambda k, c: jnp.where(k == tau, row0 + c * KV_CHUNK, big) < cand)
            return jnp.where(f < need, cand, cur)

        return lax.fori_loop(0, n_keys_pow2_bits, cbody, jnp.zeros((1, lanes), I32))

    cut = lax.cond(jnp.max(cnt_eq - need) > 0, cut_search, lambda: jnp.full((1, lanes), big, I32))
    return tau, cut


def _dsa_prompt_kernel(qi_ref, wt_ref, ki_ref, qb_ref, kb_ref, vt_ref, bias_ref, o_ref,
                       key_scr, mb_scr, tc_scr, m_scr, l_scr, acc_scr, *, topk, idx_bits):
    j = pl.program_id(1)
    n_chunks = (j * Q_TILE) // KV_CHUNK + 1
    t_row = j * Q_TILE + lax.broadcasted_iota(I32, (KV_CHUNK, Q_TILE), 1)
    s_loc = lax.broadcasted_iota(I32, (KV_CHUNK, Q_TILE), 0)
    qi = qi_ref[...].reshape(IDX_HEADS * Q_TILE, IDX_DIM)
    wt = wt_ref[...]

    def score_body(c, carry):
        off = pl.multiple_of(c * KV_CHUNK, KV_CHUNK)
        a = _dot_nt(ki_ref[pl.ds(off, KV_CHUNK), :], qi)
        sc = jnp.zeros((KV_CHUNK, Q_TILE), F32)
        for h in range(IDX_HEADS):
            sc = sc + wt[h:h + 1, :] * jnp.maximum(a[:, h * Q_TILE:(h + 1) * Q_TILE], 0.0)
        key = jnp.where(off + s_loc <= t_row, _sort_key(sc), INT_MIN)
        key_scr[pl.ds(off, KV_CHUNK), :] = key
        return carry

    lax.fori_loop(0, n_chunks, score_body, 0)

    @pl.when((j + 1) * Q_TILE <= topk)
    def _():
        tc_scr[0:1, :] = jnp.full((1, Q_TILE), INT_MIN, I32)
        tc_scr[1:2, :] = jnp.full((1, Q_TILE), -1, I32)

    @pl.when((j + 1) * Q_TILE > topk)
    def _():
        tau, cut = _topk_threshold(key_scr, n_chunks, topk, idx_bits)
        tc_scr[0:1, :] = tau
        tc_scr[1:2, :] = cut

    tau = tc_scr[0:1, :]
    cut = tc_scr[1:2, :]

    def mask_body(c, carry):
        off = pl.multiple_of(c * KV_CHUNK, KV_CHUNK)
        k = key_scr[pl.ds(off, KV_CHUNK), :]
        spos = off + s_loc
        v = jnp.where(k > tau, 0.0, jnp.where(k == tau, jnp.where(spos <= cut, 0.0, NEG), NEG))
        mb_scr[pl.ds(off, KV_CHUNK), :] = jnp.where(spos <= t_row, v, NEG)
        return carry

    lax.fori_loop(0, n_chunks, mask_body, 0)

    m_scr[...] = jnp.full(m_scr.shape, NEG, F32)
    l_scr[...] = jnp.zeros(l_scr.shape, F32)
    acc_scr[...] = jnp.zeros(acc_scr.shape, F32)
    qb = qb_ref[...]

    def att_body(c, carry):
        off = pl.multiple_of(c * KV_CHUNK, KV_CHUNK)
        kb = kb_ref[pl.ds(off, KV_CHUNK), :]
        mb = mb_scr[pl.ds(off, KV_CHUNK), :]
        bidx = jnp.minimum((j * Q_TILE - c * KV_CHUNK) // Q_TILE, 3)
        for g in range(DSA_KV_HEADS):
            kg = kb[:, g * DSA_HEAD_DIM:(g + 1) * DSA_HEAD_DIM]
            qg = qb[g * DSA_REP:(g + 1) * DSA_REP].reshape(DSA_REP * Q_TILE, DSA_HEAD_DIM)
            lg4 = _dot_nt(kg, qg) * DSA_SCALE
            vg = vt_ref[c, g * DSA_HEAD_DIM:(g + 1) * DSA_HEAD_DIM, :]
            for r in range(DSA_REP):
                h = g * DSA_REP + r
                lg = lg4[:, r * Q_TILE:(r + 1) * Q_TILE] + bias_ref[bidx, h] + mb
                m_prev = m_scr[h:h + 1, :]
                m_new = jnp.maximum(m_prev, jnp.max(lg, axis=0, keepdims=True))
                alpha = jnp.exp(m_prev - m_new)
                p = jnp.exp(lg - m_new)
                l_scr[h:h + 1, :] = alpha * l_scr[h:h + 1, :] + jnp.sum(p, axis=0, keepdims=True)
                rs = slice(h * DSA_HEAD_DIM, (h + 1) * DSA_HEAD_DIM)
                acc_scr[rs, :] = alpha * acc_scr[rs, :] + _dot(vg, p.astype(BF16))
                m_scr[h:h + 1, :] = m_new
        return carry

    lax.fori_loop(0, n_chunks, att_body, 0)
    inv = 1.0 / l_scr[...]
    parts = [acc_scr[h * DSA_HEAD_DIM:(h + 1) * DSA_HEAD_DIM, :] * inv[h:h + 1, :] for h in range(DSA_HEADS)]
    o_ref[...] = jnp.concatenate(parts, axis=0).T.astype(BF16)


def _dsa_prompt(qi_hm, wi_t, ki_bf, qb_hm, kb_bf, v_t, bias_p, b, s):
    nq = s // Q_TILE
    topk = min(IDX_TOPK_MAX, s // 4)
    idx_bits = max(1, int(math.ceil(math.log2(s))))
    width = DSA_HEADS * DSA_HEAD_DIM
    kern = functools.partial(_dsa_prompt_kernel, topk=topk, idx_bits=idx_bits)
    return pl.pallas_call(
        kern, grid=(b, nq),
        in_specs=[pl.BlockSpec((IDX_HEADS, Q_TILE, IDX_DIM), lambda bi, j: (0, bi * nq + j, 0)),
                  pl.BlockSpec((IDX_HEADS, Q_TILE), lambda bi, j: (0, bi * nq + j)),
                  pl.BlockSpec((None, s, IDX_DIM), lambda bi, j: (bi, 0, 0)),
                  pl.BlockSpec((DSA_HEADS, Q_TILE, DSA_HEAD_DIM), lambda bi, j: (0, bi * nq + j, 0)),
                  pl.BlockSpec((None, s, LANES), lambda bi, j: (bi, 0, 0)),
                  pl.BlockSpec((None, s // KV_CHUNK, LANES, KV_CHUNK), lambda bi, j: (bi, 0, 0, 0)),
                  pl.BlockSpec(bias_p.shape, lambda bi, j: (0, 0, 0, 0))],
        out_specs=pl.BlockSpec((Q_TILE, width), lambda bi, j: (bi * nq + j, 0)),
        out_shape=jax.ShapeDtypeStruct((b * s, width), BF16),
        scratch_shapes=[pltpu.VMEM((s, Q_TILE), I32), pltpu.VMEM((s, Q_TILE), F32), pltpu.VMEM((8, Q_TILE), I32),
                        pltpu.VMEM((DSA_HEADS, Q_TILE), F32), pltpu.VMEM((DSA_HEADS, Q_TILE), F32),
                        pltpu.VMEM((width, Q_TILE), F32)],
        compiler_params=_cparams(2), name="dsa_prompt",
    )(qi_hm, wi_t, ki_bf.reshape(b, s, IDX_DIM), qb_hm, kb_bf.reshape(b, s, LANES), v_t, bias_p)


def _mxu_tiles(pages, axis):
    group = MXU_N // PAGE
    return [pages[i] if len(pages[i:i + group]) == 1 else jnp.concatenate(pages[i:i + group], axis=axis)
            for i in range(0, len(pages), group)]


def _sample1_kernel(pt_ref, ql_ref, qr_ref, qi_ref, wi_ref, ckvn_ref, krn_ref, kin_ref, ckv_hbm, kr_hbm, ki_hbm,
                    olat_ref, sc_ref, scn_ref, ckv_buf, kr_buf, ki_buf, sem, m_scr, l_scr, acc_scr, *, pp, n_pages):
    slot = _page_fetch(pt_ref, (ckv_hbm, kr_hbm, ki_hbm), (ckv_buf, kr_buf, ki_buf), sem, pp, n_pages)
    ckv_pages = [ckv_buf.at[slot, k] for k in range(pp)]
    kr_pages = [kr_buf.at[slot, k] for k in range(pp)]
    ki_pages = [ki_buf.at[slot, k] for k in range(pp)]
    j = pl.program_id(1)
    last = pl.num_programs(1) - 1
    rows = ql_ref.shape[0]
    n_tok = rows // MLA_HEADS

    @pl.when(j == 0)
    def _():
        m_scr[...] = jnp.full(m_scr.shape, NEG, F32)
        l_scr[...] = jnp.zeros(l_scr.shape, F32)
        acc_scr[...] = jnp.zeros(acc_scr.shape, F32)

    ql, qr, qi, wi = ql_ref[...], qr_ref[...], qi_ref[...], wi_ref[...]

    def attend(kcs, krs, mask):
        kcs, krs = _mxu_tiles(kcs, axis=0), _mxu_tiles(krs, axis=1)
        s = jnp.concatenate([_dot_nt(ql, kc) + _dot(qr, kr) for kc, kr in zip(kcs, krs)], axis=1) * MLA_SCALE
        if mask is not None:
            s = jnp.where(mask, s, NEG)
        m_prev = m_scr[...]
        m_new = jnp.maximum(m_prev, jnp.max(s, axis=1, keepdims=True))
        alpha = jnp.exp(m_prev - m_new)
        p = jnp.exp(s - _rep(m_new, s.shape[1])).astype(BF16)
        l_scr[...] = alpha * l_scr[...] + jnp.sum(p.astype(F32), axis=1, keepdims=True)
        pv, off = None, 0
        for kc in kcs:
            d = _dot(p[:, off:off + kc.shape[0]], kc)
            pv = d if pv is None else pv + d
            off += kc.shape[0]
        acc_scr[...] = acc_scr[...] * _rep(alpha, KV_LORA) + pv
        m_scr[...] = m_new

    def index(kidx_t):
        a = jnp.maximum(_dot(qi, kidx_t), 0.0) * wi
        return jnp.sum(a.reshape(n_tok, IDX_HEADS, kidx_t.shape[1]), axis=1)

    attend([r[...].astype(BF16) for r in ckv_pages], [r[...].astype(BF16) for r in kr_pages], None)
    off = 0
    for kt in _mxu_tiles([r[...].astype(BF16) for r in ki_pages], axis=1):
        sc_ref[:, off:off + kt.shape[1]] = index(kt)
        off += kt.shape[1]

    @pl.when(j == last)
    def _():
        t_r = lax.broadcasted_iota(I32, (rows, PAGE), 0) % n_tok
        u_c = lax.broadcasted_iota(I32, (rows, PAGE), 1)
        attend([ckvn_ref[...]], [krn_ref[...]], u_c <= t_r)
        t4 = lax.broadcasted_iota(I32, (n_tok, PAGE), 0)
        u4 = lax.broadcasted_iota(I32, (n_tok, PAGE), 1)
        scn_ref[...] = jnp.where(u4 <= t4, index(kin_ref[...]), -jnp.inf)
        olat_ref[...] = acc_scr[...] / _rep(l_scr[...], KV_LORA)


def _sample1(pt_flat, ql, qr, qi, wi, ckvn, krn_t, kin_t, c_ckv, c_kr_t, c_ki_t, n_pages, pp):
    bs, rows = ql.shape[0], ql.shape[1]
    n_tok = rows // MLA_HEADS
    per_b = lambda tail: pl.BlockSpec((None,) + tail, lambda bi, j, pt: (bi,) + (0,) * len(tail))
    in_specs = [per_b((rows, KV_LORA)), per_b((rows, MLA_ROPE)), per_b((rows, IDX_DIM)), per_b((rows, 1)),
                per_b((PAGE, KV_LORA)), per_b((MLA_ROPE, PAGE)), per_b((IDX_DIM, PAGE))]
    in_specs += [pl.BlockSpec(memory_space=pl.ANY)] * 3
    out_specs = (per_b((rows, KV_LORA)),
                 pl.BlockSpec((None, n_tok, pp * PAGE), lambda bi, j, pt: (bi, 0, j)),
                 per_b((n_tok, PAGE)))
    out_shape = (jax.ShapeDtypeStruct((bs, rows, KV_LORA), F32),
                 jax.ShapeDtypeStruct((bs, n_tok, n_pages * PAGE), F32),
                 jax.ShapeDtypeStruct((bs, n_tok, PAGE), F32))
    grid_spec = pltpu.PrefetchScalarGridSpec(
        num_scalar_prefetch=1, grid=(bs, n_pages // pp), in_specs=in_specs, out_specs=out_specs,
        scratch_shapes=[pltpu.VMEM((2, pp, PAGE, KV_LORA), F32), pltpu.VMEM((2, pp, MLA_ROPE, PAGE), F32),
                        pltpu.VMEM((2, pp, IDX_DIM, PAGE), F32), pltpu.SemaphoreType.DMA((2, 3)),
                        pltpu.VMEM((rows, LANES), F32), pltpu.VMEM((rows, LANES), F32),
                        pltpu.VMEM((rows, KV_LORA), F32)])
    return pl.pallas_call(
        functools.partial(_sample1_kernel, pp=pp, n_pages=n_pages), grid_spec=grid_spec, out_shape=out_shape,
        compiler_params=_cparams(2), name="sample_mla_index",
    )(pt_flat, ql, qr, qi, wi, ckvn, krn_t, kin_t, c_ckv, c_kr_t, c_ki_t)


def _mla_out_kernel(o_ref, wuv_ref, out_ref):
    for h in range(MLA_HEADS):
        out_ref[:, h * MLA_V:(h + 1) * MLA_V] = _dot(o_ref[h], wuv_ref[h]).astype(BF16)


def _mla_out(olat_hm, wuv):
    n = olat_hm.shape[1]
    return pl.pallas_call(
        _mla_out_kernel, out_shape=jax.ShapeDtypeStruct((n, MLA_HEADS * MLA_V), BF16), name="sample_mla_out",
    )(olat_hm, wuv)


def _sample_select_kernel(sc_ref, tau_ref, cut_ref, key_scr, *, topk, idx_bits):
    n_chunks = sc_ref.shape[0] // KV_CHUNK

    def kbody(c, carry):
        off = pl.multiple_of(c * KV_CHUNK, KV_CHUNK)
        key_scr[pl.ds(off, KV_CHUNK), :] = _sort_key(sc_ref[pl.ds(off, KV_CHUNK), :])
        return carry

    lax.fori_loop(0, n_chunks, kbody, 0)
    tau, cut = _topk_threshold(key_scr, n_chunks, topk, idx_bits)
    tau_ref[...] = tau
    cut_ref[...] = cut


def _sample_select(sc_t, topk):
    kp, ns = sc_t.shape
    lt = min(LANES, ns)
    idx_bits = max(1, int(math.ceil(math.log2(kp))))
    kern = functools.partial(_sample_select_kernel, topk=topk, idx_bits=idx_bits)
    return pl.pallas_call(
        kern, grid=(ns // lt,),
        in_specs=[pl.BlockSpec((kp, lt), lambda i: (0, i))],
        out_specs=(pl.BlockSpec((1, lt), lambda i: (0, i)), pl.BlockSpec((1, lt), lambda i: (0, i))),
        out_shape=(jax.ShapeDtypeStruct((1, ns), I32), jax.ShapeDtypeStruct((1, ns), I32)),
        scratch_shapes=[pltpu.VMEM((kp, lt), I32)],
        compiler_params=_cparams(1), name="sample_select",
    )(sc_t)


def _page_fetch(pt_ref, caches, bufs, sem, pp, n_pages):
    bi, j, nj = pl.program_id(0), pl.program_id(1), pl.num_programs(1)
    step = bi * nj + j
    slot = step % 2

    def copies(first_page, sl):
        out = []
        for k in range(pp):
            page = 0 if first_page is None else pt_ref[first_page + k]
            for i, (cache, buf) in enumerate(zip(caches, bufs)):
                out.append(pltpu.make_async_copy(cache.at[page], buf.at[sl, k], sem.at[sl, i]))
        return out

    @pl.when(step == 0)
    def _():
        for c in copies(0, 0):
            c.start()

    @pl.when(step + 1 < pl.num_programs(0) * nj)
    def _():
        nxt = step + 1
        for c in copies((nxt // nj) * n_pages + (nxt % nj) * pp, 1 - slot):
            c.start()

    for c in copies(None, slot):
        c.wait()
    return slot


def _sample3_kernel(pt_ref, qb_ref, sc_ref, scn_ref, tau_ref, cut_ref, kn_ref, vn_ref, bs_ref, bf_ref, ck_hbm, cv_hbm,
                    o_ref, kbuf, vbuf, sem, m_scr, l_scr, acc_scr, *, pp, past):
    slot = _page_fetch(pt_ref, (ck_hbm, cv_hbm), (kbuf, vbuf), sem, pp, past // PAGE)
    k_pages = [kbuf.at[slot, k] for k in range(pp)]
    v_pages = [vbuf.at[slot, k] for k in range(pp)]
    j = pl.program_id(1)
    last = pl.num_programs(1) - 1
    n_tok = sc_ref.shape[0]

    @pl.when(j == 0)
    def _():
        m_scr[...] = jnp.full(m_scr.shape, NEG, F32)
        l_scr[...] = jnp.zeros(l_scr.shape, F32)
        acc_scr[...] = jnp.zeros(acc_scr.shape, F32)

    tau, cut = tau_ref[...], cut_ref[...]

    def mask_bias(sc, base):
        k = _sort_key(sc)
        spos = base + lax.broadcasted_iota(I32, sc.shape, 1)
        return jnp.where(k > tau, 0.0, jnp.where(k == tau, jnp.where(spos <= cut, 0.0, NEG), NEG))

    def attend(kts, vts, mb4, biases):
        mb = jnp.concatenate([mb4] * DSA_REP, axis=0)
        kts, vts = _mxu_tiles(kts, axis=1), _mxu_tiles(vts, axis=1)
        for g in range(DSA_KV_HEADS):
            rs = slice(g * DSA_HEAD_DIM, (g + 1) * DSA_HEAD_DIM)
            lg = jnp.concatenate([_dot(qb_ref[g], kt[rs, :]) for kt in kts], axis=1) * DSA_SCALE
            lg = lg + jnp.concatenate([b[g] for b in biases], axis=1) + mb
            m_prev = m_scr[g]
            m_new = jnp.maximum(m_prev, jnp.max(lg, axis=1, keepdims=True))
            alpha = jnp.exp(m_prev - m_new)
            p = jnp.exp(lg - _rep(m_new, lg.shape[1])).astype(BF16)
            l_scr[g] = alpha * l_scr[g] + jnp.sum(p.astype(F32), axis=1, keepdims=True)
            pv, off = None, 0
            for vt in vts:
                d = _dot_nt(p[:, off:off + vt.shape[1]], vt[rs, :])
                pv = d if pv is None else pv + d
                off += vt.shape[1]
            acc_scr[g] = acc_scr[g] * alpha[:, :DSA_HEAD_DIM] + pv
            m_scr[g] = m_new

    far = bf_ref[...]
    biases = [far] * (pp - 1) + [jnp.where(j == last, bs_ref[0], far)]
    attend([r[...].astype(BF16) for r in k_pages], [r[...].astype(BF16) for r in v_pages],
           mask_bias(sc_ref[...], j * (pp * PAGE)), biases)

    @pl.when(j == last)
    def _():
        attend([kn_ref[...]], [vn_ref[...]], mask_bias(scn_ref[...], past), [bs_ref[1]])
        for g in range(DSA_KV_HEADS):
            o_ref[g] = acc_scr[g] / l_scr[g][:, :DSA_HEAD_DIM]


def _sample3(pt_flat, qb, sc, scn, tau, cut, kn_t, vn_t, bias_s, bias_f, c_kt, c_vt, n_pages, pp):
    bs, n_tok = sc.shape[0], sc.shape[1]
    rows = DSA_REP * n_tok
    per_b = lambda tail: pl.BlockSpec((None,) + tail, lambda bi, j, pt: (bi,) + (0,) * len(tail))
    const = lambda shape: pl.BlockSpec(shape, lambda bi, j, pt: (0,) * len(shape))
    in_specs = [per_b((DSA_KV_HEADS, rows, DSA_HEAD_DIM)),
                pl.BlockSpec((None, n_tok, pp * PAGE), lambda bi, j, pt: (bi, 0, j)),
                per_b((n_tok, PAGE)), per_b((n_tok, 1)), per_b((n_tok, 1)),
                per_b((LANES, PAGE)), per_b((LANES, PAGE)),
                const(bias_s.shape), const(bias_f.shape),
                pl.BlockSpec(memory_space=pl.ANY), pl.BlockSpec(memory_space=pl.ANY)]
    grid_spec = pltpu.PrefetchScalarGridSpec(
        num_scalar_prefetch=1, grid=(bs, n_pages // pp), in_specs=in_specs,
        out_specs=per_b((DSA_KV_HEADS, rows, DSA_HEAD_DIM)),
        scratch_shapes=[pltpu.VMEM((2, pp, LANES, PAGE), F32), pltpu.VMEM((2, pp, LANES, PAGE), F32),
                        pltpu.SemaphoreType.DMA((2, 2)),
                        pltpu.VMEM((DSA_KV_HEADS, rows, LANES), F32), pltpu.VMEM((DSA_KV_HEADS, rows, LANES), F32),
                        pltpu.VMEM((DSA_KV_HEADS, rows, DSA_HEAD_DIM), F32)])
    kern = functools.partial(_sample3_kernel, pp=pp, past=n_pages * PAGE)
    return pl.pallas_call(
        kern, grid_spec=grid_spec,
        out_shape=jax.ShapeDtypeStruct((bs, DSA_KV_HEADS, rows, DSA_HEAD_DIM), F32),
        compiler_params=_cparams(2), name="sample_dsa",
    )(pt_flat, qb, sc, scn, tau, cut, kn_t, vn_t, bias_s, bias_f, c_kt, c_vt)


_N_EXTRACT = PEER_TOPK + 1


def _extract_top(cur, n):
    vals = []
    for _ in range(n):
        m = jnp.max(cur, axis=0, keepdims=True)
        vals.append(m)
        cur = jnp.where(cur == m, -jnp.inf, cur)
    return vals


def _peer_prep_kernel(x_ref, mla_ref, dsa_ref, wo_ref, g_ref, wpq_ref, keys_ref,
                      h_ref, xnt_ref, thr_ref, a_ref, s2_ref, b_ref):
    half = wo_ref.shape[0] // 2
    h = x_ref[...] + _dot(mla_ref[...], wo_ref[0:half, :]) + _dot(dsa_ref[...], wo_ref[half:, :])
    h_ref[...] = h
    xnt = _rms(h, g_ref[...]).T.astype(BF16)
    xnt_ref[...] = xnt
    tc = xnt.shape[1]
    r8 = lax.broadcasted_iota(I32, (8, LANES), 0)
    for hh in range(PEER_HEADS):
        for p, ref in ((0, thr_ref), (1, s2_ref)):
            hp = hh * 2 + p
            qt = _dot(wpq_ref[hp * PEER_HALF:(hp + 1) * PEER_HALF, :], xnt)
            ref[hh] = _dot(keys_ref[hp], qt.astype(BF16))
        for lt in range(tc // LANES):
            ls = slice(lt * LANES, (lt + 1) * LANES)
            s1, s2 = thr_ref[hh, :, ls], s2_ref[hh, :, ls]
            sv1 = _extract_top(s1, _N_EXTRACT)
            sv2 = _extract_top(s2, _N_EXTRACT)
            sv2_16 = jnp.concatenate(sv2[:PEER_TOPK], axis=0)
            sv2_8 = sv2_16[:8]
            blocks = [sv1[0] + sv2_16]
            for r1 in range(1, 8):
                blocks.append(jnp.where(r8 < PEER_TOPK // (r1 + 1), sv1[r1] + sv2_8, -jnp.inf))
            blocks.append(jnp.concatenate(sv1[8:PEER_TOPK], axis=0) + sv2[0])
            extra = jnp.where(r8 == 0, sv1[0] + sv2[PEER_TOPK],
                              jnp.where(r8 == 1, sv1[PEER_TOPK] + sv2[0], -jnp.inf))
            blocks.append(extra)
            cand = _extract_top(jnp.concatenate(blocks, axis=0), _N_EXTRACT)
            m0 = sv1[0] + sv2[0]
            z = jnp.zeros_like(m0)
            for r in range(PEER_TOPK):
                z = z + jnp.exp(cand[r] - m0)
            c16, c17 = cand[PEER_TOPK - 1], cand[PEER_TOPK]
            tau = jnp.where(c17 == -jnp.inf, c16, 0.5 * (c16 + c17))
            thr_ref[hh, :, ls] = tau - s1
            a_ref[hh, :, ls] = jnp.exp(s1 - sv1[0]) / z
            b_ref[hh, :, ls] = jnp.exp(s2 - sv2[0])


def _peer_prep(x2d, mla, dsa, wts, tc):
    n = x2d.shape[0]
    row = lambda i: (i, 0)
    const2 = lambda i: (0, 0)
    col3 = lambda i: (0, 0, i)
    gate_shape = jax.ShapeDtypeStruct((PEER_HEADS, PEER_NKEYS, n), F32)
    gate_spec = pl.BlockSpec((PEER_HEADS, PEER_NKEYS, tc), col3)
    mix = mla.shape[1]
    return pl.pallas_call(
        _peer_prep_kernel, grid=(n // tc,),
        in_specs=[pl.BlockSpec((tc, D_MODEL), row), pl.BlockSpec((tc, mix), row), pl.BlockSpec((tc, mix), row),
                  pl.BlockSpec(wts["w_out"].shape, const2), pl.BlockSpec((1, D_MODEL), const2),
                  pl.BlockSpec(wts["wpq_t"].shape, const2), pl.BlockSpec(wts["peer_keys"].shape, lambda i: (0, 0, 0))],
        out_specs=(pl.BlockSpec((tc, D_MODEL), row), pl.BlockSpec((D_MODEL, tc), lambda i: (0, i)),
                   gate_spec, gate_spec, gate_spec, gate_spec),
        out_shape=(jax.ShapeDtypeStruct((n, D_MODEL), F32), jax.ShapeDtypeStruct((D_MODEL, n), BF16),
                   gate_shape, gate_shape, gate_shape, gate_shape),
        compiler_params=_cparams(1), name="peer_prep",
    )(x2d, mla, dsa, wts["w_out"], wts["g_ffn"], wts["wpq_t"], wts["peer_keys"])


def _gelu(x):
    return 0.5 * x * (1.0 + lax.erf(x * np.float32(math.sqrt(0.5))))


def _peer_chain_kernel(xnt_ref, thr_ref, a_ref, s2_ref, b_ref, u_ref, vt_ref, h_ref, gf_ref, y_ref, acc_scr, *, ni):
    e = pl.program_id(1)

    @pl.when(e == 0)
    def _():
        acc_scr[...] = jnp.zeros(acc_scr.shape, F32)

    tc = acc_scr.shape[1]
    tw = min(tc, MXU_N)
    per_slice = MXU_N // PEER_NKEYS
    chains = [(ks, k) for ks in range(ni // per_slice) for k in range(tc // tw)]
    n_r = D_MODEL // MXU_N

    def gates(ks, k):
        out = []
        for i2 in range(per_slice):
            i1 = e * ni + ks * per_slice + i2
            thr_rows = [thr_ref[hh, pl.ds(i1, 1), :] for hh in range(PEER_HEADS)]
            a_rows = [a_ref[hh, pl.ds(i1, 1), :] for hh in range(PEER_HEADS)]
            for lt in range(tw // LANES):
                ls = slice(k * tw + lt * LANES, k * tw + (lt + 1) * LANES)
                gate = None
                for hh in range(PEER_HEADS):
                    term = jnp.where(s2_ref[hh, :, ls] >= thr_rows[hh][:, ls], b_ref[hh, :, ls], 0.0)
                    term = term * a_rows[hh][:, ls]
                    gate = term if gate is None else gate + term
                out.append(gate)
        return out

    g_next = gates(*chains[0])
    for c, (ks, k) in enumerate(chains):
        es = slice(ks * MXU_N, (ks + 1) * MXU_N)
        cs = slice(k * tw, (k + 1) * tw)
        g_cur = g_next
        act = _gelu(_dot(u_ref[es, :], xnt_ref[:, cs]))
        if c + 1 < len(chains):
            g_next = gates(*chains[c + 1])
        n_lt = tw // LANES
        parts = []
        for i2 in range(per_slice):
            tiles = [(g_cur[i2 * n_lt + lt] * act[i2 * PEER_NKEYS:(i2 + 1) * PEER_NKEYS,
                                                   lt * LANES:(lt + 1) * LANES]).astype(BF16) for lt in range(n_lt)]
            parts.append(tiles[0] if n_lt == 1 else jnp.concatenate(tiles, axis=1))
        w = jnp.concatenate(parts, axis=0)
        for r in range(n_r):
            rs = slice(r * MXU_N, (r + 1) * MXU_N)
            acc_scr[rs, cs] += _dot(vt_ref[rs, es], w)

    @pl.when(e == pl.num_programs(1) - 1)
    def _():
        y_ref[...] = _rms(acc_scr[...].T + h_ref[...], gf_ref[...])


def _peer_chain(xnt, thr, a, s2, b, h, wts, tc, ni):
    n = h.shape[0]
    eb = ni * PEER_NKEYS
    gate_spec = pl.BlockSpec((PEER_HEADS, PEER_NKEYS, tc), lambda i, e: (0, 0, i))
    return pl.pallas_call(
        functools.partial(_peer_chain_kernel, ni=ni), grid=(n // tc, PEER_EXPERTS // eb),
        in_specs=[pl.BlockSpec((D_MODEL, tc), lambda i, e: (0, i)), gate_spec, gate_spec, gate_spec, gate_spec,
                  pl.BlockSpec((eb, D_MODEL), lambda i, e: (e, 0)), pl.BlockSpec((D_MODEL, eb), lambda i, e: (0, e)),
                  pl.BlockSpec((tc, D_MODEL), lambda i, e: (i, 0)), pl.BlockSpec((1, D_MODEL), lambda i, e: (0, 0))],
        out_specs=pl.BlockSpec((tc, D_MODEL), lambda i, e: (i, 0)),
        out_shape=jax.ShapeDtypeStruct((n, D_MODEL), F32),
        scratch_shapes=[pltpu.VMEM((D_MODEL, tc), F32)],
        compiler_params=_cparams(2), name="peer_main",
    )(xnt, thr, a, s2, b, wts["peer_u"], wts["peer_vt"], h, wts["g_final"])


def _peer(x2d, mla, dsa, wts):
    n = x2d.shape[0]
    tc = min(512, n)
    h, xnt, thr, a, s2, b = _peer_prep(x2d, mla, dsa, wts, tc)
    return _peer_chain(xnt, thr, a, s2, b, h, wts, tc, ni=16)


def _pad_cols(w, n):
    return jnp.pad(w, ((0, 0), (0, n - w.shape[1])))


def _swap_halves(w):
    half = w.shape[-1] // 2
    return jnp.concatenate([w[..., half:], w[..., :half]], axis=-1)


def _prep_weights(g_attn, w_in, g_q, w_uq, g_kv, w_uk, w_uv, w_out, g_ffn, w_pq, peer_keys, peer_u, peer_v, g_final):
    w_cq, w_ckv, w_kr = w_in[:, 0:384], w_in[:, 384:640], w_in[:, 640:672]
    w_qb, w_kb, w_vb = w_in[:, 672:1184], w_in[:, 1184:1312], w_in[:, 1312:1440]
    w_qi, w_ki, w_wi = w_in[:, 1440:1952], w_in[:, 1952:2016], w_in[:, 2016:2024]
    qi3 = w_qi.reshape(D_MODEL, IDX_HEADS, IDX_DIM)
    qi_partner = jnp.concatenate([_swap_halves(qi3[..., :IDX_ROPE]), jnp.zeros_like(qi3[..., IDX_ROPE:])], axis=-1)
    w1 = jnp.concatenate([
        w_cq, w_ckv, w_qb, w_kb, w_vb, w_qi, qi_partner.reshape(D_MODEL, IDX_HEADS * IDX_DIM),
        _pad_cols(w_kr, LANES), _pad_cols(_swap_halves(w_kr), LANES),
        _pad_cols(w_ki, LANES), _pad_cols(_swap_halves(w_ki[:, :IDX_ROPE]), LANES),
        _pad_cols(w_wi, LANES)], axis=1).astype(BF16)
    uq3 = w_uq.reshape(Q_LORA, MLA_HEADS, MLA_NOPE + MLA_ROPE)
    rope3 = uq3[..., MLA_NOPE:]
    pad3 = lambda w: jnp.pad(w, ((0, 0), (0, 0), (0, LANES - MLA_ROPE))).reshape(Q_LORA, MLA_HEADS * LANES)
    wuq = jnp.concatenate([uq3[..., :MLA_NOPE].reshape(Q_LORA, MLA_HEADS * MLA_NOPE),
                           pad3(rope3), pad3(_swap_halves(rope3))], axis=1).astype(BF16)
    ukt = jnp.transpose(w_uk, (1, 2, 0))
    zero = jnp.zeros((MLA_NOPE, KV_LORA), F32)
    wuk = jnp.stack([jnp.concatenate([jnp.concatenate([ukt[2 * p], zero], axis=1),
                                      jnp.concatenate([zero, ukt[2 * p + 1]], axis=1)], axis=0)
                     for p in range(MLA_HEADS // 2)]).astype(BF16)
    return dict(
        g_attn=g_attn.reshape(1, -1), w1=w1, g_q=g_q.reshape(1, -1), wuq=wuq, g_kv=g_kv.reshape(1, -1), wuk=wuk,
        wuv=jnp.transpose(w_uv, (1, 0, 2)).astype(BF16), w_out=w_out.astype(BF16), g_ffn=g_ffn.reshape(1, -1),
        wpq_t=w_pq.T.astype(BF16),
        peer_keys=peer_keys.reshape(PEER_HEADS * 2, PEER_NKEYS, PEER_HALF).astype(BF16),
        peer_u=peer_u.astype(BF16), peer_vt=peer_v.T.astype(BF16), g_final=g_final.reshape(1, -1))


def _rope_tables(pos):
    half = MLA_ROPE // 2
    inv = ROPE_THETA ** (-jnp.arange(half, dtype=F32) / half)
    ang = pos.astype(F32)[:, None] * inv
    cos, sin = jnp.cos(ang), jnp.sin(ang)
    c32 = jnp.concatenate([cos, cos], axis=1)
    s32 = jnp.concatenate([-sin, sin], axis=1)
    n = pos.shape[0]
    one, zero = jnp.ones((n, 32), F32), jnp.zeros((n, 32), F32)
    ca = jnp.concatenate([c32, one, one, one], axis=1)
    sa = jnp.concatenate([s32, zero, zero, zero], axis=1)
    cb = jnp.concatenate([c32, one, c32, one], axis=1)
    sb = jnp.concatenate([s32, zero, s32, zero], axis=1)
    return ca, sa, cb, sb


def _pick_tile(n, choices):
    for c in choices:
        if n % c == 0:
            return c
    raise ValueError(f"no tile in {choices} divides {n}")


def kernel(x_prompt, x_sample, cache_ckv, cache_krope, cache_k, cache_v, cache_kidx, page_table, rel_bias, g_attn,
           w_in, g_q, w_uq, g_kv, w_uk, w_uv, w_out, g_ffn, w_pq, peer_keys, peer_u, peer_v, g_final):
    assert g_attn.shape[0] == 1, "single-layer kernel"
    b, s, d = x_prompt.shape
    bs, ts, _ = x_sample.shape
    n_pages = page_table.shape[1]
    past = n_pages * PAGE
    assert s % MLA_CHUNK == 0 and s % KV_CHUNK == 0 and ts <= 8 and (bs * ts) % LANES == 0
    wts = _prep_weights(g_attn[0], w_in[0], g_q[0], w_uq[0], g_kv[0], w_uk[0], w_uv[0], w_out[0], g_ffn[0],
                        w_pq[0], peer_keys[0], peer_u[0], peer_v[0], g_final)
    bias_p, bias_s, bias_f = _bias_tables(rel_bias)

    xp = x_prompt.reshape(b * s, d)
    tm = _pick_tile(s, (512, 256))
    (ckv_p, kr_p, kb_p, vb_p, ki_p, kcat, kb_bf, vb_bf, ki_bf, qcat, qb_hm, qi_hm, wi_p) = _inproj(
        xp, _rope_tables(jnp.arange(s)), s, wts, tm)
    mla_p = _mla_prompt(qcat, kcat, wts["wuv"], b, s)
    v_t = jnp.transpose(vb_bf.reshape(b, s // KV_CHUNK, KV_CHUNK, LANES), (0, 1, 3, 2))
    dsa_p = _dsa_prompt(qi_hm, wi_p.T, ki_bf, qb_hm, kb_bf, v_t, bias_p, b, s)
    y_p = _peer(xp, mla_p, dsa_p, wts)

    ns = bs * ts
    xs = x_sample.reshape(ns, d)
    pos_s = past + jnp.tile(jnp.arange(ts), bs)
    (ckv_s, kr_s, kb_s, vb_s, ki_s, kcat_s, kb_sbf, vb_sbf, ki_sbf, qcat_s, qb_shm, qi_shm, wi_s) = _inproj(
        xs, _rope_tables(pos_s), ns, wts, ns)
    pt_flat = page_table.reshape(-1).astype(I32)
    pp = _pick_tile(n_pages, (16, 8, 4, 2, 1))
    q5 = qcat_s.reshape(MLA_HEADS, bs, ts, KCAT).transpose(1, 0, 2, 3).reshape(bs, MLA_HEADS * ts, KCAT)
    qi_s = qi_shm.reshape(IDX_HEADS, bs, ts, IDX_DIM).transpose(1, 2, 0, 3).reshape(bs, ts * IDX_HEADS, IDX_DIM)
    wi_col = wi_s.reshape(bs, ts * IDX_HEADS, 1)
    pad_new = lambda a: jnp.pad(a.reshape(bs, ts, a.shape[-1]), ((0, 0), (0, PAGE - ts), (0, 0)))
    pad_new_t = lambda a: jnp.swapaxes(pad_new(a), 1, 2)
    olat, sc_past, sc_new = _sample1(
        pt_flat, q5[..., :KV_LORA], q5[..., KV_LORA:KV_LORA + MLA_ROPE], qi_s, wi_col,
        pad_new(kcat_s[:, :KV_LORA]), pad_new_t(kcat_s[:, KV_LORA:KV_LORA + MLA_ROPE]), pad_new_t(ki_sbf),
        cache_ckv[0], jnp.swapaxes(cache_krope[0], 1, 2), jnp.swapaxes(cache_kidx[0], 1, 2), n_pages, pp)
    olat_hm = olat.reshape(bs, MLA_HEADS, ts, KV_LORA).transpose(1, 0, 2, 3).reshape(MLA_HEADS, ns, KV_LORA)
    mla_s = _mla_out(olat_hm.astype(BF16), wts["wuv"])
    topk_s = min(IDX_TOPK_MAX, (past + ts) // 4)
    kp = -(-(past + PAGE) // KV_CHUNK) * KV_CHUNK
    sc_all = jnp.concatenate([sc_past, sc_new], axis=2).reshape(ns, past + PAGE)
    sc_t = jnp.pad(sc_all, ((0, 0), (0, kp - past - PAGE)), constant_values=-jnp.inf).T
    tau_s, cut_s = _sample_select(sc_t, topk_s)
    qb_s = qb_shm.reshape(DSA_KV_HEADS, DSA_REP, bs, ts, DSA_HEAD_DIM).transpose(2, 0, 1, 3, 4)
    qb_s = qb_s.reshape(bs, DSA_KV_HEADS, DSA_REP * ts, DSA_HEAD_DIM)
    bias_s4 = bias_s[:, :, :ts, :].reshape(2, DSA_KV_HEADS, DSA_REP * ts, PAGE)
    bias_f4 = bias_f[:, :ts, :].reshape(DSA_KV_HEADS, DSA_REP * ts, PAGE)
    n_pool = cache_k.shape[1]
    page_t = lambda c: jnp.transpose(c[0], (0, 2, 3, 1)).reshape(n_pool, LANES, PAGE)
    o_s = _sample3(pt_flat, qb_s, sc_past, sc_new, tau_s.reshape(bs, ts, 1), cut_s.reshape(bs, ts, 1),
                   pad_new_t(kb_sbf), pad_new_t(vb_sbf), bias_s4, bias_f4,
                   page_t(cache_k), page_t(cache_v), n_pages, pp)
    dsa_s = o_s.reshape(bs, DSA_KV_HEADS, DSA_REP, ts, DSA_HEAD_DIM).transpose(0, 3, 1, 2, 4)
    dsa_s = dsa_s.reshape(ns, DSA_HEADS * DSA_HEAD_DIM).astype(BF16)
    y_s = _peer(xs, mla_s, dsa_s, wts)

    def rows(a_t, nb, nt):
        return jnp.swapaxes(a_t, 1, 2).reshape(1, nb, nt, a_t.shape[1])

    kv5 = lambda a: a.reshape(a.shape[:3] + (DSA_KV_HEADS, DSA_HEAD_DIM))
    return (y_p.reshape(b, s, d), y_s.reshape(bs, ts, d),
            ckv_p.reshape(1, b, s, KV_LORA), rows(kr_p, b, s), kv5(rows(kb_p, b, s)), kv5(rows(vb_p, b, s)),
            rows(ki_p, b, s),
            ckv_s.reshape(1, bs, ts, KV_LORA), rows(kr_s, bs, ts), kv5(rows(kb_s, bs, ts)), kv5(rows(vb_s, bs, ts)),
            rows(ki_s, bs, ts))
```

```python
import functools
import math

import jax
import jax.numpy as jnp
import numpy as np
from jax import lax
from jax.experimental import pallas as pl
from jax.experimental.pallas import tpu as pltpu

F32 = jnp.float32
BF16 = jnp.bfloat16
I32 = jnp.int32

D_MODEL = 1024
PAGE = 128
MLA_HEADS = 8
MLA_NOPE = 64
MLA_ROPE = 32
MLA_V = 64
Q_LORA = 384
KV_LORA = 256
MLA_SCALE = (MLA_NOPE + MLA_ROPE) ** -0.5
DSA_HEADS = 8
DSA_KV_HEADS = 2
DSA_REP = DSA_HEADS // DSA_KV_HEADS
DSA_HEAD_DIM = 64
DSA_SCALE = DSA_HEAD_DIM ** -0.5
IDX_HEADS = 8
IDX_DIM = 64
IDX_ROPE = 32
IDX_TOPK_MAX = 256
IDX_W_SCALE = (IDX_HEADS * IDX_DIM) ** -0.5
REL_BUCKETS = 32
REL_MAX_DIST = 128
PEER_HEADS = 8
PEER_NKEYS = 128
PEER_EXPERTS = PEER_NKEYS * PEER_NKEYS
PEER_HALF = 128
PEER_TOPK = 16
ROPE_THETA = 10000.0
NORM_EPS = 1e-6

LANES = 128
SUBLANES = 8
MXU_N = 256
NEG = -1e30
INT_MIN = -(2 ** 31)
KCAT = KV_LORA + LANES
KV_CHUNK = 256
MLA_CHUNK = 512
Q_TILE = 128

_C_CQ, _C_CKV, _C_QB, _C_KB, _C_VB = 0, 384, 640, 1152, 1280
_C_QI, _C_QIP, _C_KR, _C_KRP, _C_KI, _C_KIP, _C_WI, _C_END = 1408, 1920, 2432, 2560, 2688, 2816, 2944, 3072
_VMEM_LIMIT = 56 * 1024 * 1024


def _cparams(n_axes):
    return pltpu.CompilerParams(dimension_semantics=("arbitrary",) * n_axes, vmem_limit_bytes=_VMEM_LIMIT)


def _dot(a, b):
    return jnp.dot(a, b, preferred_element_type=F32)


def _dot_nt(a, b):
    return lax.dot_general(a, b, (((1,), (1,)), ((), ())), preferred_element_type=F32)


def _rms(x, g):
    return x * lax.rsqrt(jnp.mean(x * x, axis=-1, keepdims=True) + NORM_EPS) * g


def _sort_key(x):
    x = jnp.where(x == 0.0, 0.0, x)
    bits = pltpu.bitcast(x, I32)
    return bits ^ ((bits >> 31) & 0x7FFFFFFF)


def _bucket_starts():
    max_exact = REL_BUCKETS // 2
    n = np.arange(0, 2 * REL_MAX_DIST, dtype=np.int64)
    nf = np.maximum(n, 1).astype(np.float32)
    large = max_exact + (np.log(nf / np.float32(max_exact)) / np.float32(math.log(REL_MAX_DIST / max_exact))
                         * np.float32(REL_BUCKETS - max_exact)).astype(np.int32)
    large = np.minimum(large, REL_BUCKETS - 1)
    bucket = np.where(n < max_exact, n, large)
    starts = []
    for k in range(REL_BUCKETS):
        hit = np.nonzero(bucket >= k)[0]
        starts.append(int(hit[0]) if hit.size else int(n[-1]) + 1)
    return starts


_BUCKET_START = _bucket_starts()


def _bias_kernel(rb_ref, bp_ref, bs_ref, bf_ref):
    def bias_of(n, h):
        b = jnp.full(n.shape, rb_ref[REL_BUCKETS - 1, h], F32)
        for k in range(REL_BUCKETS - 2, -1, -1):
            b = jnp.where(n < _BUCKET_START[k + 1], rb_ref[k, h], b)
        return b

    s_i = lax.broadcasted_iota(I32, (KV_CHUNK, Q_TILE), 0)
    t_i = lax.broadcasted_iota(I32, (KV_CHUNK, Q_TILE), 1)
    for w in range(4):
        n = jnp.maximum(w * Q_TILE + t_i - s_i, 0)
        for h in range(DSA_HEADS):
            bp_ref[w, h] = bias_of(n, h)
    t_s = lax.broadcasted_iota(I32, (SUBLANES, PAGE), 0)
    u_s = lax.broadcasted_iota(I32, (SUBLANES, PAGE), 1)
    for w in range(2):
        n = jnp.maximum((1 - w) * PAGE + t_s - u_s, 0)
        for h in range(DSA_HEADS):
            bs_ref[w, h] = bias_of(n, h)
    for h in range(DSA_HEADS):
        bf_ref[h] = jnp.full((SUBLANES, PAGE), rb_ref[REL_BUCKETS - 1, h], F32)


def _bias_tables(rel_bias):
    return pl.pallas_call(
        _bias_kernel,
        out_shape=(jax.ShapeDtypeStruct((4, DSA_HEADS, KV_CHUNK, Q_TILE), F32),
                   jax.ShapeDtypeStruct((2, DSA_HEADS, SUBLANES, PAGE), F32),
                   jax.ShapeDtypeStruct((DSA_HEADS, SUBLANES, PAGE), F32)),
        in_specs=[pl.BlockSpec(memory_space=pltpu.SMEM)],
        name="bias_tables",
    )(rel_bias)


def _inproj_kernel(x_ref, ga_ref, w1_ref, gq_ref, wuq_ref, gkv_ref, wuk_ref, ca_ref, sa_ref, cb_ref, sb_ref,
                   ckv_ref, krope_ref, kb_ref, vb_ref, ki_ref,
                   kcat_ref, kbbf_ref, vbbf_ref, kibf_ref, qcat_ref, qb_ref, qi_ref, wi_ref):
    xn = _rms(x_ref[...], ga_ref[...]).astype(BF16)

    def proj(lo, hi):
        return _dot(xn, w1_ref[:, lo:hi])

    ca, sa, cb, sb = ca_ref[...], sa_ref[...], cb_ref[...], sb_ref[...]

    ckv = _rms(proj(_C_CKV, _C_QB), gkv_ref[...])
    ckv_ref[...] = ckv
    kcat_ref[:, 0:KV_LORA] = ckv.astype(BF16)
    kr = proj(_C_KR, _C_KRP) * ca + proj(_C_KRP, _C_KI) * sa
    krope_ref[...] = kr.T[:MLA_ROPE]
    kcat_ref[:, KV_LORA:KCAT] = kr.astype(BF16)
    ki = proj(_C_KI, _C_KIP) * ca + proj(_C_KIP, _C_WI) * sa
    ki_ref[...] = ki.T[:IDX_DIM]
    kibf_ref[...] = ki[:, :IDX_DIM].astype(BF16)
    kb = proj(_C_KB, _C_VB)
    kb_ref[...] = kb.T
    kbbf_ref[...] = kb.astype(BF16)
    vb = proj(_C_VB, _C_QI)
    vb_ref[...] = vb.T
    vbbf_ref[...] = vb.astype(BF16)
    wi_ref[...] = proj(_C_WI, _C_END)[:, :IDX_HEADS] * IDX_W_SCALE

    qb = proj(_C_QB, _C_KB)
    for h in range(DSA_HEADS):
        qb_ref[h] = qb[:, h * DSA_HEAD_DIM:(h + 1) * DSA_HEAD_DIM].astype(BF16)
    qi = proj(_C_QI, _C_QIP)
    qip = proj(_C_QIP, _C_KR)
    for s in range(4):
        slab = qi[:, s * LANES:(s + 1) * LANES] * cb + qip[:, s * LANES:(s + 1) * LANES] * sb
        qi_ref[2 * s] = slab[:, :IDX_DIM].astype(BF16)
        qi_ref[2 * s + 1] = slab[:, IDX_DIM:].astype(BF16)

    cq = _rms(proj(_C_CQ, _C_CKV), gq_ref[...]).astype(BF16)
    n_nope = MLA_HEADS * MLA_NOPE
    n_pad = MLA_HEADS * LANES
    nope = _dot(cq, wuq_ref[:, 0:n_nope]).astype(BF16)
    for p in range(MLA_HEADS // 2):
        ql = _dot(nope[:, p * LANES:(p + 1) * LANES], wuk_ref[p])
        qcat_ref[2 * p, :, 0:KV_LORA] = ql[:, :KV_LORA].astype(BF16)
        qcat_ref[2 * p + 1, :, 0:KV_LORA] = ql[:, KV_LORA:].astype(BF16)
    for h in range(MLA_HEADS):
        lo = n_nope + h * LANES
        qr = _dot(cq, wuq_ref[:, lo:lo + LANES]) * ca + _dot(cq, wuq_ref[:, lo + n_pad:lo + n_pad + LANES]) * sa
        qcat_ref[h, :, KV_LORA:KCAT] = qr.astype(BF16)


def _inproj(x2d, tabs, seq, wts, tm):
    n = x2d.shape[0]
    tab_blocks = seq // tm
    n_seq = n // seq
    const2 = lambda i: (0, 0)
    const3 = lambda i: (0, 0, 0)
    row = lambda i: (i, 0)
    tab = lambda i: (i % tab_blocks, 0)
    hm = lambda i: (0, i, 0)
    col = lambda i: (i // tab_blocks, 0, i % tab_blocks)
    t_shape = lambda width: jax.ShapeDtypeStruct((n_seq, width, seq), F32)
    t_spec = lambda width: pl.BlockSpec((None, width, tm), col)
    in_specs = [
        pl.BlockSpec((tm, D_MODEL), row),
        pl.BlockSpec((1, D_MODEL), const2),
        pl.BlockSpec((D_MODEL, _C_END), const2),
        pl.BlockSpec((1, Q_LORA), const2),
        pl.BlockSpec(wts["wuq"].shape, const2),
        pl.BlockSpec((1, KV_LORA), const2),
        pl.BlockSpec(wts["wuk"].shape, const3),
    ] + [pl.BlockSpec((tm, LANES), tab)] * 4
    out_shape = (
        jax.ShapeDtypeStruct((n, KV_LORA), F32), t_shape(MLA_ROPE), t_shape(LANES), t_shape(LANES), t_shape(IDX_DIM),
        jax.ShapeDtypeStruct((n, KCAT), BF16), jax.ShapeDtypeStruct((n, LANES), BF16),
        jax.ShapeDtypeStruct((n, LANES), BF16), jax.ShapeDtypeStruct((n, IDX_DIM), BF16),
        jax.ShapeDtypeStruct((MLA_HEADS, n, KCAT), BF16),
        jax.ShapeDtypeStruct((DSA_HEADS, n, DSA_HEAD_DIM), BF16),
        jax.ShapeDtypeStruct((IDX_HEADS, n, IDX_DIM), BF16),
        jax.ShapeDtypeStruct((n, IDX_HEADS), F32),
    )
    out_specs = (
        pl.BlockSpec((tm, KV_LORA), row), t_spec(MLA_ROPE), t_spec(LANES), t_spec(LANES), t_spec(IDX_DIM),
        pl.BlockSpec((tm, KCAT), row), pl.BlockSpec((tm, LANES), row),
        pl.BlockSpec((tm, LANES), row), pl.BlockSpec((tm, IDX_DIM), row),
        pl.BlockSpec((MLA_HEADS, tm, KCAT), hm),
        pl.BlockSpec((DSA_HEADS, tm, DSA_HEAD_DIM), hm),
        pl.BlockSpec((IDX_HEADS, tm, IDX_DIM), hm),
        pl.BlockSpec((tm, IDX_HEADS), row),
    )
    return pl.pallas_call(
        _inproj_kernel, grid=(n // tm,), in_specs=in_specs, out_specs=out_specs, out_shape=out_shape,
        compiler_params=_cparams(1), name="inproj",
    )(x2d, wts["g_attn"], wts["w1"], wts["g_q"], wts["wuq"], wts["g_kv"], wts["wuk"], *tabs)


def _rep(x, width):
    k = width // LANES
    return x if k == 1 else jnp.concatenate([x] * k, axis=1)


def _mla_prompt_kernel(q_ref, k_ref, wuv_ref, o_ref, m_scr, l_scr, acc_scr):
    j = pl.program_id(1)
    rows = MLA_HEADS * Q_TILE
    q = q_ref[...].reshape(rows, KCAT)
    m_scr[...] = jnp.full(m_scr.shape, NEG, F32)
    l_scr[...] = jnp.zeros(l_scr.shape, F32)
    acc_scr[...] = jnp.zeros(acc_scr.shape, F32)
    n_full = (j * Q_TILE) // MLA_CHUNK

    def body(c, masked):
        k = k_ref[pl.ds(pl.multiple_of(c * MLA_CHUNK, MLA_CHUNK), MLA_CHUNK), :]
        s = _dot_nt(q, k) * MLA_SCALE
        if masked:
            t_row = j * Q_TILE + lax.broadcasted_iota(I32, (rows, MLA_CHUNK), 0) % Q_TILE
            u_col = lax.broadcasted_iota(I32, (rows, MLA_CHUNK), 1)
            s = jnp.where(c * MLA_CHUNK + u_col <= t_row, s, NEG)
        m_prev = m_scr[...]
        m_new = jnp.maximum(m_prev, jnp.max(s, axis=1, keepdims=True))
        alpha = jnp.exp(m_prev - m_new)
        p = jnp.exp(s - _rep(m_new, MLA_CHUNK))
        l_scr[...] = alpha * l_scr[...] + jnp.sum(p, axis=1, keepdims=True)
        acc_scr[...] = acc_scr[...] * _rep(alpha, KV_LORA) + _dot(p.astype(BF16), k[:, :KV_LORA])
        m_scr[...] = m_new

    lax.fori_loop(0, n_full, lambda c, carry: (body(c, False), carry)[1], 0)
    body(n_full, True)
    o_lat = (acc_scr[...] / _rep(l_scr[...], KV_LORA)).astype(BF16)
    for h in range(MLA_HEADS):
        o = _dot(o_lat[h * Q_TILE:(h + 1) * Q_TILE], wuv_ref[h])
        o_ref[:, h * MLA_V:(h + 1) * MLA_V] = o.astype(BF16)


def _mla_prompt(qcat, kcat, wuv, b, s):
    nq = s // Q_TILE
    rows = MLA_HEADS * Q_TILE
    return pl.pallas_call(
        _mla_prompt_kernel, grid=(b, nq),
        in_specs=[pl.BlockSpec((MLA_HEADS, Q_TILE, KCAT), lambda bi, j: (0, bi * nq + j, 0)),
                  pl.BlockSpec((None, s, KCAT), lambda bi, j: (bi, 0, 0)),
                  pl.BlockSpec(wuv.shape, lambda bi, j: (0, 0, 0))],
        out_specs=pl.BlockSpec((Q_TILE, MLA_HEADS * MLA_V), lambda bi, j: (bi * nq + j, 0)),
        out_shape=jax.ShapeDtypeStruct((b * s, MLA_HEADS * MLA_V), BF16),
        scratch_shapes=[pltpu.VMEM((rows, LANES), F32), pltpu.VMEM((rows, LANES), F32),
                        pltpu.VMEM((rows, KV_LORA), F32)],
        compiler_params=_cparams(2), name="mla_prompt",
    )(qcat, kcat.reshape(b, s, KCAT), wuv)


def _topk_threshold(key_scr, n_chunks, topk, n_keys_pow2_bits):
    lanes = key_scr.shape[1]
    sub = KV_CHUNK // SUBLANES

    def count(pred_fn):
        def body(c, acc):
            off = pl.multiple_of(c * KV_CHUNK, KV_CHUNK)
            k = key_scr[pl.ds(off, KV_CHUNK), :]
            hit = pred_fn(k, c).astype(I32)
            return acc + jnp.sum(hit.reshape(sub, SUBLANES, lanes), axis=0)

        acc = lax.fori_loop(0, n_chunks, body, jnp.zeros((SUBLANES, lanes), I32))
        return jnp.sum(acc, axis=0, keepdims=True)

    def bit_body(i, res):
        cand = res | jnp.left_shift(jnp.int32(1), 31 - i)
        cs = cand ^ INT_MIN
        cnt = count(lambda k, c: k >= cs)
        return jnp.where(cnt >= topk, cand, res)

    res = lax.fori_loop(0, 32, bit_body, jnp.zeros((1, lanes), I32))
    tau = res ^ INT_MIN
    cnt_gt = count(lambda k, c: k > tau)
    cnt_eq = count(lambda k, c: k == tau)
    need = topk - cnt_gt
    row0 = lax.broadcasted_iota(I32, (KV_CHUNK, lanes), 0)
    big = jnp.int32(2 ** 30)

    def cut_search():
        def cbody(i, cur):
            cand = cur | jnp.left_shift(jnp.int32(1), n_keys_pow2_bits - 1 - i)
            f = count(lambda k, c: jnp.where(k == tau, row0 + c * KV_CHUNK, big) < cand)
            return jnp.where(f < need, cand, cur)

        return lax.fori_loop(0, n_keys_pow2_bits, cbody, jnp.zeros((1, lanes), I32))

    cut = lax.cond(jnp.max(cnt_eq - need) > 0, cut_search, lambda: jnp.full((1, lanes), big, I32))
    return tau, cut


def _dsa_prompt_kernel(qi_ref, wt_ref, ki_ref, qb_ref, kb_ref, vt_ref, bias_ref, o_ref,
                       key_scr, mb_scr, tc_scr, m_scr, l_scr, acc_scr, *, topk, idx_bits):
    j = pl.program_id(1)
    n_chunks = (j * Q_TILE) // KV_CHUNK + 1
    t_row = j * Q_TILE + lax.broadcasted_iota(I32, (KV_CHUNK, Q_TILE), 1)
    s_loc = lax.broadcasted_iota(I32, (KV_CHUNK, Q_TILE), 0)
    qi = qi_ref[...].reshape(IDX_HEADS * Q_TILE, IDX_DIM)
    wt = wt_ref[...]

    def score_body(c, carry):
        off = pl.multiple_of(c * KV_CHUNK, KV_CHUNK)
        a = _dot_nt(ki_ref[pl.ds(off, KV_CHUNK), :], qi)
        sc = jnp.zeros((KV_CHUNK, Q_TILE), F32)
        for h in range(IDX_HEADS):
            sc = sc + wt[h:h + 1, :] * jnp.maximum(a[:, h * Q_TILE:(h + 1) * Q_TILE], 0.0)
        key = jnp.where(off + s_loc <= t_row, _sort_key(sc), INT_MIN)
        key_scr[pl.ds(off, KV_CHUNK), :] = key
        return carry

    lax.fori_loop(0, n_chunks, score_body, 0)

    @pl.when((j + 1) * Q_TILE <= topk)
    def _():
        tc_scr[0:1, :] = jnp.full((1, Q_TILE), INT_MIN, I32)
        tc_scr[1:2, :] = jnp.full((1, Q_TILE), -1, I32)

    @pl.when((j + 1) * Q_TILE > topk)
    def _():
        tau, cut = _topk_threshold(key_scr, n_chunks, topk, idx_bits)
        tc_scr[0:1, :] = tau
        tc_scr[1:2, :] = cut

    tau = tc_scr[0:1, :]
    cut = tc_scr[1:2, :]

    def mask_body(c, carry):
        off = pl.multiple_of(c * KV_CHUNK, KV_CHUNK)
        k = key_scr[pl.ds(off, KV_CHUNK), :]
        spos = off + s_loc
        v = jnp.where(k > tau, 0.0, jnp.where(k == tau, jnp.where(spos <= cut, 0.0, NEG), NEG))
        mb_scr[pl.ds(off, KV_CHUNK), :] = jnp.where(spos <= t_row, v, NEG)
        return carry

    lax.fori_loop(0, n_chunks, mask_body, 0)

    m_scr[...] = jnp.full(m_scr.shape, NEG, F32)
    l_scr[...] = jnp.zeros(l_scr.shape, F32)
    acc_scr[...] = jnp.zeros(acc_scr.shape, F32)
    qb = qb_ref[...]

    def att_body(c, carry):
        off = pl.multiple_of(c * KV_CHUNK, KV_CHUNK)
        kb = kb_ref[pl.ds(off, KV_CHUNK), :]
        mb = mb_scr[pl.ds(off, KV_CHUNK), :]
        bidx = jnp.minimum((j * Q_TILE - c * KV_CHUNK) // Q_TILE, 3)
        for g in range(DSA_KV_HEADS):
            kg = kb[:, g * DSA_HEAD_DIM:(g + 1) * DSA_HEAD_DIM]
            qg = qb[g * DSA_REP:(g + 1) * DSA_REP].reshape(DSA_REP * Q_TILE, DSA_HEAD_DIM)
            lg4 = _dot_nt(kg, qg) * DSA_SCALE
            vg = vt_ref[c, g * DSA_HEAD_DIM:(g + 1) * DSA_HEAD_DIM, :]
            for r in range(DSA_REP):
                h = g * DSA_REP + r
                lg = lg4[:, r * Q_TILE:(r + 1) * Q_TILE] + bias_ref[bidx, h] + mb
                m_prev = m_scr[h:h + 1, :]
                m_new = jnp.maximum(m_prev, jnp.max(lg, axis=0, keepdims=True))
                alpha = jnp.exp(m_prev - m_new)
                p = jnp.exp(lg - m_new)
                l_scr[h:h + 1, :] = alpha * l_scr[h:h + 1, :] + jnp.sum(p, axis=0, keepdims=True)
                rs = slice(h * DSA_HEAD_DIM, (h + 1) * DSA_HEAD_DIM)
                acc_scr[rs, :] = alpha * acc_scr[rs, :] + _dot(vg, p.astype(BF16))
                m_scr[h:h + 1, :] = m_new
        return carry

    lax.fori_loop(0, n_chunks, att_body, 0)
    inv = 1.0 / l_scr[...]
    parts = [acc_scr[h * DSA_HEAD_DIM:(h + 1) * DSA_HEAD_DIM, :] * inv[h:h + 1, :] for h in range(DSA_HEADS)]
    o_ref[...] = jnp.concatenate(parts, axis=0).T.astype(BF16)


def _dsa_prompt(qi_hm, wi_t, ki_bf, qb_hm, kb_bf, v_t, bias_p, b, s):
    nq = s // Q_TILE
    topk = min(IDX_TOPK_MAX, s // 4)
    idx_bits = max(1, int(math.ceil(math.log2(s))))
    width = DSA_HEADS * DSA_HEAD_DIM
    kern = functools.partial(_dsa_prompt_kernel, topk=topk, idx_bits=idx_bits)
    return pl.pallas_call(
        kern, grid=(b, nq),
        in_specs=[pl.BlockSpec((IDX_HEADS, Q_TILE, IDX_DIM), lambda bi, j: (0, bi * nq + j, 0)),
                  pl.BlockSpec((IDX_HEADS, Q_TILE), lambda bi, j: (0, bi * nq + j)),
                  pl.BlockSpec((None, s, IDX_DIM), lambda bi, j: (bi, 0, 0)),
                  pl.BlockSpec((DSA_HEADS, Q_TILE, DSA_HEAD_DIM), lambda bi, j: (0, bi * nq + j, 0)),
                  pl.BlockSpec((None, s, LANES), lambda bi, j: (bi, 0, 0)),
                  pl.BlockSpec((None, s // KV_CHUNK, LANES, KV_CHUNK), lambda bi, j: (bi, 0, 0, 0)),
                  pl.BlockSpec(bias_p.shape, lambda bi, j: (0, 0, 0, 0))],
        out_specs=pl.BlockSpec((Q_TILE, width), lambda bi, j: (bi * nq + j, 0)),
        out_shape=jax.ShapeDtypeStruct((b * s, width), BF16),
        scratch_shapes=[pltpu.VMEM((s, Q_TILE), I32), pltpu.VMEM((s, Q_TILE), F32), pltpu.VMEM((SUBLANES, Q_TILE), I32),
                        pltpu.VMEM((DSA_HEADS, Q_TILE), F32), pltpu.VMEM((DSA_HEADS, Q_TILE), F32),
                        pltpu.VMEM((width, Q_TILE), F32)],
        compiler_params=_cparams(2), name="dsa_prompt",
    )(qi_hm, wi_t, ki_bf.reshape(b, s, IDX_DIM), qb_hm, kb_bf.reshape(b, s, LANES), v_t, bias_p)


def _mxu_tiles(pages, axis):
    group = MXU_N // PAGE
    return [pages[i] if len(pages[i:i + group]) == 1 else jnp.concatenate(pages[i:i + group], axis=axis)
            for i in range(0, len(pages), group)]


def _sample1_kernel(pt_ref, ql_ref, qr_ref, qi_ref, wi_ref, ckvn_ref, krn_ref, kin_ref, ckv_hbm, kr_hbm, ki_hbm,
                    olat_ref, sc_ref, scn_ref, ckv_buf, kr_buf, ki_buf, sem, m_scr, l_scr, acc_scr, *, pp, n_pages):
    slot = _page_fetch(pt_ref, (ckv_hbm, kr_hbm, ki_hbm), (ckv_buf, kr_buf, ki_buf), sem, pp, n_pages)
    ckv_pages = [ckv_buf.at[slot, k] for k in range(pp)]
    kr_pages = [kr_buf.at[slot, k] for k in range(pp)]
    ki_pages = [ki_buf.at[slot, k] for k in range(pp)]
    j = pl.program_id(1)
    last = pl.num_programs(1) - 1
    rows = ql_ref.shape[0]
    n_tok = rows // MLA_HEADS

    @pl.when(j == 0)
    def _():
        m_scr[...] = jnp.full(m_scr.shape, NEG, F32)
        l_scr[...] = jnp.zeros(l_scr.shape, F32)
        acc_scr[...] = jnp.zeros(acc_scr.shape, F32)

    ql, qr, qi, wi = ql_ref[...], qr_ref[...], qi_ref[...], wi_ref[...]

    def attend(kcs, krs, mask):
        kcs, krs = _mxu_tiles(kcs, axis=0), _mxu_tiles(krs, axis=1)
        s = jnp.concatenate([_dot_nt(ql, kc) + _dot(qr, kr) for kc, kr in zip(kcs, krs)], axis=1) * MLA_SCALE
        if mask is not None:
            s = jnp.where(mask, s, NEG)
        m_prev = m_scr[...]
        m_new = jnp.maximum(m_prev, jnp.max(s, axis=1, keepdims=True))
        alpha = jnp.exp(m_prev - m_new)
        p = jnp.exp(s - _rep(m_new, s.shape[1])).astype(BF16)
        l_scr[...] = alpha * l_scr[...] + jnp.sum(p.astype(F32), axis=1, keepdims=True)
        pv, off = None, 0
        for kc in kcs:
            d = _dot(p[:, off:off + kc.shape[0]], kc)
            pv = d if pv is None else pv + d
            off += kc.shape[0]
        acc_scr[...] = acc_scr[...] * _rep(alpha, KV_LORA) + pv
        m_scr[...] = m_new

    def index(kidx_t):
        a = jnp.maximum(_dot(qi, kidx_t), 0.0) * wi
        return jnp.sum(a.reshape(n_tok, IDX_HEADS, kidx_t.shape[1]), axis=1)

    attend([r[...].astype(BF16) for r in ckv_pages], [r[...].astype(BF16) for r in kr_pages], None)
    off = 0
    for kt in _mxu_tiles([r[...].astype(BF16) for r in ki_pages], axis=1):
        sc_ref[:, off:off + kt.shape[1]] = index(kt)
        off += kt.shape[1]

    @pl.when(j == last)
    def _():
        t_r = lax.broadcasted_iota(I32, (rows, PAGE), 0) % n_tok
        u_c = lax.broadcasted_iota(I32, (rows, PAGE), 1)
        attend([ckvn_ref[...]], [krn_ref[...]], u_c <= t_r)
        t4 = lax.broadcasted_iota(I32, (n_tok, PAGE), 0)
        u4 = lax.broadcasted_iota(I32, (n_tok, PAGE), 1)
        scn_ref[...] = jnp.where(u4 <= t4, index(kin_ref[...]), -jnp.inf)
        olat_ref[...] = acc_scr[...] / _rep(l_scr[...], KV_LORA)


def _sample1(pt_flat, ql, qr, qi, wi, ckvn, krn_t, kin_t, c_ckv, c_kr_t, c_ki_t, n_pages, pp):
    bs, rows = ql.shape[0], ql.shape[1]
    n_tok = rows // MLA_HEADS
    per_b = lambda tail: pl.BlockSpec((None,) + tail, lambda bi, j, pt: (bi,) + (0,) * len(tail))
    in_specs = [per_b((rows, KV_LORA)), per_b((rows, MLA_ROPE)), per_b((rows, IDX_DIM)), per_b((rows, 1)),
                per_b((PAGE, KV_LORA)), per_b((MLA_ROPE, PAGE)), per_b((IDX_DIM, PAGE))]
    in_specs += [pl.BlockSpec(memory_space=pl.ANY)] * 3
    out_specs = (per_b((rows, KV_LORA)),
                 pl.BlockSpec((None, n_tok, pp * PAGE), lambda bi, j, pt: (bi, 0, j)),
                 per_b((n_tok, PAGE)))
    out_shape = (jax.ShapeDtypeStruct((bs, rows, KV_LORA), F32),
                 jax.ShapeDtypeStruct((bs, n_tok, n_pages * PAGE), F32),
                 jax.ShapeDtypeStruct((bs, n_tok, PAGE), F32))
    grid_spec = pltpu.PrefetchScalarGridSpec(
        num_scalar_prefetch=1, grid=(bs, n_pages // pp), in_specs=in_specs, out_specs=out_specs,
        scratch_shapes=[pltpu.VMEM((2, pp, PAGE, KV_LORA), F32), pltpu.VMEM((2, pp, MLA_ROPE, PAGE), F32),
                        pltpu.VMEM((2, pp, IDX_DIM, PAGE), F32), pltpu.SemaphoreType.DMA((2, 3)),
                        pltpu.VMEM((rows, LANES), F32), pltpu.VMEM((rows, LANES), F32),
                        pltpu.VMEM((rows, KV_LORA), F32)])
    return pl.pallas_call(
        functools.partial(_sample1_kernel, pp=pp, n_pages=n_pages), grid_spec=grid_spec, out_shape=out_shape,
        compiler_params=_cparams(2), name="sample_mla_index",
    )(pt_flat, ql, qr, qi, wi, ckvn, krn_t, kin_t, c_ckv, c_kr_t, c_ki_t)


def _mla_out_kernel(o_ref, wuv_ref, out_ref):
    for h in range(MLA_HEADS):
        out_ref[:, h * MLA_V:(h + 1) * MLA_V] = _dot(o_ref[h], wuv_ref[h]).astype(BF16)


def _mla_out(olat_hm, wuv):
    n = olat_hm.shape[1]
    return pl.pallas_call(
        _mla_out_kernel, out_shape=jax.ShapeDtypeStruct((n, MLA_HEADS * MLA_V), BF16), name="sample_mla_out",
    )(olat_hm, wuv)


def _sample_select_kernel(sc_ref, tau_ref, cut_ref, key_scr, *, topk, idx_bits):
    n_chunks = sc_ref.shape[0] // KV_CHUNK

    def kbody(c, carry):
        off = pl.multiple_of(c * KV_CHUNK, KV_CHUNK)
        key_scr[pl.ds(off, KV_CHUNK), :] = _sort_key(sc_ref[pl.ds(off, KV_CHUNK), :])
        return carry

    lax.fori_loop(0, n_chunks, kbody, 0)
    tau, cut = _topk_threshold(key_scr, n_chunks, topk, idx_bits)
    tau_ref[...] = tau
    cut_ref[...] = cut


def _sample_select(sc_t, topk):
    kp, ns = sc_t.shape
    lt = min(LANES, ns)
    idx_bits = max(1, int(math.ceil(math.log2(kp))))
    kern = functools.partial(_sample_select_kernel, topk=topk, idx_bits=idx_bits)
    return pl.pallas_call(
        kern, grid=(ns // lt,),
        in_specs=[pl.BlockSpec((kp, lt), lambda i: (0, i))],
        out_specs=(pl.BlockSpec((1, lt), lambda i: (0, i)), pl.BlockSpec((1, lt), lambda i: (0, i))),
        out_shape=(jax.ShapeDtypeStruct((1, ns), I32), jax.ShapeDtypeStruct((1, ns), I32)),
        scratch_shapes=[pltpu.VMEM((kp, lt), I32)],
        compiler_params=_cparams(1), name="sample_select",
    )(sc_t)


def _page_fetch(pt_ref, caches, bufs, sem, pp, n_pages):
    bi, j, nj = pl.program_id(0), pl.program_id(1), pl.num_programs(1)
    step = bi * nj + j
    slot = step % 2

    def copies(first_page, sl):
        out = []
        for k in range(pp):
            page = 0 if first_page is None else pt_ref[first_page + k]
            for i, (cache, buf) in enumerate(zip(caches, bufs)):
                out.append(pltpu.make_async_copy(cache.at[page], buf.at[sl, k], sem.at[sl, i]))
        return out

    @pl.when(step == 0)
    def _():
        for c in copies(0, 0):
            c.start()

    @pl.when(step + 1 < pl.num_programs(0) * nj)
    def _():
        nxt = step + 1
        for c in copies((nxt // nj) * n_pages + (nxt % nj) * pp, 1 - slot):
            c.start()

    for c in copies(None, slot):
        c.wait()
    return slot


def _sample3_kernel(pt_ref, qb_ref, sc_ref, scn_ref, tau_ref, cut_ref, kn_ref, vn_ref, bs_ref, bf_ref, ck_hbm, cv_hbm,
                    o_ref, kbuf, vbuf, sem, m_scr, l_scr, acc_scr, *, pp, past):
    slot = _page_fetch(pt_ref, (ck_hbm, cv_hbm), (kbuf, vbuf), sem, pp, past // PAGE)
    k_pages = [kbuf.at[slot, k] for k in range(pp)]
    v_pages = [vbuf.at[slot, k] for k in range(pp)]
    j = pl.program_id(1)
    last = pl.num_programs(1) - 1
    n_tok = sc_ref.shape[0]

    @pl.when(j == 0)
    def _():
        m_scr[...] = jnp.full(m_scr.shape, NEG, F32)
        l_scr[...] = jnp.zeros(l_scr.shape, F32)
        acc_scr[...] = jnp.zeros(acc_scr.shape, F32)

    tau, cut = tau_ref[...], cut_ref[...]

    def mask_bias(sc, base):
        k = _sort_key(sc)
        spos = base + lax.broadcasted_iota(I32, sc.shape, 1)
        return jnp.where(k > tau, 0.0, jnp.where(k == tau, jnp.where(spos <= cut, 0.0, NEG), NEG))

    def attend(kts, vts, mb4, biases):
        mb = jnp.concatenate([mb4] * DSA_REP, axis=0)
        kts, vts = _mxu_tiles(kts, axis=1), _mxu_tiles(vts, axis=1)
        for g in range(DSA_KV_HEADS):
            rs = slice(g * DSA_HEAD_DIM, (g + 1) * DSA_HEAD_DIM)
            lg = jnp.concatenate([_dot(qb_ref[g], kt[rs, :]) for kt in kts], axis=1) * DSA_SCALE
            lg = lg + jnp.concatenate([b[g] for b in biases], axis=1) + mb
            m_prev = m_scr[g]
            m_new = jnp.maximum(m_prev, jnp.max(lg, axis=1, keepdims=True))
            alpha = jnp.exp(m_prev - m_new)
            p = jnp.exp(lg - _rep(m_new, lg.shape[1])).astype(BF16)
            l_scr[g] = alpha * l_scr[g] + jnp.sum(p.astype(F32), axis=1, keepdims=True)
            pv, off = None, 0
            for vt in vts:
                d = _dot_nt(p[:, off:off + vt.shape[1]], vt[rs, :])
                pv = d if pv is None else pv + d
                off += vt.shape[1]
            acc_scr[g] = acc_scr[g] * alpha[:, :DSA_HEAD_DIM] + pv
            m_scr[g] = m_new

    far = bf_ref[...]
    biases = [far] * (pp - 1) + [jnp.where(j == last, bs_ref[0], far)]
    attend([r[...].astype(BF16) for r in k_pages], [r[...].astype(BF16) for r in v_pages],
           mask_bias(sc_ref[...], j * (pp * PAGE)), biases)

    @pl.when(j == last)
    def _():
        attend([kn_ref[...]], [vn_ref[...]], mask_bias(scn_ref[...], past), [bs_ref[1]])
        for g in range(DSA_KV_HEADS):
            o_ref[g] = acc_scr[g] / l_scr[g][:, :DSA_HEAD_DIM]


def _sample3(pt_flat, qb, sc, scn, tau, cut, kn_t, vn_t, bias_s, bias_f, c_kt, c_vt, n_pages, pp):
    bs, n_tok = sc.shape[0], sc.shape[1]
    rows = DSA_REP * n_tok
    per_b = lambda tail: pl.BlockSpec((None,) + tail, lambda bi, j, pt: (bi,) + (0,) * len(tail))
    const = lambda shape: pl.BlockSpec(shape, lambda bi, j, pt: (0,) * len(shape))
    in_specs = [per_b((DSA_KV_HEADS, rows, DSA_HEAD_DIM)),
                pl.BlockSpec((None, n_tok, pp * PAGE), lambda bi, j, pt: (bi, 0, j)),
                per_b((n_tok, PAGE)), per_b((n_tok, 1)), per_b((n_tok, 1)),
                per_b((LANES, PAGE)), per_b((LANES, PAGE)),
                const(bias_s.shape), const(bias_f.shape),
                pl.BlockSpec(memory_space=pl.ANY), pl.BlockSpec(memory_space=pl.ANY)]
    grid_spec = pltpu.PrefetchScalarGridSpec(
        num_scalar_prefetch=1, grid=(bs, n_pages // pp), in_specs=in_specs,
        out_specs=per_b((DSA_KV_HEADS, rows, DSA_HEAD_DIM)),
        scratch_shapes=[pltpu.VMEM((2, pp, LANES, PAGE), F32), pltpu.VMEM((2, pp, LANES, PAGE), F32),
                        pltpu.SemaphoreType.DMA((2, 2)),
                        pltpu.VMEM((DSA_KV_HEADS, rows, LANES), F32), pltpu.VMEM((DSA_KV_HEADS, rows, LANES), F32),
                        pltpu.VMEM((DSA_KV_HEADS, rows, DSA_HEAD_DIM), F32)])
    kern = functools.partial(_sample3_kernel, pp=pp, past=n_pages * PAGE)
    return pl.pallas_call(
        kern, grid_spec=grid_spec,
        out_shape=jax.ShapeDtypeStruct((bs, DSA_KV_HEADS, rows, DSA_HEAD_DIM), F32),
        compiler_params=_cparams(2), name="sample_dsa",
    )(pt_flat, qb, sc, scn, tau, cut, kn_t, vn_t, bias_s, bias_f, c_kt, c_vt)


_N_EXTRACT = PEER_TOPK + 1


def _extract_top(cur, n):
    vals = []
    for _ in range(n):
        m = jnp.max(cur, axis=0, keepdims=True)
        vals.append(m)
        cur = jnp.where(cur == m, -jnp.inf, cur)
    return vals


def _peer_prep_kernel(x_ref, mla_ref, dsa_ref, wo_ref, g_ref, wpq_ref, keys_ref,
                      h_ref, xnt_ref, thr_ref, a_ref, s2_ref, b_ref):
    half = wo_ref.shape[0] // 2
    h = x_ref[...] + _dot(mla_ref[...], wo_ref[0:half, :]) + _dot(dsa_ref[...], wo_ref[half:, :])
    h_ref[...] = h
    xnt = _rms(h, g_ref[...]).T.astype(BF16)
    xnt_ref[...] = xnt
    tc = xnt.shape[1]
    r8 = lax.broadcasted_iota(I32, (SUBLANES, LANES), 0)
    for hh in range(PEER_HEADS):
        for p, ref in ((0, thr_ref), (1, s2_ref)):
            hp = hh * 2 + p
            qt = _dot(wpq_ref[hp * PEER_HALF:(hp + 1) * PEER_HALF, :], xnt)
            ref[hh] = _dot(keys_ref[hp], qt.astype(BF16))
        for lt in range(tc // LANES):
            ls = slice(lt * LANES, (lt + 1) * LANES)
            s1, s2 = thr_ref[hh, :, ls], s2_ref[hh, :, ls]
            sv1 = _extract_top(s1, _N_EXTRACT)
            sv2 = _extract_top(s2, _N_EXTRACT)
            sv2_16 = jnp.concatenate(sv2[:PEER_TOPK], axis=0)
            sv2_8 = sv2_16[:8]
            blocks = [sv1[0] + sv2_16]
            for r1 in range(1, 8):
                blocks.append(jnp.where(r8 < PEER_TOPK // (r1 + 1), sv1[r1] + sv2_8, -jnp.inf))
            blocks.append(jnp.concatenate(sv1[8:PEER_TOPK], axis=0) + sv2[0])
            extra = jnp.where(r8 == 0, sv1[0] + sv2[PEER_TOPK],
                              jnp.where(r8 == 1, sv1[PEER_TOPK] + sv2[0], -jnp.inf))
            blocks.append(extra)
            cand = _extract_top(jnp.concatenate(blocks, axis=0), _N_EXTRACT)
            m0 = sv1[0] + sv2[0]
            z = jnp.zeros_like(m0)
            for r in range(PEER_TOPK):
                z = z + jnp.exp(cand[r] - m0)
            c16, c17 = cand[PEER_TOPK - 1], cand[PEER_TOPK]
            tau = jnp.where(c17 == -jnp.inf, c16, 0.5 * (c16 + c17))
            thr_ref[hh, :, ls] = tau - s1
            a_ref[hh, :, ls] = jnp.exp(s1 - sv1[0]) / z * 0.5
            b_ref[hh, :, ls] = jnp.exp(s2 - sv2[0])


def _peer_prep(x2d, mla, dsa, wts, tc):
    n = x2d.shape[0]
    row = lambda i: (i, 0)
    const2 = lambda i: (0, 0)
    col3 = lambda i: (0, 0, i)
    gate_shape = jax.ShapeDtypeStruct((PEER_HEADS, PEER_NKEYS, n), F32)
    gate_spec = pl.BlockSpec((PEER_HEADS, PEER_NKEYS, tc), col3)
    mix = mla.shape[1]
    return pl.pallas_call(
        _peer_prep_kernel, grid=(n // tc,),
        in_specs=[pl.BlockSpec((tc, D_MODEL), row), pl.BlockSpec((tc, mix), row), pl.BlockSpec((tc, mix), row),
                  pl.BlockSpec(wts["w_out"].shape, const2), pl.BlockSpec((1, D_MODEL), const2),
                  pl.BlockSpec(wts["wpq_t"].shape, const2), pl.BlockSpec(wts["peer_keys"].shape, lambda i: (0, 0, 0))],
        out_specs=(pl.BlockSpec((tc, D_MODEL), row), pl.BlockSpec((D_MODEL, tc), lambda i: (0, i)),
                   gate_spec, gate_spec, gate_spec, gate_spec),
        out_shape=(jax.ShapeDtypeStruct((n, D_MODEL), F32), jax.ShapeDtypeStruct((D_MODEL, n), BF16),
                   gate_shape, gate_shape, gate_shape, gate_shape),
        compiler_params=_cparams(1), name="peer_prep",
    )(x2d, mla, dsa, wts["w_out"], wts["g_ffn"], wts["wpq_t"], wts["peer_keys"])


def _gelu_x2(x):
    return x * (1.0 + lax.erf(x * np.float32(math.sqrt(0.5))))


def _peer_chain_kernel(xnt_ref, thr_ref, a_ref, s2_ref, b_ref, u_ref, vt_ref, h_ref, gf_ref, y_ref, acc_scr, *, ni):
    e = pl.program_id(1)

    @pl.when(e == 0)
    def _():
        acc_scr[...] = jnp.zeros(acc_scr.shape, F32)

    tc = acc_scr.shape[1]
    tw = min(tc, MXU_N)
    per_slice = MXU_N // PEER_NKEYS
    chains = [(ks, k) for ks in range(ni // per_slice) for k in range(tc // tw)]
    n_r = D_MODEL // MXU_N

    def gates(ks, k):
        out = []
        for i2 in range(per_slice):
            i1 = e * ni + ks * per_slice + i2
            thr_rows = [thr_ref[hh, pl.ds(i1, 1), :] for hh in range(PEER_HEADS)]
            a_rows = [a_ref[hh, pl.ds(i1, 1), :] for hh in range(PEER_HEADS)]
            for lt in range(tw // LANES):
                ls = slice(k * tw + lt * LANES, k * tw + (lt + 1) * LANES)
                gate = None
                for hh in range(PEER_HEADS):
                    term = jnp.where(s2_ref[hh, :, ls] >= thr_rows[hh][:, ls], b_ref[hh, :, ls], 0.0)
                    term = term * a_rows[hh][:, ls]
                    gate = term if gate is None else gate + term
                out.append(gate)
        return out

    for ks, k in chains:
        es = slice(ks * MXU_N, (ks + 1) * MXU_N)
        cs = slice(k * tw, (k + 1) * tw)
        act = _gelu_x2(_dot(u_ref[es, :], xnt_ref[:, cs]))
        g_cur = gates(ks, k)
        n_lt = tw // LANES
        parts = []
        for i2 in range(per_slice):
            tiles = [(g_cur[i2 * n_lt + lt] * act[i2 * PEER_NKEYS:(i2 + 1) * PEER_NKEYS,
                                                   lt * LANES:(lt + 1) * LANES]).astype(BF16) for lt in range(n_lt)]
            parts.append(tiles[0] if n_lt == 1 else jnp.concatenate(tiles, axis=1))
        w = jnp.concatenate(parts, axis=0)
        for r in range(n_r):
            rs = slice(r * MXU_N, (r + 1) * MXU_N)
            acc_scr[rs, cs] += _dot(vt_ref[rs, es], w)

    @pl.when(e == pl.num_programs(1) - 1)
    def _():
        y_ref[...] = _rms(acc_scr[...].T + h_ref[...], gf_ref[...])


def _peer_chain(xnt, thr, a, s2, b, h, wts, tc, ni):
    n = h.shape[0]
    eb = ni * PEER_NKEYS
    gate_spec = pl.BlockSpec((PEER_HEADS, PEER_NKEYS, tc), lambda i, e: (0, 0, i))
    return pl.pallas_call(
        functools.partial(_peer_chain_kernel, ni=ni), grid=(n // tc, PEER_EXPERTS // eb),
        in_specs=[pl.BlockSpec((D_MODEL, tc), lambda i, e: (0, i)), gate_spec, gate_spec, gate_spec, gate_spec,
                  pl.BlockSpec((eb, D_MODEL), lambda i, e: (e, 0)), pl.BlockSpec((D_MODEL, eb), lambda i, e: (0, e)),
                  pl.BlockSpec((tc, D_MODEL), lambda i, e: (i, 0)), pl.BlockSpec((1, D_MODEL), lambda i, e: (0, 0))],
        out_specs=pl.BlockSpec((tc, D_MODEL), lambda i, e: (i, 0)),
        out_shape=jax.ShapeDtypeStruct((n, D_MODEL), F32),
        scratch_shapes=[pltpu.VMEM((D_MODEL, tc), F32)],
        compiler_params=_cparams(2), name="peer_main",
    )(xnt, thr, a, s2, b, wts["peer_u"], wts["peer_vt"], h, wts["g_final"])


def _peer(x2d, mla, dsa, wts):
    n = x2d.shape[0]
    tc = min(512, n)
    h, xnt, thr, a, s2, b = _peer_prep(x2d, mla, dsa, wts, tc)
    return _peer_chain(xnt, thr, a, s2, b, h, wts, tc, ni=16)


def _pad_cols(w, n):
    return jnp.pad(w, ((0, 0), (0, n - w.shape[1])))


def _swap_halves(w):
    half = w.shape[-1] // 2
    return jnp.concatenate([w[..., half:], w[..., :half]], axis=-1)


def _prep_weights(g_attn, w_in, g_q, w_uq, g_kv, w_uk, w_uv, w_out, g_ffn, w_pq, peer_keys, peer_u, peer_v, g_final):
    w_cq, w_ckv, w_kr = w_in[:, 0:384], w_in[:, 384:640], w_in[:, 640:672]
    w_qb, w_kb, w_vb = w_in[:, 672:1184], w_in[:, 1184:1312], w_in[:, 1312:1440]
    w_qi, w_ki, w_wi = w_in[:, 1440:1952], w_in[:, 1952:2016], w_in[:, 2016:2024]
    qi3 = w_qi.reshape(D_MODEL, IDX_HEADS, IDX_DIM)
    qi_partner = jnp.concatenate([_swap_halves(qi3[..., :IDX_ROPE]), jnp.zeros_like(qi3[..., IDX_ROPE:])], axis=-1)
    w1 = jnp.concatenate([
        w_cq, w_ckv, w_qb, w_kb, w_vb, w_qi, qi_partner.reshape(D_MODEL, IDX_HEADS * IDX_DIM),
        _pad_cols(w_kr, LANES), _pad_cols(_swap_halves(w_kr), LANES),
        _pad_cols(w_ki, LANES), _pad_cols(_swap_halves(w_ki[:, :IDX_ROPE]), LANES),
        _pad_cols(w_wi, LANES)], axis=1).astype(BF16)
    uq3 = w_uq.reshape(Q_LORA, MLA_HEADS, MLA_NOPE + MLA_ROPE)
    rope3 = uq3[..., MLA_NOPE:]
    pad3 = lambda w: jnp.pad(w, ((0, 0), (0, 0), (0, LANES - MLA_ROPE))).reshape(Q_LORA, MLA_HEADS * LANES)
    wuq = jnp.concatenate([uq3[..., :MLA_NOPE].reshape(Q_LORA, MLA_HEADS * MLA_NOPE),
                           pad3(rope3), pad3(_swap_halves(rope3))], axis=1).astype(BF16)
    ukt = jnp.transpose(w_uk, (1, 2, 0))
    zero = jnp.zeros((MLA_NOPE, KV_LORA), F32)
    wuk = jnp.stack([jnp.concatenate([jnp.concatenate([ukt[2 * p], zero], axis=1),
                                      jnp.concatenate([zero, ukt[2 * p + 1]], axis=1)], axis=0)
                     for p in range(MLA_HEADS // 2)]).astype(BF16)
    return dict(
        g_attn=g_attn.reshape(1, -1), w1=w1, g_q=g_q.reshape(1, -1), wuq=wuq, g_kv=g_kv.reshape(1, -1), wuk=wuk,
        wuv=jnp.transpose(w_uv, (1, 0, 2)).astype(BF16), w_out=w_out.astype(BF16), g_ffn=g_ffn.reshape(1, -1),
        wpq_t=w_pq.T.astype(BF16),
        peer_keys=peer_keys.reshape(PEER_HEADS * 2, PEER_NKEYS, PEER_HALF).astype(BF16),
        peer_u=peer_u.astype(BF16), peer_vt=peer_v.T.astype(BF16), g_final=g_final.reshape(1, -1))


def _rope_tables(pos):
    half = MLA_ROPE // 2
    inv = ROPE_THETA ** (-jnp.arange(half, dtype=F32) / half)
    ang = pos.astype(F32)[:, None] * inv
    cos, sin = jnp.cos(ang), jnp.sin(ang)
    c32 = jnp.concatenate([cos, cos], axis=1)
    s32 = jnp.concatenate([-sin, sin], axis=1)
    n = pos.shape[0]
    one, zero = jnp.ones((n, 32), F32), jnp.zeros((n, 32), F32)
    ca = jnp.concatenate([c32, one, one, one], axis=1)
    sa = jnp.concatenate([s32, zero, zero, zero], axis=1)
    cb = jnp.concatenate([c32, one, c32, one], axis=1)
    sb = jnp.concatenate([s32, zero, s32, zero], axis=1)
    return ca, sa, cb, sb


def _pick_tile(n, choices):
    for c in choices:
        if n % c == 0:
            return c
    raise ValueError(f"no tile in {choices} divides {n}")


def kernel(x_prompt, x_sample, cache_ckv, cache_krope, cache_k, cache_v, cache_kidx, page_table, rel_bias, g_attn,
           w_in, g_q, w_uq, g_kv, w_uk, w_uv, w_out, g_ffn, w_pq, peer_keys, peer_u, peer_v, g_final):
    assert g_attn.shape[0] == 1, "single-layer kernel"
    b, s, d = x_prompt.shape
    bs, ts, _ = x_sample.shape
    n_pages = page_table.shape[1]
    past = n_pages * PAGE
    assert s % MLA_CHUNK == 0 and s % KV_CHUNK == 0 and ts <= 8 and (bs * ts) % LANES == 0
    wts = _prep_weights(g_attn[0], w_in[0], g_q[0], w_uq[0], g_kv[0], w_uk[0], w_uv[0], w_out[0], g_ffn[0],
                        w_pq[0], peer_keys[0], peer_u[0], peer_v[0], g_final)
    bias_p, bias_s, bias_f = _bias_tables(rel_bias)

    xp = x_prompt.reshape(b * s, d)
    tm = _pick_tile(s, (512, 256))
    (ckv_p, kr_p, kb_p, vb_p, ki_p, kcat, kb_bf, vb_bf, ki_bf, qcat, qb_hm, qi_hm, wi_p) = _inproj(
        xp, _rope_tables(jnp.arange(s)), s, wts, tm)
    mla_p = _mla_prompt(qcat, kcat, wts["wuv"], b, s)
    v_t = jnp.transpose(vb_bf.reshape(b, s // KV_CHUNK, KV_CHUNK, LANES), (0, 1, 3, 2))
    dsa_p = _dsa_prompt(qi_hm, wi_p.T, ki_bf, qb_hm, kb_bf, v_t, bias_p, b, s)
    y_p = _peer(xp, mla_p, dsa_p, wts)

    ns = bs * ts
    xs = x_sample.reshape(ns, d)
    pos_s = past + jnp.tile(jnp.arange(ts), bs)
    (ckv_s, kr_s, kb_s, vb_s, ki_s, kcat_s, kb_sbf, vb_sbf, ki_sbf, qcat_s, qb_shm, qi_shm, wi_s) = _inproj(
        xs, _rope_tables(pos_s), ns, wts, ns)
    pt_flat = page_table.reshape(-1).astype(I32)
    pp = _pick_tile(n_pages, (16, 8, 4, 2, 1))
    q5 = qcat_s.reshape(MLA_HEADS, bs, ts, KCAT).transpose(1, 0, 2, 3).reshape(bs, MLA_HEADS * ts, KCAT)
    qi_s = qi_shm.reshape(IDX_HEADS, bs, ts, IDX_DIM).transpose(1, 2, 0, 3).reshape(bs, ts * IDX_HEADS, IDX_DIM)
    wi_col = wi_s.reshape(bs, ts * IDX_HEADS, 1)
    pad_new = lambda a: jnp.pad(a.reshape(bs, ts, a.shape[-1]), ((0, 0), (0, PAGE - ts), (0, 0)))
    pad_new_t = lambda a: jnp.swapaxes(pad_new(a), 1, 2)
    olat, sc_past, sc_new = _sample1(
        pt_flat, q5[..., :KV_LORA], q5[..., KV_LORA:KV_LORA + MLA_ROPE], qi_s, wi_col,
        pad_new(kcat_s[:, :KV_LORA]), pad_new_t(kcat_s[:, KV_LORA:KV_LORA + MLA_ROPE]), pad_new_t(ki_sbf),
        cache_ckv[0], jnp.swapaxes(cache_krope[0], 1, 2), jnp.swapaxes(cache_kidx[0], 1, 2), n_pages, pp)
    olat_hm = olat.reshape(bs, MLA_HEADS, ts, KV_LORA).transpose(1, 0, 2, 3).reshape(MLA_HEADS, ns, KV_LORA)
    mla_s = _mla_out(olat_hm.astype(BF16), wts["wuv"])
    topk_s = min(IDX_TOPK_MAX, (past + ts) // 4)
    kp = -(-(past + PAGE) // KV_CHUNK) * KV_CHUNK
    sc_all = jnp.concatenate([sc_past, sc_new], axis=2).reshape(ns, past + PAGE)
    sc_t = jnp.pad(sc_all, ((0, 0), (0, kp - past - PAGE)), constant_values=-jnp.inf).T
    tau_s, cut_s = _sample_select(sc_t, topk_s)
    qb_s = qb_shm.reshape(DSA_KV_HEADS, DSA_REP, bs, ts, DSA_HEAD_DIM).transpose(2, 0, 1, 3, 4)
    qb_s = qb_s.reshape(bs, DSA_KV_HEADS, DSA_REP * ts, DSA_HEAD_DIM)
    bias_s4 = bias_s[:, :, :ts, :].reshape(2, DSA_KV_HEADS, DSA_REP * ts, PAGE)
    bias_f4 = bias_f[:, :ts, :].reshape(DSA_KV_HEADS, DSA_REP * ts, PAGE)
    n_pool = cache_k.shape[1]
    page_t = lambda c: jnp.transpose(c[0], (0, 2, 3, 1)).reshape(n_pool, LANES, PAGE)
    o_s = _sample3(pt_flat, qb_s, sc_past, sc_new, tau_s.reshape(bs, ts, 1), cut_s.reshape(bs, ts, 1),
                   pad_new_t(kb_sbf), pad_new_t(vb_sbf), bias_s4, bias_f4,
                   page_t(cache_k), page_t(cache_v), n_pages, pp)
    dsa_s = o_s.reshape(bs, DSA_KV_HEADS, DSA_REP, ts, DSA_HEAD_DIM).transpose(0, 3, 1, 2, 4)
    dsa_s = dsa_s.reshape(ns, DSA_HEADS * DSA_HEAD_DIM).astype(BF16)
    y_s = _peer(xs, mla_s, dsa_s, wts)

    def rows(a_t, nb, nt):
        return jnp.swapaxes(a_t, 1, 2).reshape(1, nb, nt, a_t.shape[1])

    kv5 = lambda a: a.reshape(a.shape[:3] + (DSA_KV_HEADS, DSA_HEAD_DIM))
    return (y_p.reshape(b, s, d), y_s.reshape(bs, ts, d),
            ckv_p.reshape(1, b, s, KV_LORA), rows(kr_p, b, s), kv5(rows(kb_p, b, s)), kv5(rows(vb_p, b, s)),
            rows(ki_p, b, s),
            ckv_s.reshape(1, bs, ts, KV_LORA), rows(kr_s, bs, ts), kv5(rows(kb_s, bs, ts)), kv5(rows(vb_s, bs, ts)),
            rows(ki_s, bs, ts))
```

```python
import functools
import math

import jax
import jax.numpy as jnp
import numpy as np
from jax import lax
from jax.experimental import pallas as pl
from jax.experimental.pallas import tpu as pltpu

F32 = jnp.float32
BF16 = jnp.bfloat16
I32 = jnp.int32

D_MODEL = 1024
PAGE = 128
MLA_HEADS = 8
MLA_NOPE = 64
MLA_ROPE = 32
MLA_V = 64
Q_LORA = 384
KV_LORA = 256
MLA_SCALE = (MLA_NOPE + MLA_ROPE) ** -0.5
DSA_HEADS = 8
DSA_KV_HEADS = 2
DSA_REP = DSA_HEADS // DSA_KV_HEADS
DSA_HEAD_DIM = 64
DSA_SCALE = DSA_HEAD_DIM ** -0.5
IDX_HEADS = 8
IDX_DIM = 64
IDX_ROPE = 32
IDX_TOPK_MAX = 256
IDX_W_SCALE = (IDX_HEADS * IDX_DIM) ** -0.5
REL_BUCKETS = 32
REL_MAX_DIST = 128
PEER_HEADS = 8
PEER_NKEYS = 128
PEER_EXPERTS = PEER_NKEYS * PEER_NKEYS
PEER_HALF = 128
PEER_TOPK = 16
ROPE_THETA = 10000.0
NORM_EPS = 1e-6

LANES = 128
SUBLANES = 8
MXU_N = 256
NEG = -1e30
INT_MIN = -(2 ** 31)
KCAT = KV_LORA + LANES
KV_CHUNK = 256
MLA_CHUNK = 512
Q_TILE = 128

_C_CQ, _C_CKV, _C_QB, _C_KB, _C_VB = 0, 384, 640, 1152, 1280
_C_QI, _C_QIP, _C_KR, _C_KRP, _C_KI, _C_KIP, _C_WI, _C_END = 1408, 1920, 2432, 2560, 2688, 2816, 2944, 3072
_VMEM_LIMIT = 56 * 1024 * 1024


def _cparams(n_axes):
    return pltpu.CompilerParams(dimension_semantics=("arbitrary",) * n_axes, vmem_limit_bytes=_VMEM_LIMIT)


def _dot(a, b):
    return jnp.dot(a, b, preferred_element_type=F32)


def _dot_nt(a, b):
    return lax.dot_general(a, b, (((1,), (1,)), ((), ())), preferred_element_type=F32)


def _rms(x, g):
    return x * lax.rsqrt(jnp.mean(x * x, axis=-1, keepdims=True) + NORM_EPS) * g


def _sort_key(x):
    x = jnp.where(x == 0.0, 0.0, x)
    bits = pltpu.bitcast(x, I32)
    return bits ^ ((bits >> 31) & 0x7FFFFFFF)


def _bucket_starts():
    max_exact = REL_BUCKETS // 2
    n = np.arange(0, 2 * REL_MAX_DIST, dtype=np.int64)
    nf = np.maximum(n, 1).astype(np.float32)
    large = max_exact + (np.log(nf / np.float32(max_exact)) / np.float32(math.log(REL_MAX_DIST / max_exact))
                         * np.float32(REL_BUCKETS - max_exact)).astype(np.int32)
    large = np.minimum(large, REL_BUCKETS - 1)
    bucket = np.where(n < max_exact, n, large)
    starts = []
    for k in range(REL_BUCKETS):
        hit = np.nonzero(bucket >= k)[0]
        starts.append(int(hit[0]) if hit.size else int(n[-1]) + 1)
    return starts


_BUCKET_START = _bucket_starts()


def _bias_kernel(rb_ref, bp_ref, bs_ref, bf_ref):
    def bias_of(n, h):
        b = jnp.full(n.shape, rb_ref[REL_BUCKETS - 1, h], F32)
        for k in range(REL_BUCKETS - 2, -1, -1):
            b = jnp.where(n < _BUCKET_START[k + 1], rb_ref[k, h], b)
        return b

    s_i = lax.broadcasted_iota(I32, (KV_CHUNK, Q_TILE), 0)
    t_i = lax.broadcasted_iota(I32, (KV_CHUNK, Q_TILE), 1)
    for w in range(4):
        n = jnp.maximum(w * Q_TILE + t_i - s_i, 0)
        for h in range(DSA_HEADS):
            bp_ref[w, h] = bias_of(n, h)
    t_s = lax.broadcasted_iota(I32, (SUBLANES, PAGE), 0)
    u_s = lax.broadcasted_iota(I32, (SUBLANES, PAGE), 1)
    for w in range(2):
        n = jnp.maximum((1 - w) * PAGE + t_s - u_s, 0)
        for h in range(DSA_HEADS):
            bs_ref[w, h] = bias_of(n, h)
    for h in range(DSA_HEADS):
        bf_ref[h] = jnp.full((SUBLANES, PAGE), rb_ref[REL_BUCKETS - 1, h], F32)


def _bias_tables(rel_bias):
    return pl.pallas_call(
        _bias_kernel,
        out_shape=(jax.ShapeDtypeStruct((4, DSA_HEADS, KV_CHUNK, Q_TILE), F32),
                   jax.ShapeDtypeStruct((2, DSA_HEADS, SUBLANES, PAGE), F32),
                   jax.ShapeDtypeStruct((DSA_HEADS, SUBLANES, PAGE), F32)),
        in_specs=[pl.BlockSpec(memory_space=pltpu.SMEM)],
        name="bias_tables",
    )(rel_bias)


def _inproj_kernel(x_ref, ga_ref, w1_ref, gq_ref, wuq_ref, gkv_ref, wuk_ref, ca_ref, sa_ref, cb_ref, sb_ref,
                   ckv_ref, krope_ref, kb_ref, vb_ref, ki_ref,
                   kcat_ref, kbbf_ref, vbbf_ref, kibf_ref, qcat_ref, qb_ref, qi_ref, wi_ref):
    xn = _rms(x_ref[...], ga_ref[...]).astype(BF16)

    def proj(lo, hi):
        return _dot(xn, w1_ref[:, lo:hi])

    ca, sa, cb, sb = ca_ref[...], sa_ref[...], cb_ref[...], sb_ref[...]

    ckv = _rms(proj(_C_CKV, _C_QB), gkv_ref[...])
    ckv_ref[...] = ckv
    kcat_ref[:, 0:KV_LORA] = ckv.astype(BF16)
    kr = proj(_C_KR, _C_KRP) * ca + proj(_C_KRP, _C_KI) * sa
    krope_ref[...] = kr.T[:MLA_ROPE]
    kcat_ref[:, KV_LORA:KCAT] = kr.astype(BF16)
    ki = proj(_C_KI, _C_KIP) * ca + proj(_C_KIP, _C_WI) * sa
    ki_ref[...] = ki.T[:IDX_DIM]
    kibf_ref[...] = ki[:, :IDX_DIM].astype(BF16)
    kb = proj(_C_KB, _C_VB)
    kb_ref[...] = kb.T
    kbbf_ref[...] = kb.astype(BF16)
    vb = proj(_C_VB, _C_QI)
    vb_ref[...] = vb.T
    vbbf_ref[...] = vb.astype(BF16)
    wi_ref[...] = proj(_C_WI, _C_END)[:, :IDX_HEADS] * IDX_W_SCALE

    qb = proj(_C_QB, _C_KB)
    for h in range(DSA_HEADS):
        qb_ref[h] = qb[:, h * DSA_HEAD_DIM:(h + 1) * DSA_HEAD_DIM].astype(BF16)
    qi = proj(_C_QI, _C_QIP)
    qip = proj(_C_QIP, _C_KR)
    for s in range(4):
        slab = qi[:, s * LANES:(s + 1) * LANES] * cb + qip[:, s * LANES:(s + 1) * LANES] * sb
        qi_ref[2 * s] = slab[:, :IDX_DIM].astype(BF16)
        qi_ref[2 * s + 1] = slab[:, IDX_DIM:].astype(BF16)

    cq = _rms(proj(_C_CQ, _C_CKV), gq_ref[...]).astype(BF16)
    n_nope = MLA_HEADS * MLA_NOPE
    n_pad = MLA_HEADS * LANES
    nope = _dot(cq, wuq_ref[:, 0:n_nope]).astype(BF16)
    for p in range(MLA_HEADS // 2):
        ql = _dot(nope[:, p * LANES:(p + 1) * LANES], wuk_ref[p])
        qcat_ref[2 * p, :, 0:KV_LORA] = ql[:, :KV_LORA].astype(BF16)
        qcat_ref[2 * p + 1, :, 0:KV_LORA] = ql[:, KV_LORA:].astype(BF16)
    for h in range(MLA_HEADS):
        lo = n_nope + h * LANES
        qr = _dot(cq, wuq_ref[:, lo:lo + LANES]) * ca + _dot(cq, wuq_ref[:, lo + n_pad:lo + n_pad + LANES]) * sa
        qcat_ref[h, :, KV_LORA:KCAT] = qr.astype(BF16)


def _inproj(x2d, tabs, seq, wts, tm):
    n = x2d.shape[0]
    tab_blocks = seq // tm
    n_seq = n // seq
    const2 = lambda i: (0, 0)
    const3 = lambda i: (0, 0, 0)
    row = lambda i: (i, 0)
    tab = lambda i: (i % tab_blocks, 0)
    hm = lambda i: (0, i, 0)
    col = lambda i: (i // tab_blocks, 0, i % tab_blocks)
    t_shape = lambda width: jax.ShapeDtypeStruct((n_seq, width, seq), F32)
    t_spec = lambda width: pl.BlockSpec((None, width, tm), col)
    in_specs = [
        pl.BlockSpec((tm, D_MODEL), row),
        pl.BlockSpec((1, D_MODEL), const2),
        pl.BlockSpec((D_MODEL, _C_END), const2),
        pl.BlockSpec((1, Q_LORA), const2),
        pl.BlockSpec(wts["wuq"].shape, const2),
        pl.BlockSpec((1, KV_LORA), const2),
        pl.BlockSpec(wts["wuk"].shape, const3),
    ] + [pl.BlockSpec((tm, LANES), tab)] * 4
    out_shape = (
        jax.ShapeDtypeStruct((n, KV_LORA), F32), t_shape(MLA_ROPE), t_shape(LANES), t_shape(LANES), t_shape(IDX_DIM),
        jax.ShapeDtypeStruct((n, KCAT), BF16), jax.ShapeDtypeStruct((n, LANES), BF16),
        jax.ShapeDtypeStruct((n, LANES), BF16), jax.ShapeDtypeStruct((n, IDX_DIM), BF16),
        jax.ShapeDtypeStruct((MLA_HEADS, n, KCAT), BF16),
        jax.ShapeDtypeStruct((DSA_HEADS, n, DSA_HEAD_DIM), BF16),
        jax.ShapeDtypeStruct((IDX_HEADS, n, IDX_DIM), BF16),
        jax.ShapeDtypeStruct((n, IDX_HEADS), F32),
    )
    out_specs = (
        pl.BlockSpec((tm, KV_LORA), row), t_spec(MLA_ROPE), t_spec(LANES), t_spec(LANES), t_spec(IDX_DIM),
        pl.BlockSpec((tm, KCAT), row), pl.BlockSpec((tm, LANES), row),
        pl.BlockSpec((tm, LANES), row), pl.BlockSpec((tm, IDX_DIM), row),
        pl.BlockSpec((MLA_HEADS, tm, KCAT), hm),
        pl.BlockSpec((DSA_HEADS, tm, DSA_HEAD_DIM), hm),
        pl.BlockSpec((IDX_HEADS, tm, IDX_DIM), hm),
        pl.BlockSpec((tm, IDX_HEADS), row),
    )
    return pl.pallas_call(
        _inproj_kernel, grid=(n // tm,), in_specs=in_specs, out_specs=out_specs, out_shape=out_shape,
        compiler_params=_cparams(1), name="inproj",
    )(x2d, wts["g_attn"], wts["w1"], wts["g_q"], wts["wuq"], wts["g_kv"], wts["wuk"], *tabs)


def _rep(x, width):
    k = width // LANES
    return x if k == 1 else jnp.concatenate([x] * k, axis=1)


def _mla_prompt_kernel(q_ref, k_ref, wuv_ref, o_ref, m_scr, l_scr, acc_scr):
    j = pl.program_id(1)
    rows = MLA_HEADS * Q_TILE
    q = q_ref[...].reshape(rows, KCAT)
    m_scr[...] = jnp.full(m_scr.shape, NEG, F32)
    l_scr[...] = jnp.zeros(l_scr.shape, F32)
    acc_scr[...] = jnp.zeros(acc_scr.shape, F32)
    n_full = (j * Q_TILE) // MLA_CHUNK

    def body(c, masked):
        k = k_ref[pl.ds(pl.multiple_of(c * MLA_CHUNK, MLA_CHUNK), MLA_CHUNK), :]
        s = _dot_nt(q, k) * MLA_SCALE
        if masked:
            t_row = j * Q_TILE + lax.broadcasted_iota(I32, (rows, MLA_CHUNK), 0) % Q_TILE
            u_col = lax.broadcasted_iota(I32, (rows, MLA_CHUNK), 1)
            s = jnp.where(c * MLA_CHUNK + u_col <= t_row, s, NEG)
        m_prev = m_scr[...]
        m_new = jnp.maximum(m_prev, jnp.max(s, axis=1, keepdims=True))
        alpha = jnp.exp(m_prev - m_new)
        p = jnp.exp(s - _rep(m_new, MLA_CHUNK))
        l_scr[...] = alpha * l_scr[...] + jnp.sum(p, axis=1, keepdims=True)
        acc_scr[...] = acc_scr[...] * _rep(alpha, KV_LORA) + _dot(p.astype(BF16), k[:, :KV_LORA])
        m_scr[...] = m_new

    lax.fori_loop(0, n_full, lambda c, carry: (body(c, False), carry)[1], 0)
    body(n_full, True)
    o_lat = (acc_scr[...] / _rep(l_scr[...], KV_LORA)).astype(BF16)
    for h in range(MLA_HEADS):
        o = _dot(o_lat[h * Q_TILE:(h + 1) * Q_TILE], wuv_ref[h])
        o_ref[:, h * MLA_V:(h + 1) * MLA_V] = o.astype(BF16)


def _mla_prompt(qcat, kcat, wuv, b, s):
    nq = s // Q_TILE
    rows = MLA_HEADS * Q_TILE
    return pl.pallas_call(
        _mla_prompt_kernel, grid=(b, nq),
        in_specs=[pl.BlockSpec((MLA_HEADS, Q_TILE, KCAT), lambda bi, j: (0, bi * nq + j, 0)),
                  pl.BlockSpec((None, s, KCAT), lambda bi, j: (bi, 0, 0)),
                  pl.BlockSpec(wuv.shape, lambda bi, j: (0, 0, 0))],
        out_specs=pl.BlockSpec((Q_TILE, MLA_HEADS * MLA_V), lambda bi, j: (bi * nq + j, 0)),
        out_shape=jax.ShapeDtypeStruct((b * s, MLA_HEADS * MLA_V), BF16),
        scratch_shapes=[pltpu.VMEM((rows, LANES), F32), pltpu.VMEM((rows, LANES), F32),
                        pltpu.VMEM((rows, KV_LORA), F32)],
        compiler_params=_cparams(2), name="mla_prompt",
    )(qcat, kcat.reshape(b, s, KCAT), wuv)


def _topk_threshold(key_scr, n_chunks, topk, n_keys_pow2_bits):
    lanes = key_scr.shape[1]
    sub = KV_CHUNK // SUBLANES

    def count(pred_fn):
        def body(c, acc):
            off = pl.multiple_of(c * KV_CHUNK, KV_CHUNK)
            k = key_scr[pl.ds(off, KV_CHUNK), :]
            hit = pred_fn(k, c).astype(I32)
            return acc + jnp.sum(hit.reshape(sub, SUBLANES, lanes), axis=0)

        acc = lax.fori_loop(0, n_chunks, body, jnp.zeros((SUBLANES, lanes), I32))
        return jnp.sum(acc, axis=0, keepdims=True)

    def bit_body(i, res):
        cand = res | jnp.left_shift(jnp.int32(1), 31 - i)
        cs = cand ^ INT_MIN
        cnt = count(lambda k, c: k >= cs)
        return jnp.where(cnt >= topk, cand, res)

    res = lax.fori_loop(0, 32, bit_body, jnp.zeros((1, lanes), I32))
    tau = res ^ INT_MIN
    cnt_gt = count(lambda k, c: k > tau)
    cnt_eq = count(lambda k, c: k == tau)
    need = topk - cnt_gt
    row0 = lax.broadcasted_iota(I32, (KV_CHUNK, lanes), 0)
    big = jnp.int32(2 ** 30)

    def cut_search():
        def cbody(i, cur):
            cand = cur | jnp.left_shift(jnp.int32(1), n_keys_pow2_bits - 1 - i)
            f = count(lambda k, c: jnp.where(k == tau, row0 + c * KV_CHUNK, big) < cand)
            return jnp.where(f < need, cand, cur)

        return lax.fori_loop(0, n_keys_pow2_bits, cbody, jnp.zeros((1, lanes), I32))

    cut = lax.cond(jnp.max(cnt_eq - need) > 0, cut_search, lambda: jnp.full((1, lanes), big, I32))
    return tau, cut


def _dsa_prompt_kernel(qi_ref, wt_ref, ki_ref, qb_ref, kb_ref, vt_ref, bias_ref, o_ref,
                       key_scr, mb_scr, tc_scr, m_scr, l_scr, acc_scr, *, topk, idx_bits):
    j = pl.program_id(1)
    n_chunks = (j * Q_TILE) // KV_CHUNK + 1
    t_row = j * Q_TILE + lax.broadcasted_iota(I32, (KV_CHUNK, Q_TILE), 1)
    s_loc = lax.broadcasted_iota(I32, (KV_CHUNK, Q_TILE), 0)
    qi = qi_ref[...].reshape(IDX_HEADS * Q_TILE, IDX_DIM)
    wt = wt_ref[...]

    def score_body(c, carry):
        off = pl.multiple_of(c * KV_CHUNK, KV_CHUNK)
        a = _dot_nt(ki_ref[pl.ds(off, KV_CHUNK), :], qi)
        sc = jnp.zeros((KV_CHUNK, Q_TILE), F32)
        for h in range(IDX_HEADS):
            sc = sc + wt[h:h + 1, :] * jnp.maximum(a[:, h * Q_TILE:(h + 1) * Q_TILE], 0.0)
        key = jnp.where(off + s_loc <= t_row, _sort_key(sc), INT_MIN)
        key_scr[pl.ds(off, KV_CHUNK), :] = key
        return carry

    lax.fori_loop(0, n_chunks, score_body, 0)

    @pl.when((j + 1) * Q_TILE <= topk)
    def _():
        tc_scr[0:1, :] = jnp.full((1, Q_TILE), INT_MIN, I32)
        tc_scr[1:2, :] = jnp.full((1, Q_TILE), -1, I32)

    @pl.when((j + 1) * Q_TILE > topk)
    def _():
        tau, cut = _topk_threshold(key_scr, n_chunks, topk, idx_bits)
        tc_scr[0:1, :] = tau
        tc_scr[1:2, :] = cut

    tau = tc_scr[0:1, :]
    cut = tc_scr[1:2, :]

    def mask_body(c, carry):
        off = pl.multiple_of(c * KV_CHUNK, KV_CHUNK)
        k = key_scr[pl.ds(off, KV_CHUNK), :]
        spos = off + s_loc
        v = jnp.where(k > tau, 0.0, jnp.where(k == tau, jnp.where(spos <= cut, 0.0, NEG), NEG))
        mb_scr[pl.ds(off, KV_CHUNK), :] = jnp.where(spos <= t_row, v, NEG)
        return carry

    lax.fori_loop(0, n_chunks, mask_body, 0)

    m_scr[...] = jnp.full(m_scr.shape, NEG, F32)
    l_scr[...] = jnp.zeros(l_scr.shape, F32)
    acc_scr[...] = jnp.zeros(acc_scr.shape, F32)
    qb = qb_ref[...]

    def att_body(c, carry):
        off = pl.multiple_of(c * KV_CHUNK, KV_CHUNK)
        kb = kb_ref[pl.ds(off, KV_CHUNK), :]
        mb = mb_scr[pl.ds(off, KV_CHUNK), :]
        bidx = jnp.minimum((j * Q_TILE - c * KV_CHUNK) // Q_TILE, 3)
        for g in range(DSA_KV_HEADS):
            kg = kb[:, g * DSA_HEAD_DIM:(g + 1) * DSA_HEAD_DIM]
            qg = qb[g * DSA_REP:(g + 1) * DSA_REP].reshape(DSA_REP * Q_TILE, DSA_HEAD_DIM)
            lg4 = _dot_nt(kg, qg) * DSA_SCALE
            vg = vt_ref[c, g * DSA_HEAD_DIM:(g + 1) * DSA_HEAD_DIM, :]
            for r in range(DSA_REP):
                h = g * DSA_REP + r
                lg = lg4[:, r * Q_TILE:(r + 1) * Q_TILE] + bias_ref[bidx, h] + mb
                m_prev = m_scr[h:h + 1, :]
                m_new = jnp.maximum(m_prev, jnp.max(lg, axis=0, keepdims=True))
                alpha = jnp.exp(m_prev - m_new)
                p = jnp.exp(lg - m_new)
                l_scr[h:h + 1, :] = alpha * l_scr[h:h + 1, :] + jnp.sum(p, axis=0, keepdims=True)
                rs = slice(h * DSA_HEAD_DIM, (h + 1) * DSA_HEAD_DIM)
                acc_scr[rs, :] = alpha * acc_scr[rs, :] + _dot(vg, p.astype(BF16))
                m_scr[h:h + 1, :] = m_new
        return carry

    lax.fori_loop(0, n_chunks, att_body, 0)
    inv = 1.0 / l_scr[...]
    parts = [acc_scr[h * DSA_HEAD_DIM:(h + 1) * DSA_HEAD_DIM, :] * inv[h:h + 1, :] for h in range(DSA_HEADS)]
    o_ref[...] = jnp.concatenate(parts, axis=0).T.astype(BF16)


def _dsa_prompt(qi_hm, wi_t, ki_bf, qb_hm, kb_bf, v_t, bias_p, b, s):
    nq = s // Q_TILE
    topk = min(IDX_TOPK_MAX, s // 4)
    idx_bits = max(1, int(math.ceil(math.log2(s))))
    width = DSA_HEADS * DSA_HEAD_DIM
    kern = functools.partial(_dsa_prompt_kernel, topk=topk, idx_bits=idx_bits)
    return pl.pallas_call(
        kern, grid=(b, nq),
        in_specs=[pl.BlockSpec((IDX_HEADS, Q_TILE, IDX_DIM), lambda bi, j: (0, bi * nq + j, 0)),
                  pl.BlockSpec((IDX_HEADS, Q_TILE), lambda bi, j: (0, bi * nq + j)),
                  pl.BlockSpec((None, s, IDX_DIM), lambda bi, j: (bi, 0, 0)),
                  pl.BlockSpec((DSA_HEADS, Q_TILE, DSA_HEAD_DIM), lambda bi, j: (0, bi * nq + j, 0)),
                  pl.BlockSpec((None, s, LANES), lambda bi, j: (bi, 0, 0)),
                  pl.BlockSpec((None, s // KV_CHUNK, LANES, KV_CHUNK), lambda bi, j: (bi, 0, 0, 0)),
                  pl.BlockSpec(bias_p.shape, lambda bi, j: (0, 0, 0, 0))],
        out_specs=pl.BlockSpec((Q_TILE, width), lambda bi, j: (bi * nq + j, 0)),
        out_shape=jax.ShapeDtypeStruct((b * s, width), BF16),
        scratch_shapes=[pltpu.VMEM((s, Q_TILE), I32), pltpu.VMEM((s, Q_TILE), F32), pltpu.VMEM((SUBLANES, Q_TILE), I32),
                        pltpu.VMEM((DSA_HEADS, Q_TILE), F32), pltpu.VMEM((DSA_HEADS, Q_TILE), F32),
                        pltpu.VMEM((width, Q_TILE), F32)],
        compiler_params=_cparams(2), name="dsa_prompt",
    )(qi_hm, wi_t, ki_bf.reshape(b, s, IDX_DIM), qb_hm, kb_bf.reshape(b, s, LANES), v_t, bias_p)


def _mxu_tiles(pages, axis):
    group = MXU_N // PAGE
    return [pages[i] if len(pages[i:i + group]) == 1 else jnp.concatenate(pages[i:i + group], axis=axis)
            for i in range(0, len(pages), group)]


def _sample1_kernel(pt_ref, ql_ref, qr_ref, qi_ref, wi_ref, ckvn_ref, krn_ref, kin_ref, ckv_hbm, kr_hbm, ki_hbm,
                    olat_ref, sc_ref, scn_ref, ckv_buf, kr_buf, ki_buf, sem, m_scr, l_scr, acc_scr, *, pp, n_pages):
    slot = _page_fetch(pt_ref, (ckv_hbm, kr_hbm, ki_hbm), (ckv_buf, kr_buf, ki_buf), sem, pp, n_pages)
    ckv_pages = [ckv_buf.at[slot, k] for k in range(pp)]
    kr_pages = [kr_buf.at[slot, k] for k in range(pp)]
    ki_pages = [ki_buf.at[slot, k] for k in range(pp)]
    j = pl.program_id(1)
    last = pl.num_programs(1) - 1
    rows = ql_ref.shape[0]
    n_tok = rows // MLA_HEADS

    @pl.when(j == 0)
    def _():
        m_scr[...] = jnp.full(m_scr.shape, NEG, F32)
        l_scr[...] = jnp.zeros(l_scr.shape, F32)
        acc_scr[...] = jnp.zeros(acc_scr.shape, F32)

    ql, qr, qi, wi = ql_ref[...], qr_ref[...], qi_ref[...], wi_ref[...]

    def attend(kcs, krs, mask):
        kcs, krs = _mxu_tiles(kcs, axis=0), _mxu_tiles(krs, axis=1)
        s = jnp.concatenate([_dot_nt(ql, kc) + _dot(qr, kr) for kc, kr in zip(kcs, krs)], axis=1) * MLA_SCALE
        if mask is not None:
            s = jnp.where(mask, s, NEG)
        m_prev = m_scr[...]
        m_new = jnp.maximum(m_prev, jnp.max(s, axis=1, keepdims=True))
        alpha = jnp.exp(m_prev - m_new)
        p = jnp.exp(s - _rep(m_new, s.shape[1])).astype(BF16)
        l_scr[...] = alpha * l_scr[...] + jnp.sum(p.astype(F32), axis=1, keepdims=True)
        pv, off = None, 0
        for kc in kcs:
            d = _dot(p[:, off:off + kc.shape[0]], kc)
            pv = d if pv is None else pv + d
            off += kc.shape[0]
        acc_scr[...] = acc_scr[...] * _rep(alpha, KV_LORA) + pv
        m_scr[...] = m_new

    def index(kidx_t):
        a = jnp.maximum(_dot(qi, kidx_t), 0.0) * wi
        return jnp.sum(a.reshape(n_tok, IDX_HEADS, kidx_t.shape[1]), axis=1)

    attend([r[...].astype(BF16) for r in ckv_pages], [r[...].astype(BF16) for r in kr_pages], None)
    off = 0
    for kt in _mxu_tiles([r[...].astype(BF16) for r in ki_pages], axis=1):
        sc_ref[:, off:off + kt.shape[1]] = index(kt)
        off += kt.shape[1]

    @pl.when(j == last)
    def _():
        t_r = lax.broadcasted_iota(I32, (rows, PAGE), 0) % n_tok
        u_c = lax.broadcasted_iota(I32, (rows, PAGE), 1)
        attend([ckvn_ref[...]], [krn_ref[...]], u_c <= t_r)
        t4 = lax.broadcasted_iota(I32, (n_tok, PAGE), 0)
        u4 = lax.broadcasted_iota(I32, (n_tok, PAGE), 1)
        scn_ref[...] = jnp.where(u4 <= t4, index(kin_ref[...]), -jnp.inf)
        olat_ref[...] = acc_scr[...] / _rep(l_scr[...], KV_LORA)


def _sample1(pt_flat, ql, qr, qi, wi, ckvn, krn_t, kin_t, c_ckv, c_kr_t, c_ki_t, n_pages, pp):
    bs, rows = ql.shape[0], ql.shape[1]
    n_tok = rows // MLA_HEADS
    per_b = lambda tail: pl.BlockSpec((None,) + tail, lambda bi, j, pt: (bi,) + (0,) * len(tail))
    in_specs = [per_b((rows, KV_LORA)), per_b((rows, MLA_ROPE)), per_b((rows, IDX_DIM)), per_b((rows, 1)),
                per_b((PAGE, KV_LORA)), per_b((MLA_ROPE, PAGE)), per_b((IDX_DIM, PAGE))]
    in_specs += [pl.BlockSpec(memory_space=pl.ANY)] * 3
    out_specs = (per_b((rows, KV_LORA)),
                 pl.BlockSpec((None, n_tok, pp * PAGE), lambda bi, j, pt: (bi, 0, j)),
                 per_b((n_tok, PAGE)))
    out_shape = (jax.ShapeDtypeStruct((bs, rows, KV_LORA), F32),
                 jax.ShapeDtypeStruct((bs, n_tok, n_pages * PAGE), F32),
                 jax.ShapeDtypeStruct((bs, n_tok, PAGE), F32))
    grid_spec = pltpu.PrefetchScalarGridSpec(
        num_scalar_prefetch=1, grid=(bs, n_pages // pp), in_specs=in_specs, out_specs=out_specs,
        scratch_shapes=[pltpu.VMEM((2, pp, PAGE, KV_LORA), F32), pltpu.VMEM((2, pp, MLA_ROPE, PAGE), F32),
                        pltpu.VMEM((2, pp, IDX_DIM, PAGE), F32), pltpu.SemaphoreType.DMA((2, 3)),
                        pltpu.VMEM((rows, LANES), F32), pltpu.VMEM((rows, LANES), F32),
                        pltpu.VMEM((rows, KV_LORA), F32)])
    return pl.pallas_call(
        functools.partial(_sample1_kernel, pp=pp, n_pages=n_pages), grid_spec=grid_spec, out_shape=out_shape,
        compiler_params=_cparams(2), name="sample_mla_index",
    )(pt_flat, ql, qr, qi, wi, ckvn, krn_t, kin_t, c_ckv, c_kr_t, c_ki_t)


def _mla_out_kernel(o_ref, wuv_ref, out_ref):
    for h in range(MLA_HEADS):
        out_ref[:, h * MLA_V:(h + 1) * MLA_V] = _dot(o_ref[h], wuv_ref[h]).astype(BF16)


def _mla_out(olat_hm, wuv):
    n = olat_hm.shape[1]
    return pl.pallas_call(
        _mla_out_kernel, out_shape=jax.ShapeDtypeStruct((n, MLA_HEADS * MLA_V), BF16), name="sample_mla_out",
    )(olat_hm, wuv)


def _sample_select_kernel(sc_ref, tau_ref, cut_ref, key_scr, *, topk, idx_bits):
    n_chunks = sc_ref.shape[0] // KV_CHUNK

    def kbody(c, carry):
        off = pl.multiple_of(c * KV_CHUNK, KV_CHUNK)
        key_scr[pl.ds(off, KV_CHUNK), :] = _sort_key(sc_ref[pl.ds(off, KV_CHUNK), :])
        return carry

    lax.fori_loop(0, n_chunks, kbody, 0)
    tau, cut = _topk_threshold(key_scr, n_chunks, topk, idx_bits)
    tau_ref[...] = tau
    cut_ref[...] = cut


def _sample_select(sc_t, topk):
    kp, ns = sc_t.shape
    lt = min(LANES, ns)
    idx_bits = max(1, int(math.ceil(math.log2(kp))))
    kern = functools.partial(_sample_select_kernel, topk=topk, idx_bits=idx_bits)
    return pl.pallas_call(
        kern, grid=(ns // lt,),
        in_specs=[pl.BlockSpec((kp, lt), lambda i: (0, i))],
        out_specs=(pl.BlockSpec((1, lt), lambda i: (0, i)), pl.BlockSpec((1, lt), lambda i: (0, i))),
        out_shape=(jax.ShapeDtypeStruct((1, ns), I32), jax.ShapeDtypeStruct((1, ns), I32)),
        scratch_shapes=[pltpu.VMEM((kp, lt), I32)],
        compiler_params=_cparams(1), name="sample_select",
    )(sc_t)


def _page_fetch(pt_ref, caches, bufs, sem, pp, n_pages):
    bi, j, nj = pl.program_id(0), pl.program_id(1), pl.num_programs(1)
    step = bi * nj + j
    slot = step % 2

    def copies(first_page, sl):
        out = []
        for k in range(pp):
            page = 0 if first_page is None else pt_ref[first_page + k]
            for i, (cache, buf) in enumerate(zip(caches, bufs)):
                out.append(pltpu.make_async_copy(cache.at[page], buf.at[sl, k], sem.at[sl, i]))
        return out

    @pl.when(step == 0)
    def _():
        for c in copies(0, 0):
            c.start()

    @pl.when(step + 1 < pl.num_programs(0) * nj)
    def _():
        nxt = step + 1
        for c in copies((nxt // nj) * n_pages + (nxt % nj) * pp, 1 - slot):
            c.start()

    for c in copies(None, slot):
        c.wait()
    return slot


def _sample3_kernel(pt_ref, qb_ref, sc_ref, scn_ref, tau_ref, cut_ref, kn_ref, vn_ref, bs_ref, bf_ref, ck_hbm, cv_hbm,
                    o_ref, kbuf, vbuf, sem, m_scr, l_scr, acc_scr, *, pp, past):
    slot = _page_fetch(pt_ref, (ck_hbm, cv_hbm), (kbuf, vbuf), sem, pp, past // PAGE)
    k_pages = [kbuf.at[slot, k] for k in range(pp)]
    v_pages = [vbuf.at[slot, k] for k in range(pp)]
    j = pl.program_id(1)
    last = pl.num_programs(1) - 1
    n_tok = sc_ref.shape[0]

    @pl.when(j == 0)
    def _():
        m_scr[...] = jnp.full(m_scr.shape, NEG, F32)
        l_scr[...] = jnp.zeros(l_scr.shape, F32)
        acc_scr[...] = jnp.zeros(acc_scr.shape, F32)

    tau, cut = tau_ref[...], cut_ref[...]

    def mask_bias(sc, base):
        k = _sort_key(sc)
        spos = base + lax.broadcasted_iota(I32, sc.shape, 1)
        return jnp.where(k > tau, 0.0, jnp.where(k == tau, jnp.where(spos <= cut, 0.0, NEG), NEG))

    def attend(kts, vts, mb4, biases):
        mb = jnp.concatenate([mb4] * DSA_REP, axis=0)
        kts, vts = _mxu_tiles(kts, axis=1), _mxu_tiles(vts, axis=1)
        for g in range(DSA_KV_HEADS):
            rs = slice(g * DSA_HEAD_DIM, (g + 1) * DSA_HEAD_DIM)
            lg = jnp.concatenate([_dot(qb_ref[g], kt[rs, :]) for kt in kts], axis=1) * DSA_SCALE
            lg = lg + jnp.concatenate([b[g] for b in biases], axis=1) + mb
            m_prev = m_scr[g]
            m_new = jnp.maximum(m_prev, jnp.max(lg, axis=1, keepdims=True))
            alpha = jnp.exp(m_prev - m_new)
            p = jnp.exp(lg - _rep(m_new, lg.shape[1])).astype(BF16)
            l_scr[g] = alpha * l_scr[g] + jnp.sum(p.astype(F32), axis=1, keepdims=True)
            pv, off = None, 0
            for vt in vts:
                d = _dot_nt(p[:, off:off + vt.shape[1]], vt[rs, :])
                pv = d if pv is None else pv + d
                off += vt.shape[1]
            acc_scr[g] = acc_scr[g] * alpha[:, :DSA_HEAD_DIM] + pv
            m_scr[g] = m_new

    far = bf_ref[...]
    biases = [far] * (pp - 1) + [jnp.where(j == last, bs_ref[0], far)]
    attend([r[...].astype(BF16) for r in k_pages], [r[...].astype(BF16) for r in v_pages],
           mask_bias(sc_ref[...], j * (pp * PAGE)), biases)

    @pl.when(j == last)
    def _():
        attend([kn_ref[...]], [vn_ref[...]], mask_bias(scn_ref[...], past), [bs_ref[1]])
        for g in range(DSA_KV_HEADS):
            o_ref[g] = acc_scr[g] / l_scr[g][:, :DSA_HEAD_DIM]


def _sample3(pt_flat, qb, sc, scn, tau, cut, kn_t, vn_t, bias_s, bias_f, c_kt, c_vt, n_pages, pp):
    bs, n_tok = sc.shape[0], sc.shape[1]
    rows = DSA_REP * n_tok
    per_b = lambda tail: pl.BlockSpec((None,) + tail, lambda bi, j, pt: (bi,) + (0,) * len(tail))
    const = lambda shape: pl.BlockSpec(shape, lambda bi, j, pt: (0,) * len(shape))
    in_specs = [per_b((DSA_KV_HEADS, rows, DSA_HEAD_DIM)),
                pl.BlockSpec((None, n_tok, pp * PAGE), lambda bi, j, pt: (bi, 0, j)),
                per_b((n_tok, PAGE)), per_b((n_tok, 1)), per_b((n_tok, 1)),
                per_b((LANES, PAGE)), per_b((LANES, PAGE)),
                const(bias_s.shape), const(bias_f.shape),
                pl.BlockSpec(memory_space=pl.ANY), pl.BlockSpec(memory_space=pl.ANY)]
    grid_spec = pltpu.PrefetchScalarGridSpec(
        num_scalar_prefetch=1, grid=(bs, n_pages // pp), in_specs=in_specs,
        out_specs=per_b((DSA_KV_HEADS, rows, DSA_HEAD_DIM)),
        scratch_shapes=[pltpu.VMEM((2, pp, LANES, PAGE), F32), pltpu.VMEM((2, pp, LANES, PAGE), F32),
                        pltpu.SemaphoreType.DMA((2, 2)),
                        pltpu.VMEM((DSA_KV_HEADS, rows, LANES), F32), pltpu.VMEM((DSA_KV_HEADS, rows, LANES), F32),
                        pltpu.VMEM((DSA_KV_HEADS, rows, DSA_HEAD_DIM), F32)])
    kern = functools.partial(_sample3_kernel, pp=pp, past=n_pages * PAGE)
    return pl.pallas_call(
        kern, grid_spec=grid_spec,
        out_shape=jax.ShapeDtypeStruct((bs, DSA_KV_HEADS, rows, DSA_HEAD_DIM), F32),
        compiler_params=_cparams(2), name="sample_dsa",
    )(pt_flat, qb, sc, scn, tau, cut, kn_t, vn_t, bias_s, bias_f, c_kt, c_vt)


_N_EXTRACT = PEER_TOPK + 1


def _extract_top(cur, n):
    vals = []
    for _ in range(n):
        m = jnp.max(cur, axis=0, keepdims=True)
        vals.append(m)
        cur = jnp.where(cur == m, -jnp.inf, cur)
    return vals


def _peer_prep_kernel(x_ref, mla_ref, dsa_ref, wo_ref, g_ref, wpq_ref, keys_ref,
                      h_ref, xnt_ref, thr_ref, a_ref, s2_ref, b_ref):
    half = wo_ref.shape[0] // 2
    h = x_ref[...] + _dot(mla_ref[...], wo_ref[0:half, :]) + _dot(dsa_ref[...], wo_ref[half:, :])
    h_ref[...] = h
    xnt = _rms(h, g_ref[...]).T.astype(BF16)
    xnt_ref[...] = xnt
    tc = xnt.shape[1]
    r8 = lax.broadcasted_iota(I32, (SUBLANES, LANES), 0)
    for hh in range(PEER_HEADS):
        for p, ref in ((0, thr_ref), (1, s2_ref)):
            hp = hh * 2 + p
            qt = _dot(wpq_ref[hp * PEER_HALF:(hp + 1) * PEER_HALF, :], xnt)
            ref[hh] = _dot(keys_ref[hp], qt.astype(BF16))
        for lt in range(tc // LANES):
            ls = slice(lt * LANES, (lt + 1) * LANES)
            s1, s2 = thr_ref[hh, :, ls], s2_ref[hh, :, ls]
            sv1 = _extract_top(s1, _N_EXTRACT)
            sv2 = _extract_top(s2, _N_EXTRACT)
            sv2_16 = jnp.concatenate(sv2[:PEER_TOPK], axis=0)
            sv2_8 = sv2_16[:8]
            blocks = [sv1[0] + sv2_16]
            for r1 in range(1, 8):
                blocks.append(jnp.where(r8 < PEER_TOPK // (r1 + 1), sv1[r1] + sv2_8, -jnp.inf))
            blocks.append(jnp.concatenate(sv1[8:PEER_TOPK], axis=0) + sv2[0])
            extra = jnp.where(r8 == 0, sv1[0] + sv2[PEER_TOPK],
                              jnp.where(r8 == 1, sv1[PEER_TOPK] + sv2[0], -jnp.inf))
            blocks.append(extra)
            cand = _extract_top(jnp.concatenate(blocks, axis=0), _N_EXTRACT)
            m0 = sv1[0] + sv2[0]
            z = jnp.zeros_like(m0)
            for r in range(PEER_TOPK):
                z = z + jnp.exp(cand[r] - m0)
            c16, c17 = cand[PEER_TOPK - 1], cand[PEER_TOPK]
            tau = jnp.where(c17 == -jnp.inf, c16, 0.5 * (c16 + c17))
            thr_ref[hh, :, ls] = tau - s1
            a_ref[hh, :, ls] = jnp.exp(s1 - sv1[0]) / z * 0.5
            b_ref[hh, :, ls] = jnp.exp(s2 - sv2[0])


def _peer_prep(x2d, mla, dsa, wts, tc):
    n = x2d.shape[0]
    row = lambda i: (i, 0)
    const2 = lambda i: (0, 0)
    col3 = lambda i: (0, 0, i)
    gate_shape = jax.ShapeDtypeStruct((PEER_HEADS, PEER_NKEYS, n), F32)
    gate_spec = pl.BlockSpec((PEER_HEADS, PEER_NKEYS, tc), col3)
    mix = mla.shape[1]
    return pl.pallas_call(
        _peer_prep_kernel, grid=(n // tc,),
        in_specs=[pl.BlockSpec((tc, D_MODEL), row), pl.BlockSpec((tc, mix), row), pl.BlockSpec((tc, mix), row),
                  pl.BlockSpec(wts["w_out"].shape, const2), pl.BlockSpec((1, D_MODEL), const2),
                  pl.BlockSpec(wts["wpq_t"].shape, const2), pl.BlockSpec(wts["peer_keys"].shape, lambda i: (0, 0, 0))],
        out_specs=(pl.BlockSpec((tc, D_MODEL), row), pl.BlockSpec((D_MODEL, tc), lambda i: (0, i)),
                   gate_spec, gate_spec, gate_spec, gate_spec),
        out_shape=(jax.ShapeDtypeStruct((n, D_MODEL), F32), jax.ShapeDtypeStruct((D_MODEL, n), BF16),
                   gate_shape, gate_shape, gate_shape, gate_shape),
        compiler_params=_cparams(1), name="peer_prep",
    )(x2d, mla, dsa, wts["w_out"], wts["g_ffn"], wts["wpq_t"], wts["peer_keys"])


def _gelu_x2(x):
    return x * (1.0 + lax.erf(x * np.float32(math.sqrt(0.5))))


def _peer_chain_kernel(xnt_ref, thr_ref, a_ref, s2_ref, b_ref, u_ref, vt_ref, h_ref, gf_ref, y_ref, acc_scr, *, ni):
    e = pl.program_id(1)

    @pl.when(e == 0)
    def _():
        acc_scr[...] = jnp.zeros(acc_scr.shape, F32)

    tc = acc_scr.shape[1]
    tw = min(tc, MXU_N)
    per_slice = MXU_N // PEER_NKEYS
    chains = [(ks, k) for ks in range(ni // per_slice) for k in range(tc // tw)]
    n_r = D_MODEL // MXU_N

    def gates(ks, k):
        out = []
        for i2 in range(per_slice):
            i1 = e * ni + ks * per_slice + i2
            thr_rows = [thr_ref[hh, pl.ds(i1, 1), :] for hh in range(PEER_HEADS)]
            a_rows = [a_ref[hh, pl.ds(i1, 1), :] for hh in range(PEER_HEADS)]
            for lt in range(tw // LANES):
                ls = slice(k * tw + lt * LANES, k * tw + (lt + 1) * LANES)
                gate = None
                for hh in range(PEER_HEADS):
                    term = jnp.where(s2_ref[hh, :, ls] >= thr_rows[hh][:, ls], b_ref[hh, :, ls], 0.0)
                    term = term * a_rows[hh][:, ls]
                    gate = term if gate is None else gate + term
                out.append(gate)
        return out

    for ks, k in chains:
        es = slice(ks * MXU_N, (ks + 1) * MXU_N)
        cs = slice(k * tw, (k + 1) * tw)
        act = _gelu_x2(_dot(u_ref[es, :], xnt_ref[:, cs]))
        g_cur = gates(ks, k)
        n_lt = tw // LANES
        parts = []
        for i2 in range(per_slice):
            tiles = [(g_cur[i2 * n_lt + lt] * act[i2 * PEER_NKEYS:(i2 + 1) * PEER_NKEYS,
                                                   lt * LANES:(lt + 1) * LANES]).astype(BF16) for lt in range(n_lt)]
            parts.append(tiles[0] if n_lt == 1 else jnp.concatenate(tiles, axis=1))
        w = jnp.concatenate(parts, axis=0)
        for r in range(n_r):
            rs = slice(r * MXU_N, (r + 1) * MXU_N)
            acc_scr[rs, cs] += _dot(vt_ref[rs, es], w)

    @pl.when(e == pl.num_programs(1) - 1)
    def _():
        y_ref[...] = _rms(acc_scr[...].T + h_ref[...], gf_ref[...])


def _peer_chain(xnt, thr, a, s2, b, h, wts, tc, ni):
    n = h.shape[0]
    eb = ni * PEER_NKEYS
    gate_spec = pl.BlockSpec((PEER_HEADS, PEER_NKEYS, tc), lambda i, e: (0, 0, i))
    return pl.pallas_call(
        functools.partial(_peer_chain_kernel, ni=ni), grid=(n // tc, PEER_EXPERTS // eb),
        in_specs=[pl.BlockSpec((D_MODEL, tc), lambda i, e: (0, i)), gate_spec, gate_spec, gate_spec, gate_spec,
                  pl.BlockSpec((eb, D_MODEL), lambda i, e: (e, 0)), pl.BlockSpec((D_MODEL, eb), lambda i, e: (0, e)),
                  pl.BlockSpec((tc, D_MODEL), lambda i, e: (i, 0)), pl.BlockSpec((1, D_MODEL), lambda i, e: (0, 0))],
        out_specs=pl.BlockSpec((tc, D_MODEL), lambda i, e: (i, 0)),
        out_shape=jax.ShapeDtypeStruct((n, D_MODEL), F32),
        scratch_shapes=[pltpu.VMEM((D_MODEL, tc), F32)],
        compiler_params=_cparams(2), name="peer_main",
    )(xnt, thr, a, s2, b, wts["peer_u"], wts["peer_vt"], h, wts["g_final"])


def _peer(x2d, mla, dsa, wts):
    n = x2d.shape[0]
    tc = min(512, n)
    h, xnt, thr, a, s2, b = _peer_prep(x2d, mla, dsa, wts, tc)
    return _peer_chain(xnt, thr, a, s2, b, h, wts, tc, ni=16)


def _pad_cols(w, n):
    return jnp.pad(w, ((0, 0), (0, n - w.shape[1])))


def _swap_halves(w):
    half = w.shape[-1] // 2
    return jnp.concatenate([w[..., half:], w[..., :half]], axis=-1)


def _prep_weights(g_attn, w_in, g_q, w_uq, g_kv, w_uk, w_uv, w_out, g_ffn, w_pq, peer_keys, peer_u, peer_v, g_final):
    w_cq, w_ckv, w_kr = w_in[:, 0:384], w_in[:, 384:640], w_in[:, 640:672]
    w_qb, w_kb, w_vb = w_in[:, 672:1184], w_in[:, 1184:1312], w_in[:, 1312:1440]
    w_qi, w_ki, w_wi = w_in[:, 1440:1952], w_in[:, 1952:2016], w_in[:, 2016:2024]
    qi3 = w_qi.reshape(D_MODEL, IDX_HEADS, IDX_DIM)
    qi_partner = jnp.concatenate([_swap_halves(qi3[..., :IDX_ROPE]), jnp.zeros_like(qi3[..., IDX_ROPE:])], axis=-1)
    w1 = jnp.concatenate([
        w_cq, w_ckv, w_qb, w_kb, w_vb, w_qi, qi_partner.reshape(D_MODEL, IDX_HEADS * IDX_DIM),
        _pad_cols(w_kr, LANES), _pad_cols(_swap_halves(w_kr), LANES),
        _pad_cols(w_ki, LANES), _pad_cols(_swap_halves(w_ki[:, :IDX_ROPE]), LANES),
        _pad_cols(w_wi, LANES)], axis=1).astype(BF16)
    uq3 = w_uq.reshape(Q_LORA, MLA_HEADS, MLA_NOPE + MLA_ROPE)
    rope3 = uq3[..., MLA_NOPE:]
    pad3 = lambda w: jnp.pad(w, ((0, 0), (0, 0), (0, LANES - MLA_ROPE))).reshape(Q_LORA, MLA_HEADS * LANES)
    wuq = jnp.concatenate([uq3[..., :MLA_NOPE].reshape(Q_LORA, MLA_HEADS * MLA_NOPE),
                           pad3(rope3), pad3(_swap_halves(rope3))], axis=1).astype(BF16)
    ukt = jnp.transpose(w_uk, (1, 2, 0))
    zero = jnp.zeros((MLA_NOPE, KV_LORA), F32)
    wuk = jnp.stack([jnp.concatenate([jnp.concatenate([ukt[2 * p], zero], axis=1),
                                      jnp.concatenate([zero, ukt[2 * p + 1]], axis=1)], axis=0)
                     for p in range(MLA_HEADS // 2)]).astype(BF16)
    return dict(
        g_attn=g_attn.reshape(1, -1), w1=w1, g_q=g_q.reshape(1, -1), wuq=wuq, g_kv=g_kv.reshape(1, -1), wuk=wuk,
        wuv=jnp.transpose(w_uv, (1, 0, 2)).astype(BF16), w_out=w_out.astype(BF16), g_ffn=g_ffn.reshape(1, -1),
        wpq_t=w_pq.T.astype(BF16),
        peer_keys=peer_keys.reshape(PEER_HEADS * 2, PEER_NKEYS, PEER_HALF).astype(BF16),
        peer_u=peer_u.astype(BF16), peer_vt=peer_v.T.astype(BF16), g_final=g_final.reshape(1, -1))


def _rope_tables(pos):
    half = MLA_ROPE // 2
    inv = ROPE_THETA ** (-jnp.arange(half, dtype=F32) / half)
    ang = pos.astype(F32)[:, None] * inv
    cos, sin = jnp.cos(ang), jnp.sin(ang)
    c32 = jnp.concatenate([cos, cos], axis=1)
    s32 = jnp.concatenate([-sin, sin], axis=1)
    n = pos.shape[0]
    one, zero = jnp.ones((n, 32), F32), jnp.zeros((n, 32), F32)
    ca = jnp.concatenate([c32, one, one, one], axis=1)
    sa = jnp.concatenate([s32, zero, zero, zero], axis=1)
    cb = jnp.concatenate([c32, one, c32, one], axis=1)
    sb = jnp.concatenate([s32, zero, s32, zero], axis=1)
    return ca, sa, cb, sb


def _pick_tile(n, choices):
    for c in choices:
        if n % c == 0:
            return c
    raise ValueError(f"no tile in {choices} divides {n}")


def kernel(x_prompt, x_sample, cache_ckv, cache_krope, cache_k, cache_v, cache_kidx, page_table, rel_bias, g_attn,
           w_in, g_q, w_uq, g_kv, w_uk, w_uv, w_out, g_ffn, w_pq, peer_keys, peer_u, peer_v, g_final):
    assert g_attn.shape[0] == 1, "single-layer kernel"
    b, s, d = x_prompt.shape
    bs, ts, _ = x_sample.shape
    n_pages = page_table.shape[1]
    past = n_pages * PAGE
    assert s % MLA_CHUNK == 0 and s % KV_CHUNK == 0 and ts <= 8 and (bs * ts) % LANES == 0
    wts = _prep_weights(g_attn[0], w_in[0], g_q[0], w_uq[0], g_kv[0], w_uk[0], w_uv[0], w_out[0], g_ffn[0],
                        w_pq[0], peer_keys[0], peer_u[0], peer_v[0], g_final)
    bias_p, bias_s, bias_f = _bias_tables(rel_bias)

    xp = x_prompt.reshape(b * s, d)
    tm = _pick_tile(s, (512, 256))
    (ckv_p, kr_p, kb_p, vb_p, ki_p, kcat, kb_bf, vb_bf, ki_bf, qcat, qb_hm, qi_hm, wi_p) = _inproj(
        xp, _rope_tables(jnp.arange(s)), s, wts, tm)
    mla_p = _mla_prompt(qcat, kcat, wts["wuv"], b, s)
    v_t = jnp.transpose(vb_bf.reshape(b, s // KV_CHUNK, KV_CHUNK, LANES), (0, 1, 3, 2))
    dsa_p = _dsa_prompt(qi_hm, wi_p.T, ki_bf, qb_hm, kb_bf, v_t, bias_p, b, s)
    y_p = _peer(xp, mla_p, dsa_p, wts)

    ns = bs * ts
    xs = x_sample.reshape(ns, d)
    pos_s = past + jnp.tile(jnp.arange(ts), bs)
    (ckv_s, kr_s, kb_s, vb_s, ki_s, kcat_s, kb_sbf, vb_sbf, ki_sbf, qcat_s, qb_shm, qi_shm, wi_s) = _inproj(
        xs, _rope_tables(pos_s), ns, wts, ns)
    pt_flat = page_table.reshape(-1).astype(I32)
    pp = _pick_tile(n_pages, (32, 16, 8, 4, 2, 1))
    q5 = qcat_s.reshape(MLA_HEADS, bs, ts, KCAT).transpose(1, 0, 2, 3).reshape(bs, MLA_HEADS * ts, KCAT)
    qi_s = qi_shm.reshape(IDX_HEADS, bs, ts, IDX_DIM).transpose(1, 2, 0, 3).reshape(bs, ts * IDX_HEADS, IDX_DIM)
    wi_col = wi_s.reshape(bs, ts * IDX_HEADS, 1)
    pad_new = lambda a: jnp.pad(a.reshape(bs, ts, a.shape[-1]), ((0, 0), (0, PAGE - ts), (0, 0)))
    pad_new_t = lambda a: jnp.swapaxes(pad_new(a), 1, 2)
    olat, sc_past, sc_new = _sample1(
        pt_flat, q5[..., :KV_LORA], q5[..., KV_LORA:KV_LORA + MLA_ROPE], qi_s, wi_col,
        pad_new(kcat_s[:, :KV_LORA]), pad_new_t(kcat_s[:, KV_LORA:KV_LORA + MLA_ROPE]), pad_new_t(ki_sbf),
        cache_ckv[0], jnp.swapaxes(cache_krope[0], 1, 2), jnp.swapaxes(cache_kidx[0], 1, 2), n_pages, pp)
    olat_hm = olat.reshape(bs, MLA_HEADS, ts, KV_LORA).transpose(1, 0, 2, 3).reshape(MLA_HEADS, ns, KV_LORA)
    mla_s = _mla_out(olat_hm.astype(BF16), wts["wuv"])
    topk_s = min(IDX_TOPK_MAX, (past + ts) // 4)
    kp = -(-(past + PAGE) // KV_CHUNK) * KV_CHUNK
    sc_all = jnp.concatenate([sc_past, sc_new], axis=2).reshape(ns, past + PAGE)
    sc_t = jnp.pad(sc_all, ((0, 0), (0, kp - past - PAGE)), constant_values=-jnp.inf).T
    tau_s, cut_s = _sample_select(sc_t, topk_s)
    qb_s = qb_shm.reshape(DSA_KV_HEADS, DSA_REP, bs, ts, DSA_HEAD_DIM).transpose(2, 0, 1, 3, 4)
    qb_s = qb_s.reshape(bs, DSA_KV_HEADS, DSA_REP * ts, DSA_HEAD_DIM)
    bias_s4 = bias_s[:, :, :ts, :].reshape(2, DSA_KV_HEADS, DSA_REP * ts, PAGE)
    bias_f4 = bias_f[:, :ts, :].reshape(DSA_KV_HEADS, DSA_REP * ts, PAGE)
    n_pool = cache_k.shape[1]
    page_t = lambda c: jnp.transpose(c[0], (0, 2, 3, 1)).reshape(n_pool, LANES, PAGE)
    o_s = _sample3(pt_flat, qb_s, sc_past, sc_new, tau_s.reshape(bs, ts, 1), cut_s.reshape(bs, ts, 1),
                   pad_new_t(kb_sbf), pad_new_t(vb_sbf), bias_s4, bias_f4,
                   page_t(cache_k), page_t(cache_v), n_pages, pp)
    dsa_s = o_s.reshape(bs, DSA_KV_HEADS, DSA_REP, ts, DSA_HEAD_DIM).transpose(0, 3, 1, 2, 4)
    dsa_s = dsa_s.reshape(ns, DSA_HEADS * DSA_HEAD_DIM).astype(BF16)
    y_s = _peer(xs, mla_s, dsa_s, wts)

    def rows(a_t, nb, nt):
        return jnp.swapaxes(a_t, 1, 2).reshape(1, nb, nt, a_t.shape[1])

    kv5 = lambda a: a.reshape(a.shape[:3] + (DSA_KV_HEADS, DSA_HEAD_DIM))
    return (y_p.reshape(b, s, d), y_s.reshape(bs, ts, d),
            ckv_p.reshape(1, b, s, KV_LORA), rows(kr_p, b, s), kv5(rows(kb_p, b, s)), kv5(rows(vb_p, b, s)),
            rows(ki_p, b, s),
            ckv_s.reshape(1, bs, ts, KV_LORA), rows(kr_s, bs, ts), kv5(rows(kb_s, bs, ts)), kv5(rows(vb_s, bs, ts)),
            rows(ki_s, bs, ts))
```

```python
import functools
import math

import jax
import jax.numpy as jnp
import numpy as np
from jax import lax
from jax.experimental import pallas as pl
from jax.experimental.pallas import tpu as pltpu

F32 = jnp.float32
BF16 = jnp.bfloat16
I32 = jnp.int32

D_MODEL = 1024
PAGE = 128
MLA_HEADS = 8
MLA_NOPE = 64
MLA_ROPE = 32
MLA_V = 64
Q_LORA = 384
KV_LORA = 256
MLA_SCALE = (MLA_NOPE + MLA_ROPE) ** -0.5
DSA_HEADS = 8
DSA_KV_HEADS = 2
DSA_REP = DSA_HEADS // DSA_KV_HEADS
DSA_HEAD_DIM = 64
DSA_SCALE = DSA_HEAD_DIM ** -0.5
IDX_HEADS = 8
IDX_DIM = 64
IDX_ROPE = 32
IDX_TOPK_MAX = 256
IDX_W_SCALE = (IDX_HEADS * IDX_DIM) ** -0.5
REL_BUCKETS = 32
REL_MAX_DIST = 128
PEER_HEADS = 8
PEER_NKEYS = 128
PEER_EXPERTS = PEER_NKEYS * PEER_NKEYS
PEER_HALF = 128
PEER_TOPK = 16
ROPE_THETA = 10000.0
NORM_EPS = 1e-6

LANES = 128
SUBLANES = 8
MXU_N = 256
NEG = -1e30
INT_MIN = -(2 ** 31)
KCAT = KV_LORA + LANES
KV_CHUNK = 256
MLA_CHUNK = 512
Q_TILE = 128

_C_CQ, _C_CKV, _C_QB, _C_KB, _C_VB = 0, 384, 640, 1152, 1280
_C_QI, _C_QIP, _C_KR, _C_KRP, _C_KI, _C_KIP, _C_WI, _C_END = 1408, 1920, 2432, 2560, 2688, 2816, 2944, 3072
_VMEM_LIMIT = 56 * 1024 * 1024


def _cparams(n_axes):
    return pltpu.CompilerParams(dimension_semantics=("arbitrary",) * n_axes, vmem_limit_bytes=_VMEM_LIMIT)


def _dot(a, b):
    return jnp.dot(a, b, preferred_element_type=F32)


def _dot_nt(a, b):
    return lax.dot_general(a, b, (((1,), (1,)), ((), ())), preferred_element_type=F32)


def _rms(x, g):
    return x * lax.rsqrt(jnp.mean(x * x, axis=-1, keepdims=True) + NORM_EPS) * g


def _sort_key(x):
    x = jnp.where(x == 0.0, 0.0, x)
    bits = pltpu.bitcast(x, I32)
    return bits ^ ((bits >> 31) & 0x7FFFFFFF)


def _bucket_starts():
    max_exact = REL_BUCKETS // 2
    n = np.arange(0, 2 * REL_MAX_DIST, dtype=np.int64)
    nf = np.maximum(n, 1).astype(np.float32)
    large = max_exact + (np.log(nf / np.float32(max_exact)) / np.float32(math.log(REL_MAX_DIST / max_exact))
                         * np.float32(REL_BUCKETS - max_exact)).astype(np.int32)
    large = np.minimum(large, REL_BUCKETS - 1)
    bucket = np.where(n < max_exact, n, large)
    starts = []
    for k in range(REL_BUCKETS):
        hit = np.nonzero(bucket >= k)[0]
        starts.append(int(hit[0]) if hit.size else int(n[-1]) + 1)
    return starts


_BUCKET_START = _bucket_starts()


def _bias_kernel(rb_ref, bp_ref, bs_ref, bf_ref):
    def bias_of(n, h):
        b = jnp.full(n.shape, rb_ref[REL_BUCKETS - 1, h], F32)
        for k in range(REL_BUCKETS - 2, -1, -1):
            b = jnp.where(n < _BUCKET_START[k + 1], rb_ref[k, h], b)
        return b

    s_i = lax.broadcasted_iota(I32, (KV_CHUNK, Q_TILE), 0)
    t_i = lax.broadcasted_iota(I32, (KV_CHUNK, Q_TILE), 1)
    for w in range(4):
        n = jnp.maximum(w * Q_TILE + t_i - s_i, 0)
        for h in range(DSA_HEADS):
            bp_ref[w, h] = bias_of(n, h)
    t_s = lax.broadcasted_iota(I32, (SUBLANES, PAGE), 0)
    u_s = lax.broadcasted_iota(I32, (SUBLANES, PAGE), 1)
    for w in range(2):
        n = jnp.maximum((1 - w) * PAGE + t_s - u_s, 0)
        for h in range(DSA_HEADS):
            bs_ref[w, h] = bias_of(n, h)
    for h in range(DSA_HEADS):
        bf_ref[h] = jnp.full((SUBLANES, PAGE), rb_ref[REL_BUCKETS - 1, h], F32)


def _bias_tables(rel_bias):
    return pl.pallas_call(
        _bias_kernel,
        out_shape=(jax.ShapeDtypeStruct((4, DSA_HEADS, KV_CHUNK, Q_TILE), F32),
                   jax.ShapeDtypeStruct((2, DSA_HEADS, SUBLANES, PAGE), F32),
                   jax.ShapeDtypeStruct((DSA_HEADS, SUBLANES, PAGE), F32)),
        in_specs=[pl.BlockSpec(memory_space=pltpu.SMEM)],
        name="bias_tables",
    )(rel_bias)


def _inproj_kernel(x_ref, ga_ref, w1_ref, gq_ref, wuq_ref, gkv_ref, wuk_ref, ca_ref, sa_ref, cb_ref, sb_ref,
                   ckv_ref, krope_ref, kb_ref, vb_ref, ki_ref,
                   kcat_ref, kbbf_ref, vbbf_ref, kibf_ref, qcat_ref, qb_ref, qi_ref, wi_ref):
    xn = _rms(x_ref[...], ga_ref[...]).astype(BF16)

    def proj(lo, hi):
        return _dot(xn, w1_ref[:, lo:hi])

    ca, sa, cb, sb = ca_ref[...], sa_ref[...], cb_ref[...], sb_ref[...]

    ckv = _rms(proj(_C_CKV, _C_QB), gkv_ref[...])
    ckv_ref[...] = ckv
    kcat_ref[:, 0:KV_LORA] = ckv.astype(BF16)
    kr = proj(_C_KR, _C_KRP) * ca + proj(_C_KRP, _C_KI) * sa
    krope_ref[...] = kr.T[:MLA_ROPE]
    kcat_ref[:, KV_LORA:KCAT] = kr.astype(BF16)
    ki = proj(_C_KI, _C_KIP) * ca + proj(_C_KIP, _C_WI) * sa
    ki_ref[...] = ki.T[:IDX_DIM]
    kibf_ref[...] = ki[:, :IDX_DIM].astype(BF16)
    kb = proj(_C_KB, _C_VB)
    kb_ref[...] = kb.T
    kbbf_ref[...] = kb.astype(BF16)
    vb = proj(_C_VB, _C_QI)
    vb_ref[...] = vb.T
    vbbf_ref[...] = vb.astype(BF16)
    wi_ref[...] = proj(_C_WI, _C_END)[:, :IDX_HEADS] * IDX_W_SCALE

    qb = proj(_C_QB, _C_KB)
    for h in range(DSA_HEADS):
        qb_ref[h] = qb[:, h * DSA_HEAD_DIM:(h + 1) * DSA_HEAD_DIM].astype(BF16)
    qi = proj(_C_QI, _C_QIP)
    qip = proj(_C_QIP, _C_KR)
    for s in range(4):
        slab = qi[:, s * LANES:(s + 1) * LANES] * cb + qip[:, s * LANES:(s + 1) * LANES] * sb
        qi_ref[2 * s] = slab[:, :IDX_DIM].astype(BF16)
        qi_ref[2 * s + 1] = slab[:, IDX_DIM:].astype(BF16)

    cq = _rms(proj(_C_CQ, _C_CKV), gq_ref[...]).astype(BF16)
    n_nope = MLA_HEADS * MLA_NOPE
    n_pad = MLA_HEADS * LANES
    nope = _dot(cq, wuq_ref[:, 0:n_nope]).astype(BF16)
    for p in range(MLA_HEADS // 2):
        ql = _dot(nope[:, p * LANES:(p + 1) * LANES], wuk_ref[p])
        qcat_ref[2 * p, :, 0:KV_LORA] = ql[:, :KV_LORA].astype(BF16)
        qcat_ref[2 * p + 1, :, 0:KV_LORA] = ql[:, KV_LORA:].astype(BF16)
    for h in range(MLA_HEADS):
        lo = n_nope + h * LANES
        qr = _dot(cq, wuq_ref[:, lo:lo + LANES]) * ca + _dot(cq, wuq_ref[:, lo + n_pad:lo + n_pad + LANES]) * sa
        qcat_ref[h, :, KV_LORA:KCAT] = qr.astype(BF16)


def _inproj(x2d, tabs, seq, wts, tm):
    n = x2d.shape[0]
    tab_blocks = seq // tm
    n_seq = n // seq
    const2 = lambda i: (0, 0)
    const3 = lambda i: (0, 0, 0)
    row = lambda i: (i, 0)
    tab = lambda i: (i % tab_blocks, 0)
    hm = lambda i: (0, i, 0)
    col = lambda i: (i // tab_blocks, 0, i % tab_blocks)
    t_shape = lambda width: jax.ShapeDtypeStruct((n_seq, width, seq), F32)
    t_spec = lambda width: pl.BlockSpec((None, width, tm), col)
    in_specs = [
        pl.BlockSpec((tm, D_MODEL), row),
        pl.BlockSpec((1, D_MODEL), const2),
        pl.BlockSpec((D_MODEL, _C_END), const2),
        pl.BlockSpec((1, Q_LORA), const2),
        pl.BlockSpec(wts["wuq"].shape, const2),
        pl.BlockSpec((1, KV_LORA), const2),
        pl.BlockSpec(wts["wuk"].shape, const3),
    ] + [pl.BlockSpec((tm, LANES), tab)] * 4
    out_shape = (
        jax.ShapeDtypeStruct((n, KV_LORA), F32), t_shape(MLA_ROPE), t_shape(LANES), t_shape(LANES), t_shape(IDX_DIM),
        jax.ShapeDtypeStruct((n, KCAT), BF16), jax.ShapeDtypeStruct((n, LANES), BF16),
        jax.ShapeDtypeStruct((n, LANES), BF16), jax.ShapeDtypeStruct((n, IDX_DIM), BF16),
        jax.ShapeDtypeStruct((MLA_HEADS, n, KCAT), BF16),
        jax.ShapeDtypeStruct((DSA_HEADS, n, DSA_HEAD_DIM), BF16),
        jax.ShapeDtypeStruct((IDX_HEADS, n, IDX_DIM), BF16),
        jax.ShapeDtypeStruct((n, IDX_HEADS), F32),
    )
    out_specs = (
        pl.BlockSpec((tm, KV_LORA), row), t_spec(MLA_ROPE), t_spec(LANES), t_spec(LANES), t_spec(IDX_DIM),
        pl.BlockSpec((tm, KCAT), row), pl.BlockSpec((tm, LANES), row),
        pl.BlockSpec((tm, LANES), row), pl.BlockSpec((tm, IDX_DIM), row),
        pl.BlockSpec((MLA_HEADS, tm, KCAT), hm),
        pl.BlockSpec((DSA_HEADS, tm, DSA_HEAD_DIM), hm),
        pl.BlockSpec((IDX_HEADS, tm, IDX_DIM), hm),
        pl.BlockSpec((tm, IDX_HEADS), row),
    )
    return pl.pallas_call(
        _inproj_kernel, grid=(n // tm,), in_specs=in_specs, out_specs=out_specs, out_shape=out_shape,
        compiler_params=_cparams(1), name="inproj",
    )(x2d, wts["g_attn"], wts["w1"], wts["g_q"], wts["wuq"], wts["g_kv"], wts["wuk"], *tabs)


def _rep(x, width):
    k = width // LANES
    return x if k == 1 else jnp.concatenate([x] * k, axis=1)


def _mla_prompt_kernel(q_ref, k_ref, wuv_ref, o_ref, m_scr, l_scr, acc_scr):
    j = pl.program_id(1)
    rows = MLA_HEADS * Q_TILE
    q = q_ref[...].reshape(rows, KCAT)
    m_scr[...] = jnp.full(m_scr.shape, NEG, F32)
    l_scr[...] = jnp.zeros(l_scr.shape, F32)
    acc_scr[...] = jnp.zeros(acc_scr.shape, F32)
    n_full = (j * Q_TILE) // MLA_CHUNK

    def body(c, masked):
        k = k_ref[pl.ds(pl.multiple_of(c * MLA_CHUNK, MLA_CHUNK), MLA_CHUNK), :]
        s = _dot_nt(q, k) * MLA_SCALE
        if masked:
            t_row = j * Q_TILE + lax.broadcasted_iota(I32, (rows, MLA_CHUNK), 0) % Q_TILE
            u_col = lax.broadcasted_iota(I32, (rows, MLA_CHUNK), 1)
            s = jnp.where(c * MLA_CHUNK + u_col <= t_row, s, NEG)
        m_prev = m_scr[...]
        m_new = jnp.maximum(m_prev, jnp.max(s, axis=1, keepdims=True))
        alpha = jnp.exp(m_prev - m_new)
        p = jnp.exp(s - _rep(m_new, MLA_CHUNK))
        l_scr[...] = alpha * l_scr[...] + jnp.sum(p, axis=1, keepdims=True)
        acc_scr[...] = acc_scr[...] * _rep(alpha, KV_LORA) + _dot(p.astype(BF16), k[:, :KV_LORA])
        m_scr[...] = m_new

    lax.fori_loop(0, n_full, lambda c, carry: (body(c, False), carry)[1], 0)
    body(n_full, True)
    o_lat = (acc_scr[...] / _rep(l_scr[...], KV_LORA)).astype(BF16)
    for h in range(MLA_HEADS):
        o = _dot(o_lat[h * Q_TILE:(h + 1) * Q_TILE], wuv_ref[h])
        o_ref[:, h * MLA_V:(h + 1) * MLA_V] = o.astype(BF16)


def _mla_prompt(qcat, kcat, wuv, b, s):
    nq = s // Q_TILE
    rows = MLA_HEADS * Q_TILE
    return pl.pallas_call(
        _mla_prompt_kernel, grid=(b, nq),
        in_specs=[pl.BlockSpec((MLA_HEADS, Q_TILE, KCAT), lambda bi, j: (0, bi * nq + j, 0)),
                  pl.BlockSpec((None, s, KCAT), lambda bi, j: (bi, 0, 0)),
                  pl.BlockSpec(wuv.shape, lambda bi, j: (0, 0, 0))],
        out_specs=pl.BlockSpec((Q_TILE, MLA_HEADS * MLA_V), lambda bi, j: (bi * nq + j, 0)),
        out_shape=jax.ShapeDtypeStruct((b * s, MLA_HEADS * MLA_V), BF16),
        scratch_shapes=[pltpu.VMEM((rows, LANES), F32), pltpu.VMEM((rows, LANES), F32),
                        pltpu.VMEM((rows, KV_LORA), F32)],
        compiler_params=_cparams(2), name="mla_prompt",
    )(qcat, kcat.reshape(b, s, KCAT), wuv)


def _topk_threshold(key_scr, n_chunks, topk, n_keys_pow2_bits):
    lanes = key_scr.shape[1]
    sub = KV_CHUNK // SUBLANES

    def count(pred_fn):
        def body(c, acc):
            off = pl.multiple_of(c * KV_CHUNK, KV_CHUNK)
            k = key_scr[pl.ds(off, KV_CHUNK), :]
            hit = pred_fn(k, c).astype(I32)
            return acc + jnp.sum(hit.reshape(sub, SUBLANES, lanes), axis=0)

        acc = lax.fori_loop(0, n_chunks, body, jnp.zeros((SUBLANES, lanes), I32))
        return jnp.sum(acc, axis=0, keepdims=True)

    def bit_body(i, res):
        cand = res | jnp.left_shift(jnp.int32(1), 31 - i)
        cs = cand ^ INT_MIN
        cnt = count(lambda k, c: k >= cs)
        return jnp.where(cnt >= topk, cand, res)

    res = lax.fori_loop(0, 32, bit_body, jnp.zeros((1, lanes), I32))
    tau = res ^ INT_MIN
    cnt_gt = count(lambda k, c: k > tau)
    cnt_eq = count(lambda k, c: k == tau)
    need = topk - cnt_gt
    row0 = lax.broadcasted_iota(I32, (KV_CHUNK, lanes), 0)
    big = jnp.int32(2 ** 30)

    def cut_search():
        def cbody(i, cur):
            cand = cur | jnp.left_shift(jnp.int32(1), n_keys_pow2_bits - 1 - i)
            f = count(lambda k, c: jnp.where(k == tau, row0 + c * KV_CHUNK, big) < cand)
            return jnp.where(f < need, cand, cur)

        return lax.fori_loop(0, n_keys_pow2_bits, cbody, jnp.zeros((1, lanes), I32))

    cut = lax.cond(jnp.max(cnt_eq - need) > 0, cut_search, lambda: jnp.full((1, lanes), big, I32))
    return tau, cut


def _dsa_prompt_kernel(qi_ref, wt_ref, ki_ref, qb_ref, kb_ref, vt_ref, bias_ref, o_ref,
                       key_scr, mb_scr, tc_scr, m_scr, l_scr, acc_scr, *, topk, idx_bits):
    j = pl.program_id(1)
    n_chunks = (j * Q_TILE) // KV_CHUNK + 1
    t_row = j * Q_TILE + lax.broadcasted_iota(I32, (KV_CHUNK, Q_TILE), 1)
    s_loc = lax.broadcasted_iota(I32, (KV_CHUNK, Q_TILE), 0)
    qi = qi_ref[...].reshape(IDX_HEADS * Q_TILE, IDX_DIM)
    wt = wt_ref[...]

    def score_body(c, carry):
        off = pl.multiple_of(c * KV_CHUNK, KV_CHUNK)
        a = _dot_nt(ki_ref[pl.ds(off, KV_CHUNK), :], qi)
        sc = jnp.zeros((KV_CHUNK, Q_TILE), F32)
        for h in range(IDX_HEADS):
            sc = sc + wt[h:h + 1, :] * jnp.maximum(a[:, h * Q_TILE:(h + 1) * Q_TILE], 0.0)
        key = jnp.where(off + s_loc <= t_row, _sort_key(sc), INT_MIN)
        key_scr[pl.ds(off, KV_CHUNK), :] = key
        return carry

    lax.fori_loop(0, n_chunks, score_body, 0)

    @pl.when((j + 1) * Q_TILE <= topk)
    def _():
        tc_scr[0:1, :] = jnp.full((1, Q_TILE), INT_MIN, I32)
        tc_scr[1:2, :] = jnp.full((1, Q_TILE), -1, I32)

    @pl.when((j + 1) * Q_TILE > topk)
    def _():
        tau, cut = _topk_threshold(key_scr, n_chunks, topk, idx_bits)
        tc_scr[0:1, :] = tau
        tc_scr[1:2, :] = cut

    tau = tc_scr[0:1, :]
    cut = tc_scr[1:2, :]

    def mask_body(c, carry):
        off = pl.multiple_of(c * KV_CHUNK, KV_CHUNK)
        k = key_scr[pl.ds(off, KV_CHUNK), :]
        spos = off + s_loc
        v = jnp.where(k > tau, 0.0, jnp.where(k == tau, jnp.where(spos <= cut, 0.0, NEG), NEG))
        mb_scr[pl.ds(off, KV_CHUNK), :] = jnp.where(spos <= t_row, v, NEG)
        return carry

    lax.fori_loop(0, n_chunks, mask_body, 0)

    m_scr[...] = jnp.full(m_scr.shape, NEG, F32)
    l_scr[...] = jnp.zeros(l_scr.shape, F32)
    acc_scr[...] = jnp.zeros(acc_scr.shape, F32)
    qb = qb_ref[...]

    def att_body(c, carry):
        off = pl.multiple_of(c * KV_CHUNK, KV_CHUNK)
        kb = kb_ref[pl.ds(off, KV_CHUNK), :]
        mb = mb_scr[pl.ds(off, KV_CHUNK), :]
        bidx = jnp.minimum((j * Q_TILE - c * KV_CHUNK) // Q_TILE, 3)
        for g in range(DSA_KV_HEADS):
            kg = kb[:, g * DSA_HEAD_DIM:(g + 1) * DSA_HEAD_DIM]
            qg = qb[g * DSA_REP:(g + 1) * DSA_REP].reshape(DSA_REP * Q_TILE, DSA_HEAD_DIM)
            lg4 = _dot_nt(kg, qg) * DSA_SCALE
            vg = vt_ref[c, g * DSA_HEAD_DIM:(g + 1) * DSA_HEAD_DIM, :]
            for r in range(DSA_REP):
                h = g * DSA_REP + r
                lg = lg4[:, r * Q_TILE:(r + 1) * Q_TILE] + bias_ref[bidx, h] + mb
                m_prev = m_scr[h:h + 1, :]
                m_new = jnp.maximum(m_prev, jnp.max(lg, axis=0, keepdims=True))
                alpha = jnp.exp(m_prev - m_new)
                p = jnp.exp(lg - m_new)
                l_scr[h:h + 1, :] = alpha * l_scr[h:h + 1, :] + jnp.sum(p, axis=0, keepdims=True)
                rs = slice(h * DSA_HEAD_DIM, (h + 1) * DSA_HEAD_DIM)
                acc_scr[rs, :] = alpha * acc_scr[rs, :] + _dot(vg, p.astype(BF16))
                m_scr[h:h + 1, :] = m_new
        return carry

    lax.fori_loop(0, n_chunks, att_body, 0)
    inv = 1.0 / l_scr[...]
    parts = [acc_scr[h * DSA_HEAD_DIM:(h + 1) * DSA_HEAD_DIM, :] * inv[h:h + 1, :] for h in range(DSA_HEADS)]
    o_ref[...] = jnp.concatenate(parts, axis=0).T.astype(BF16)


def _dsa_prompt(qi_hm, wi_t, ki_bf, qb_hm, kb_bf, v_t, bias_p, b, s):
    nq = s // Q_TILE
    topk = min(IDX_TOPK_MAX, s // 4)
    idx_bits = max(1, int(math.ceil(math.log2(s))))
    width = DSA_HEADS * DSA_HEAD_DIM
    kern = functools.partial(_dsa_prompt_kernel, topk=topk, idx_bits=idx_bits)
    return pl.pallas_call(
        kern, grid=(b, nq),
        in_specs=[pl.BlockSpec((IDX_HEADS, Q_TILE, IDX_DIM), lambda bi, j: (0, bi * nq + j, 0)),
                  pl.BlockSpec((IDX_HEADS, Q_TILE), lambda bi, j: (0, bi * nq + j)),
                  pl.BlockSpec((None, s, IDX_DIM), lambda bi, j: (bi, 0, 0)),
                  pl.BlockSpec((DSA_HEADS, Q_TILE, DSA_HEAD_DIM), lambda bi, j: (0, bi * nq + j, 0)),
                  pl.BlockSpec((None, s, LANES), lambda bi, j: (bi, 0, 0)),
                  pl.BlockSpec((None, s // KV_CHUNK, LANES, KV_CHUNK), lambda bi, j: (bi, 0, 0, 0)),
                  pl.BlockSpec(bias_p.shape, lambda bi, j: (0, 0, 0, 0))],
        out_specs=pl.BlockSpec((Q_TILE, width), lambda bi, j: (bi * nq + j, 0)),
        out_shape=jax.ShapeDtypeStruct((b * s, width), BF16),
        scratch_shapes=[pltpu.VMEM((s, Q_TILE), I32), pltpu.VMEM((s, Q_TILE), F32), pltpu.VMEM((SUBLANES, Q_TILE), I32),
                        pltpu.VMEM((DSA_HEADS, Q_TILE), F32), pltpu.VMEM((DSA_HEADS, Q_TILE), F32),
                        pltpu.VMEM((width, Q_TILE), F32)],
        compiler_params=_cparams(2), name="dsa_prompt",
    )(qi_hm, wi_t, ki_bf.reshape(b, s, IDX_DIM), qb_hm, kb_bf.reshape(b, s, LANES), v_t, bias_p)


def _mxu_tiles(pages, axis):
    group = MXU_N // PAGE
    return [pages[i] if len(pages[i:i + group]) == 1 else jnp.concatenate(pages[i:i + group], axis=axis)
            for i in range(0, len(pages), group)]


def _sample1_kernel(pt_ref, ql_ref, qr_ref, qi_ref, wi_ref, ckvn_ref, krn_ref, kin_ref, ckv_hbm, kr_hbm, ki_hbm,
                    olat_ref, sc_ref, scn_ref, ckv_buf, kr_buf, ki_buf, sem, m_scr, l_scr, acc_scr, *, pp, n_pages):
    slot = _page_fetch(pt_ref, (ckv_hbm, kr_hbm, ki_hbm), (ckv_buf, kr_buf, ki_buf), sem, pp, n_pages)
    ckv_pages = [ckv_buf.at[slot, k] for k in range(pp)]
    kr_pages = [kr_buf.at[slot, k] for k in range(pp)]
    ki_pages = [ki_buf.at[slot, k] for k in range(pp)]
    j = pl.program_id(1)
    last = pl.num_programs(1) - 1
    rows = ql_ref.shape[0]
    n_tok = rows // MLA_HEADS

    @pl.when(j == 0)
    def _():
        m_scr[...] = jnp.full(m_scr.shape, NEG, F32)
        l_scr[...] = jnp.zeros(l_scr.shape, F32)
        acc_scr[...] = jnp.zeros(acc_scr.shape, F32)

    ql, qr, qi, wi = ql_ref[...], qr_ref[...], qi_ref[...], wi_ref[...]

    def attend(kcs, krs, mask):
        kcs, krs = _mxu_tiles(kcs, axis=0), _mxu_tiles(krs, axis=1)
        s = jnp.concatenate([_dot_nt(ql, kc) + _dot(qr, kr) for kc, kr in zip(kcs, krs)], axis=1) * MLA_SCALE
        if mask is not None:
            s = jnp.where(mask, s, NEG)
        m_prev = m_scr[...]
        m_new = jnp.maximum(m_prev, jnp.max(s, axis=1, keepdims=True))
        alpha = jnp.exp(m_prev - m_new)
        p = jnp.exp(s - _rep(m_new, s.shape[1])).astype(BF16)
        l_scr[...] = alpha * l_scr[...] + jnp.sum(p.astype(F32), axis=1, keepdims=True)
        pv, off = None, 0
        for kc in kcs:
            d = _dot(p[:, off:off + kc.shape[0]], kc)
            pv = d if pv is None else pv + d
            off += kc.shape[0]
        acc_scr[...] = acc_scr[...] * _rep(alpha, KV_LORA) + pv
        m_scr[...] = m_new

    def index(kidx_t):
        a = jnp.maximum(_dot(qi, kidx_t), 0.0) * wi
        return jnp.sum(a.reshape(n_tok, IDX_HEADS, kidx_t.shape[1]), axis=1)

    attend([r[...].astype(BF16) for r in ckv_pages], [r[...].astype(BF16) for r in kr_pages], None)
    off = 0
    for kt in _mxu_tiles([r[...].astype(BF16) for r in ki_pages], axis=1):
        sc_ref[:, off:off + kt.shape[1]] = index(kt)
        off += kt.shape[1]

    @pl.when(j == last)
    def _():
        t_r = lax.broadcasted_iota(I32, (rows, PAGE), 0) % n_tok
        u_c = lax.broadcasted_iota(I32, (rows, PAGE), 1)
        attend([ckvn_ref[...]], [krn_ref[...]], u_c <= t_r)
        t4 = lax.broadcasted_iota(I32, (n_tok, PAGE), 0)
        u4 = lax.broadcasted_iota(I32, (n_tok, PAGE), 1)
        scn_ref[...] = jnp.where(u4 <= t4, index(kin_ref[...]), -jnp.inf)
        olat_ref[...] = acc_scr[...] / _rep(l_scr[...], KV_LORA)


def _sample1(pt_flat, ql, qr, qi, wi, ckvn, krn_t, kin_t, c_ckv, c_kr_t, c_ki_t, n_pages, pp):
    bs, rows = ql.shape[0], ql.shape[1]
    n_tok = rows // MLA_HEADS
    per_b = lambda tail: pl.BlockSpec((None,) + tail, lambda bi, j, pt: (bi,) + (0,) * len(tail))
    in_specs = [per_b((rows, KV_LORA)), per_b((rows, MLA_ROPE)), per_b((rows, IDX_DIM)), per_b((rows, 1)),
                per_b((PAGE, KV_LORA)), per_b((MLA_ROPE, PAGE)), per_b((IDX_DIM, PAGE))]
    in_specs += [pl.BlockSpec(memory_space=pl.ANY)] * 3
    out_specs = (per_b((rows, KV_LORA)),
                 pl.BlockSpec((None, n_tok, pp * PAGE), lambda bi, j, pt: (bi, 0, j)),
                 per_b((n_tok, PAGE)))
    out_shape = (jax.ShapeDtypeStruct((bs, rows, KV_LORA), F32),
                 jax.ShapeDtypeStruct((bs, n_tok, n_pages * PAGE), F32),
                 jax.ShapeDtypeStruct((bs, n_tok, PAGE), F32))
    grid_spec = pltpu.PrefetchScalarGridSpec(
        num_scalar_prefetch=1, grid=(bs, n_pages // pp), in_specs=in_specs, out_specs=out_specs,
        scratch_shapes=[pltpu.VMEM((2, pp, PAGE, KV_LORA), F32), pltpu.VMEM((2, pp, MLA_ROPE, PAGE), F32),
                        pltpu.VMEM((2, pp, IDX_DIM, PAGE), F32), pltpu.SemaphoreType.DMA((2, 3)),
                        pltpu.VMEM((rows, LANES), F32), pltpu.VMEM((rows, LANES), F32),
                        pltpu.VMEM((rows, KV_LORA), F32)])
    return pl.pallas_call(
        functools.partial(_sample1_kernel, pp=pp, n_pages=n_pages), grid_spec=grid_spec, out_shape=out_shape,
        compiler_params=_cparams(2), name="sample_mla_index",
    )(pt_flat, ql, qr, qi, wi, ckvn, krn_t, kin_t, c_ckv, c_kr_t, c_ki_t)


def _mla_out_kernel(o_ref, wuv_ref, out_ref):
    for h in range(MLA_HEADS):
        out_ref[:, h * MLA_V:(h + 1) * MLA_V] = _dot(o_ref[h], wuv_ref[h]).astype(BF16)


def _mla_out(olat_hm, wuv):
    n = olat_hm.shape[1]
    return pl.pallas_call(
        _mla_out_kernel, out_shape=jax.ShapeDtypeStruct((n, MLA_HEADS * MLA_V), BF16), name="sample_mla_out",
    )(olat_hm, wuv)


def _sample_select_kernel(sc_ref, tau_ref, cut_ref, key_scr, *, topk, idx_bits):
    n_chunks = sc_ref.shape[0] // KV_CHUNK

    def kbody(c, carry):
        off = pl.multiple_of(c * KV_CHUNK, KV_CHUNK)
        key_scr[pl.ds(off, KV_CHUNK), :] = _sort_key(sc_ref[pl.ds(off, KV_CHUNK), :])
        return carry

    lax.fori_loop(0, n_chunks, kbody, 0)
    tau, cut = _topk_threshold(key_scr, n_chunks, topk, idx_bits)
    tau_ref[...] = tau
    cut_ref[...] = cut


def _sample_select(sc_t, topk):
    kp, ns = sc_t.shape
    lt = min(LANES, ns)
    idx_bits = max(1, int(math.ceil(math.log2(kp))))
    kern = functools.partial(_sample_select_kernel, topk=topk, idx_bits=idx_bits)
    return pl.pallas_call(
        kern, grid=(ns // lt,),
        in_specs=[pl.BlockSpec((kp, lt), lambda i: (0, i))],
        out_specs=(pl.BlockSpec((1, lt), lambda i: (0, i)), pl.BlockSpec((1, lt), lambda i: (0, i))),
        out_shape=(jax.ShapeDtypeStruct((1, ns), I32), jax.ShapeDtypeStruct((1, ns), I32)),
        scratch_shapes=[pltpu.VMEM((kp, lt), I32)],
        compiler_params=_cparams(1), name="sample_select",
    )(sc_t)


def _page_fetch(pt_ref, caches, bufs, sem, pp, n_pages):
    bi, j, nj = pl.program_id(0), pl.program_id(1), pl.num_programs(1)
    step = bi * nj + j
    slot = step % 2

    def copies(first_page, sl):
        out = []
        for k in range(pp):
            page = 0 if first_page is None else pt_ref[first_page + k]
            for i, (cache, buf) in enumerate(zip(caches, bufs)):
                out.append(pltpu.make_async_copy(cache.at[page], buf.at[sl, k], sem.at[sl, i]))
        return out

    @pl.when(step == 0)
    def _():
        for c in copies(0, 0):
            c.start()

    @pl.when(step + 1 < pl.num_programs(0) * nj)
    def _():
        nxt = step + 1
        for c in copies((nxt // nj) * n_pages + (nxt % nj) * pp, 1 - slot):
            c.start()

    for c in copies(None, slot):
        c.wait()
    return slot


def _sample3_kernel(pt_ref, qb_ref, sc_ref, scn_ref, tau_ref, cut_ref, kn_ref, vn_ref, bs_ref, bf_ref, ck_hbm, cv_hbm,
                    o_ref, kbuf, vbuf, sem, m_scr, l_scr, acc_scr, *, pp, past):
    slot = _page_fetch(pt_ref, (ck_hbm, cv_hbm), (kbuf, vbuf), sem, pp, past // PAGE)
    k_pages = [kbuf.at[slot, k] for k in range(pp)]
    v_pages = [vbuf.at[slot, k] for k in range(pp)]
    j = pl.program_id(1)
    last = pl.num_programs(1) - 1
    n_tok = sc_ref.shape[0]

    @pl.when(j == 0)
    def _():
        m_scr[...] = jnp.full(m_scr.shape, NEG, F32)
        l_scr[...] = jnp.zeros(l_scr.shape, F32)
        acc_scr[...] = jnp.zeros(acc_scr.shape, F32)

    tau, cut = tau_ref[...], cut_ref[...]

    def mask_bias(sc, base):
        k = _sort_key(sc)
        spos = base + lax.broadcasted_iota(I32, sc.shape, 1)
        return jnp.where(k > tau, 0.0, jnp.where(k == tau, jnp.where(spos <= cut, 0.0, NEG), NEG))

    def attend(kts, vts, mb4, biases):
        mb = jnp.concatenate([mb4] * DSA_REP, axis=0)
        kts, vts = _mxu_tiles(kts, axis=1), _mxu_tiles(vts, axis=1)
        for g in range(DSA_KV_HEADS):
            rs = slice(g * DSA_HEAD_DIM, (g + 1) * DSA_HEAD_DIM)
            lg = jnp.concatenate([_dot(qb_ref[g], kt[rs, :]) for kt in kts], axis=1) * DSA_SCALE
            lg = lg + jnp.concatenate([b[g] for b in biases], axis=1) + mb
            m_prev = m_scr[g]
            m_new = jnp.maximum(m_prev, jnp.max(lg, axis=1, keepdims=True))
            alpha = jnp.exp(m_prev - m_new)
            p = jnp.exp(lg - _rep(m_new, lg.shape[1])).astype(BF16)
            l_scr[g] = alpha * l_scr[g] + jnp.sum(p.astype(F32), axis=1, keepdims=True)
            pv, off = None, 0
            for vt in vts:
                d = _dot_nt(p[:, off:off + vt.shape[1]], vt[rs, :])
                pv = d if pv is None else pv + d
                off += vt.shape[1]
            acc_scr[g] = acc_scr[g] * alpha[:, :DSA_HEAD_DIM] + pv
            m_scr[g] = m_new

    far = bf_ref[...]
    biases = [far] * (pp - 1) + [jnp.where(j == last, bs_ref[0], far)]
    attend([r[...].astype(BF16) for r in k_pages], [r[...].astype(BF16) for r in v_pages],
           mask_bias(sc_ref[...], j * (pp * PAGE)), biases)

    @pl.when(j == last)
    def _():
        attend([kn_ref[...]], [vn_ref[...]], mask_bias(scn_ref[...], past), [bs_ref[1]])
        for g in range(DSA_KV_HEADS):
            o_ref[g] = acc_scr[g] / l_scr[g][:, :DSA_HEAD_DIM]


def _sample3(pt_flat, qb, sc, scn, tau, cut, kn_t, vn_t, bias_s, bias_f, c_kt, c_vt, n_pages, pp):
    bs, n_tok = sc.shape[0], sc.shape[1]
    rows = DSA_REP * n_tok
    per_b = lambda tail: pl.BlockSpec((None,) + tail, lambda bi, j, pt: (bi,) + (0,) * len(tail))
    const = lambda shape: pl.BlockSpec(shape, lambda bi, j, pt: (0,) * len(shape))
    in_specs = [per_b((DSA_KV_HEADS, rows, DSA_HEAD_DIM)),
                pl.BlockSpec((None, n_tok, pp * PAGE), lambda bi, j, pt: (bi, 0, j)),
                per_b((n_tok, PAGE)), per_b((n_tok, 1)), per_b((n_tok, 1)),
                per_b((LANES, PAGE)), per_b((LANES, PAGE)),
                const(bias_s.shape), const(bias_f.shape),
                pl.BlockSpec(memory_space=pl.ANY), pl.BlockSpec(memory_space=pl.ANY)]
    grid_spec = pltpu.PrefetchScalarGridSpec(
        num_scalar_prefetch=1, grid=(bs, n_pages // pp), in_specs=in_specs,
        out_specs=per_b((DSA_KV_HEADS, rows, DSA_HEAD_DIM)),
        scratch_shapes=[pltpu.VMEM((2, pp, LANES, PAGE), F32), pltpu.VMEM((2, pp, LANES, PAGE), F32),
                        pltpu.SemaphoreType.DMA((2, 2)),
                        pltpu.VMEM((DSA_KV_HEADS, rows, LANES), F32), pltpu.VMEM((DSA_KV_HEADS, rows, LANES), F32),
                        pltpu.VMEM((DSA_KV_HEADS, rows, DSA_HEAD_DIM), F32)])
    kern = functools.partial(_sample3_kernel, pp=pp, past=n_pages * PAGE)
    return pl.pallas_call(
        kern, grid_spec=grid_spec,
        out_shape=jax.ShapeDtypeStruct((bs, DSA_KV_HEADS, rows, DSA_HEAD_DIM), F32),
        compiler_params=_cparams(2), name="sample_dsa",
    )(pt_flat, qb, sc, scn, tau, cut, kn_t, vn_t, bias_s, bias_f, c_kt, c_vt)


_N_EXTRACT = PEER_TOPK + 1


def _extract_top(cur, n):
    vals = []
    for _ in range(n):
        m = jnp.max(cur, axis=0, keepdims=True)
        vals.append(m)
        cur = jnp.where(cur == m, -jnp.inf, cur)
    return vals


def _peer_prep_kernel(x_ref, mla_ref, dsa_ref, wo_ref, g_ref, wpq_ref, keys_ref,
                      h_ref, xnt_ref, thr_ref, a_ref, s2_ref, b_ref):
    half = wo_ref.shape[0] // 2
    h = x_ref[...] + _dot(mla_ref[...], wo_ref[0:half, :]) + _dot(dsa_ref[...], wo_ref[half:, :])
    h_ref[...] = h
    xnt = _rms(h, g_ref[...]).T.astype(BF16)
    xnt_ref[...] = xnt
    tc = xnt.shape[1]
    r8 = lax.broadcasted_iota(I32, (SUBLANES, LANES), 0)
    for hh in range(PEER_HEADS):
        for p, ref in ((0, thr_ref), (1, s2_ref)):
            hp = hh * 2 + p
            qt = _dot(wpq_ref[hp * PEER_HALF:(hp + 1) * PEER_HALF, :], xnt)
            ref[hh] = _dot(keys_ref[hp], qt.astype(BF16))
        for lt in range(tc // LANES):
            ls = slice(lt * LANES, (lt + 1) * LANES)
            s1, s2 = thr_ref[hh, :, ls], s2_ref[hh, :, ls]
            sv1 = _extract_top(s1, _N_EXTRACT)
            sv2 = _extract_top(s2, _N_EXTRACT)
            sv2_16 = jnp.concatenate(sv2[:PEER_TOPK], axis=0)
            sv2_8 = sv2_16[:8]
            blocks = [sv1[0] + sv2_16]
            for r1 in range(1, 8):
                blocks.append(jnp.where(r8 < PEER_TOPK // (r1 + 1), sv1[r1] + sv2_8, -jnp.inf))
            blocks.append(jnp.concatenate(sv1[8:PEER_TOPK], axis=0) + sv2[0])
            extra = jnp.where(r8 == 0, sv1[0] + sv2[PEER_TOPK],
                              jnp.where(r8 == 1, sv1[PEER_TOPK] + sv2[0], -jnp.inf))
            blocks.append(extra)
            cand = _extract_top(jnp.concatenate(blocks, axis=0), _N_EXTRACT)
            m0 = sv1[0] + sv2[0]
            z = jnp.zeros_like(m0)
            for r in range(PEER_TOPK):
                z = z + jnp.exp(cand[r] - m0)
            c16, c17 = cand[PEER_TOPK - 1], cand[PEER_TOPK]
            tau = jnp.where(c17 == -jnp.inf, c16, 0.5 * (c16 + c17))
            thr_ref[hh, :, ls] = tau - s1
            a_ref[hh, :, ls] = jnp.exp(s1 - sv1[0]) / z * 0.5
            b_ref[hh, :, ls] = jnp.exp(s2 - sv2[0])


def _peer_prep(x2d, mla, dsa, wts, tc):
    n = x2d.shape[0]
    row = lambda i: (i, 0)
    const2 = lambda i: (0, 0)
    col3 = lambda i: (0, 0, i)
    gate_shape = jax.ShapeDtypeStruct((PEER_HEADS, PEER_NKEYS, n), F32)
    gate_spec = pl.BlockSpec((PEER_HEADS, PEER_NKEYS, tc), col3)
    mix = mla.shape[1]
    return pl.pallas_call(
        _peer_prep_kernel, grid=(n // tc,),
        in_specs=[pl.BlockSpec((tc, D_MODEL), row), pl.BlockSpec((tc, mix), row), pl.BlockSpec((tc, mix), row),
                  pl.BlockSpec(wts["w_out"].shape, const2), pl.BlockSpec((1, D_MODEL), const2),
                  pl.BlockSpec(wts["wpq_t"].shape, const2), pl.BlockSpec(wts["peer_keys"].shape, lambda i: (0, 0, 0))],
        out_specs=(pl.BlockSpec((tc, D_MODEL), row), pl.BlockSpec((D_MODEL, tc), lambda i: (0, i)),
                   gate_spec, gate_spec, gate_spec, gate_spec),
        out_shape=(jax.ShapeDtypeStruct((n, D_MODEL), F32), jax.ShapeDtypeStruct((D_MODEL, n), BF16),
                   gate_shape, gate_shape, gate_shape, gate_shape),
        compiler_params=_cparams(1), name="peer_prep",
    )(x2d, mla, dsa, wts["w_out"], wts["g_ffn"], wts["wpq_t"], wts["peer_keys"])


def _gelu_x2(x):
    return x * (1.0 + lax.erf(x * np.float32(math.sqrt(0.5))))


def _peer_chain_kernel(xnt_ref, thr_ref, a_ref, s2_ref, b_ref, u_ref, vt_ref, h_ref, gf_ref, y_ref, acc_scr, *, ni):
    e = pl.program_id(1)

    @pl.when(e == 0)
    def _():
        acc_scr[...] = jnp.zeros(acc_scr.shape, F32)

    tc = acc_scr.shape[1]
    tw = min(tc, MXU_N)
    per_slice = MXU_N // PEER_NKEYS
    chains = [(ks, k) for ks in range(ni // per_slice) for k in range(tc // tw)]
    n_r = D_MODEL // MXU_N

    def gates(ks, k):
        out = []
        for i2 in range(per_slice):
            i1 = e * ni + ks * per_slice + i2
            thr_rows = [thr_ref[hh, pl.ds(i1, 1), :] for hh in range(PEER_HEADS)]
            a_rows = [a_ref[hh, pl.ds(i1, 1), :] for hh in range(PEER_HEADS)]
            for lt in range(tw // LANES):
                ls = slice(k * tw + lt * LANES, k * tw + (lt + 1) * LANES)
                gate = None
                for hh in range(PEER_HEADS):
                    term = jnp.where(s2_ref[hh, :, ls] >= thr_rows[hh][:, ls], b_ref[hh, :, ls], 0.0)
                    term = term * a_rows[hh][:, ls]
                    gate = term if gate is None else gate + term
                out.append(gate)
        return out

    for ks, k in chains:
        es = slice(ks * MXU_N, (ks + 1) * MXU_N)
        cs = slice(k * tw, (k + 1) * tw)
        act = _gelu_x2(_dot(u_ref[es, :], xnt_ref[:, cs]))
        g_cur = gates(ks, k)
        n_lt = tw // LANES
        parts = []
        for i2 in range(per_slice):
            tiles = [(g_cur[i2 * n_lt + lt] * act[i2 * PEER_NKEYS:(i2 + 1) * PEER_NKEYS,
                                                   lt * LANES:(lt + 1) * LANES]).astype(BF16) for lt in range(n_lt)]
            parts.append(tiles[0] if n_lt == 1 else jnp.concatenate(tiles, axis=1))
        w = jnp.concatenate(parts, axis=0)
        for r in range(n_r):
            rs = slice(r * MXU_N, (r + 1) * MXU_N)
            acc_scr[rs, cs] += _dot(vt_ref[rs, es], w)

    @pl.when(e == pl.num_programs(1) - 1)
    def _():
        y_ref[...] = _rms(acc_scr[...].T + h_ref[...], gf_ref[...])


def _peer_chain(xnt, thr, a, s2, b, h, wts, tc, ni):
    n = h.shape[0]
    eb = ni * PEER_NKEYS
    gate_spec = pl.BlockSpec((PEER_HEADS, PEER_NKEYS, tc), lambda i, e: (0, 0, i))
    return pl.pallas_call(
        functools.partial(_peer_chain_kernel, ni=ni), grid=(n // tc, PEER_EXPERTS // eb),
        in_specs=[pl.BlockSpec((D_MODEL, tc), lambda i, e: (0, i)), gate_spec, gate_spec, gate_spec, gate_spec,
                  pl.BlockSpec((eb, D_MODEL), lambda i, e: (e, 0)), pl.BlockSpec((D_MODEL, eb), lambda i, e: (0, e)),
                  pl.BlockSpec((tc, D_MODEL), lambda i, e: (i, 0)), pl.BlockSpec((1, D_MODEL), lambda i, e: (0, 0))],
        out_specs=pl.BlockSpec((tc, D_MODEL), lambda i, e: (i, 0)),
        out_shape=jax.ShapeDtypeStruct((n, D_MODEL), F32),
        scratch_shapes=[pltpu.VMEM((D_MODEL, tc), F32)],
        compiler_params=_cparams(2), name="peer_main",
    )(xnt, thr, a, s2, b, wts["peer_u"], wts["peer_vt"], h, wts["g_final"])


def _peer(x2d, mla, dsa, wts):
    n = x2d.shape[0]
    tc = min(512, n)
    h, xnt, thr, a, s2, b = _peer_prep(x2d, mla, dsa, wts, tc)
    return _peer_chain(xnt, thr, a, s2, b, h, wts, tc, ni=16)


def _pad_cols(w, n):
    return jnp.pad(w, ((0, 0), (0, n - w.shape[1])))


def _swap_halves(w):
    half = w.shape[-1] // 2
    return jnp.concatenate([w[..., half:], w[..., :half]], axis=-1)


def _prep_weights(g_attn, w_in, g_q, w_uq, g_kv, w_uk, w_uv, w_out, g_ffn, w_pq, peer_keys, peer_u, peer_v, g_final):
    w_cq, w_ckv, w_kr = w_in[:, 0:384], w_in[:, 384:640], w_in[:, 640:672]
    w_qb, w_kb, w_vb = w_in[:, 672:1184], w_in[:, 1184:1312], w_in[:, 1312:1440]
    w_qi, w_ki, w_wi = w_in[:, 1440:1952], w_in[:, 1952:2016], w_in[:, 2016:2024]
    qi3 = w_qi.reshape(D_MODEL, IDX_HEADS, IDX_DIM)
    qi_partner = jnp.concatenate([_swap_halves(qi3[..., :IDX_ROPE]), jnp.zeros_like(qi3[..., IDX_ROPE:])], axis=-1)
    w1 = jnp.concatenate([
        w_cq, w_ckv, w_qb, w_kb, w_vb, w_qi, qi_partner.reshape(D_MODEL, IDX_HEADS * IDX_DIM),
        _pad_cols(w_kr, LANES), _pad_cols(_swap_halves(w_kr), LANES),
        _pad_cols(w_ki, LANES), _pad_cols(_swap_halves(w_ki[:, :IDX_ROPE]), LANES),
        _pad_cols(w_wi, LANES)], axis=1).astype(BF16)
    uq3 = w_uq.reshape(Q_LORA, MLA_HEADS, MLA_NOPE + MLA_ROPE)
    rope3 = uq3[..., MLA_NOPE:]
    pad3 = lambda w: jnp.pad(w, ((0, 0), (0, 0), (0, LANES - MLA_ROPE))).reshape(Q_LORA, MLA_HEADS * LANES)
    wuq = jnp.concatenate([uq3[..., :MLA_NOPE].reshape(Q_LORA, MLA_HEADS * MLA_NOPE),
                           pad3(rope3), pad3(_swap_halves(rope3))], axis=1).astype(BF16)
    ukt = jnp.transpose(w_uk, (1, 2, 0))
    zero = jnp.zeros((MLA_NOPE, KV_LORA), F32)
    wuk = jnp.stack([jnp.concatenate([jnp.concatenate([ukt[2 * p], zero], axis=1),
                                      jnp.concatenate([zero, ukt[2 * p + 1]], axis=1)], axis=0)
                     for p in range(MLA_HEADS // 2)]).astype(BF16)
    return dict(
        g_attn=g_attn.reshape(1, -1), w1=w1, g_q=g_q.reshape(1, -1), wuq=wuq, g_kv=g_kv.reshape(1, -1), wuk=wuk,
        wuv=jnp.transpose(w_uv, (1, 0, 2)).astype(BF16), w_out=w_out.astype(BF16), g_ffn=g_ffn.reshape(1, -1),
        wpq_t=w_pq.T.astype(BF16),
        peer_keys=peer_keys.reshape(PEER_HEADS * 2, PEER_NKEYS, PEER_HALF).astype(BF16),
        peer_u=peer_u.astype(BF16), peer_vt=peer_v.T.astype(BF16), g_final=g_final.reshape(1, -1))


def _rope_tables(pos):
    half = MLA_ROPE // 2
    inv = ROPE_THETA ** (-jnp.arange(half, dtype=F32) / half)
    ang = pos.astype(F32)[:, None] * inv
    cos, sin = jnp.cos(ang), jnp.sin(ang)
    c32 = jnp.concatenate([cos, cos], axis=1)
    s32 = jnp.concatenate([-sin, sin], axis=1)
    n = pos.shape[0]
    one, zero = jnp.ones((n, 32), F32), jnp.zeros((n, 32), F32)
    ca = jnp.concatenate([c32, one, one, one], axis=1)
    sa = jnp.concatenate([s32, zero, zero, zero], axis=1)
    cb = jnp.concatenate([c32, one, c32, one], axis=1)
    sb = jnp.concatenate([s32, zero, s32, zero], axis=1)
    return ca, sa, cb, sb


def _pick_tile(n, choices):
    for c in choices:
        if n % c == 0:
            return c
    raise ValueError(f"no tile in {choices} divides {n}")


def kernel(x_prompt, x_sample, cache_ckv, cache_krope, cache_k, cache_v, cache_kidx, page_table, rel_bias, g_attn,
           w_in, g_q, w_uq, g_kv, w_uk, w_uv, w_out, g_ffn, w_pq, peer_keys, peer_u, peer_v, g_final):
    assert g_attn.shape[0] == 1, "single-layer kernel"
    b, s, d = x_prompt.shape
    bs, ts, _ = x_sample.shape
    n_pages = page_table.shape[1]
    past = n_pages * PAGE
    assert s % MLA_CHUNK == 0 and s % KV_CHUNK == 0 and ts <= 8 and (bs * ts) % LANES == 0
    wts = _prep_weights(g_attn[0], w_in[0], g_q[0], w_uq[0], g_kv[0], w_uk[0], w_uv[0], w_out[0], g_ffn[0],
                        w_pq[0], peer_keys[0], peer_u[0], peer_v[0], g_final)
    bias_p, bias_s, bias_f = _bias_tables(rel_bias)

    xp = x_prompt.reshape(b * s, d)
    tm = _pick_tile(s, (512, 256))
    (ckv_p, kr_p, kb_p, vb_p, ki_p, kcat, kb_bf, vb_bf, ki_bf, qcat, qb_hm, qi_hm, wi_p) = _inproj(
        xp, _rope_tables(jnp.arange(s)), s, wts, tm)
    mla_p = _mla_prompt(qcat, kcat, wts["wuv"], b, s)
    v_t = jnp.transpose(vb_bf.reshape(b, s // KV_CHUNK, KV_CHUNK, LANES), (0, 1, 3, 2))
    dsa_p = _dsa_prompt(qi_hm, wi_p.T, ki_bf, qb_hm, kb_bf, v_t, bias_p, b, s)
    y_p = _peer(xp, mla_p, dsa_p, wts)

    ns = bs * ts
    xs = x_sample.reshape(ns, d)
    pos_s = past + jnp.tile(jnp.arange(ts), bs)
    (ckv_s, kr_s, kb_s, vb_s, ki_s, kcat_s, kb_sbf, vb_sbf, ki_sbf, qcat_s, qb_shm, qi_shm, wi_s) = _inproj(
        xs, _rope_tables(pos_s), ns, wts, ns)
    pt_flat = page_table.reshape(-1).astype(I32)
    pp = _pick_tile(n_pages, (64, 32, 16, 8, 4, 2, 1))
    q5 = qcat_s.reshape(MLA_HEADS, bs, ts, KCAT).transpose(1, 0, 2, 3).reshape(bs, MLA_HEADS * ts, KCAT)
    qi_s = qi_shm.reshape(IDX_HEADS, bs, ts, IDX_DIM).transpose(1, 2, 0, 3).reshape(bs, ts * IDX_HEADS, IDX_DIM)
    wi_col = wi_s.reshape(bs, ts * IDX_HEADS, 1)
    pad_new = lambda a: jnp.pad(a.reshape(bs, ts, a.shape[-1]), ((0, 0), (0, PAGE - ts), (0, 0)))
    pad_new_t = lambda a: jnp.swapaxes(pad_new(a), 1, 2)
    olat, sc_past, sc_new = _sample1(
        pt_flat, q5[..., :KV_LORA], q5[..., KV_LORA:KV_LORA + MLA_ROPE], qi_s, wi_col,
        pad_new(kcat_s[:, :KV_LORA]), pad_new_t(kcat_s[:, KV_LORA:KV_LORA + MLA_ROPE]), pad_new_t(ki_sbf),
        cache_ckv[0], jnp.swapaxes(cache_krope[0], 1, 2), jnp.swapaxes(cache_kidx[0], 1, 2), n_pages, pp)
    olat_hm = olat.reshape(bs, MLA_HEADS, ts, KV_LORA).transpose(1, 0, 2, 3).reshape(MLA_HEADS, ns, KV_LORA)
    mla_s = _mla_out(olat_hm.astype(BF16), wts["wuv"])
    topk_s = min(IDX_TOPK_MAX, (past + ts) // 4)
    kp = -(-(past + PAGE) // KV_CHUNK) * KV_CHUNK
    sc_all = jnp.concatenate([sc_past, sc_new], axis=2).reshape(ns, past + PAGE)
    sc_t = jnp.pad(sc_all, ((0, 0), (0, kp - past - PAGE)), constant_values=-jnp.inf).T
    tau_s, cut_s = _sample_select(sc_t, topk_s)
    qb_s = qb_shm.reshape(DSA_KV_HEADS, DSA_REP, bs, ts, DSA_HEAD_DIM).transpose(2, 0, 1, 3, 4)
    qb_s = qb_s.reshape(bs, DSA_KV_HEADS, DSA_REP * ts, DSA_HEAD_DIM)
    bias_s4 = bias_s[:, :, :ts, :].reshape(2, DSA_KV_HEADS, DSA_REP * ts, PAGE)
    bias_f4 = bias_f[:, :ts, :].reshape(DSA_KV_HEADS, DSA_REP * ts, PAGE)
    n_pool = cache_k.shape[1]
    page_t = lambda c: jnp.transpose(c[0], (0, 2, 3, 1)).reshape(n_pool, LANES, PAGE)
    o_s = _sample3(pt_flat, qb_s, sc_past, sc_new, tau_s.reshape(bs, ts, 1), cut_s.reshape(bs, ts, 1),
                   pad_new_t(kb_sbf), pad_new_t(vb_sbf), bias_s4, bias_f4,
                   page_t(cache_k), page_t(cache_v), n_pages, pp)
    dsa_s = o_s.reshape(bs, DSA_KV_HEADS, DSA_REP, ts, DSA_HEAD_DIM).transpose(0, 3, 1, 2, 4)
    dsa_s = dsa_s.reshape(ns, DSA_HEADS * DSA_HEAD_DIM).astype(BF16)
    y_s = _peer(xs, mla_s, dsa_s, wts)

    def rows(a_t, nb, nt):
        return jnp.swapaxes(a_t, 1, 2).reshape(1, nb, nt, a_t.shape[1])

    kv5 = lambda a: a.reshape(a.shape[:3] + (DSA_KV_HEADS, DSA_HEAD_DIM))
    return (y_p.reshape(b, s, d), y_s.reshape(bs, ts, d),
            ckv_p.reshape(1, b, s, KV_LORA), rows(kr_p, b, s), kv5(rows(kb_p, b, s)), kv5(rows(vb_p, b, s)),
            rows(ki_p, b, s),
            ckv_s.reshape(1, bs, ts, KV_LORA), rows(kr_s, bs, ts), kv5(rows(kb_s, bs, ts)), kv5(rows(vb_s, bs, ts)),
            rows(ki_s, bs, ts))
```

```python
import functools
import math

import jax
import jax.numpy as jnp
import numpy as np
from jax import lax
from jax.experimental import pallas as pl
from jax.experimental.pallas import tpu as pltpu

F32 = jnp.float32
BF16 = jnp.bfloat16
I32 = jnp.int32

D_MODEL = 1024
PAGE = 128
MLA_HEADS = 8
MLA_NOPE = 64
MLA_ROPE = 32
MLA_V = 64
Q_LORA = 384
KV_LORA = 256
MLA_SCALE = (MLA_NOPE + MLA_ROPE) ** -0.5
DSA_HEADS = 8
DSA_KV_HEADS = 2
DSA_REP = DSA_HEADS // DSA_KV_HEADS
DSA_HEAD_DIM = 64
DSA_SCALE = DSA_HEAD_DIM ** -0.5
IDX_HEADS = 8
IDX_DIM = 64
IDX_ROPE = 32
IDX_TOPK_MAX = 256
IDX_W_SCALE = (IDX_HEADS * IDX_DIM) ** -0.5
REL_BUCKETS = 32
REL_MAX_DIST = 128
PEER_HEADS = 8
PEER_NKEYS = 128
PEER_EXPERTS = PEER_NKEYS * PEER_NKEYS
PEER_HALF = 128
PEER_TOPK = 16
ROPE_THETA = 10000.0
NORM_EPS = 1e-6

LANES = 128
SUBLANES = 8
MXU_N = 256
NEG = -1e30
INT_MIN = -(2 ** 31)
KCAT = KV_LORA + LANES
KV_CHUNK = 256
MLA_CHUNK = 512
Q_TILE = 128

_C_CQ, _C_CKV, _C_QB, _C_KB, _C_VB = 0, 384, 640, 1152, 1280
_C_QI, _C_QIP, _C_KR, _C_KRP, _C_KI, _C_KIP, _C_WI, _C_END = 1408, 1920, 2432, 2560, 2688, 2816, 2944, 3072
_VMEM_LIMIT = 56 * 1024 * 1024


def _cparams(n_axes):
    return pltpu.CompilerParams(dimension_semantics=("arbitrary",) * n_axes, vmem_limit_bytes=_VMEM_LIMIT)


def _dot(a, b):
    return jnp.dot(a, b, preferred_element_type=F32)


def _dot_nt(a, b):
    return lax.dot_general(a, b, (((1,), (1,)), ((), ())), preferred_element_type=F32)


def _rms(x, g):
    return x * lax.rsqrt(jnp.mean(x * x, axis=-1, keepdims=True) + NORM_EPS) * g


def _sort_key(x):
    x = jnp.where(x == 0.0, 0.0, x)
    bits = pltpu.bitcast(x, I32)
    return bits ^ ((bits >> 31) & 0x7FFFFFFF)


def _bucket_starts():
    max_exact = REL_BUCKETS // 2
    n = np.arange(0, 2 * REL_MAX_DIST, dtype=np.int64)
    nf = np.maximum(n, 1).astype(np.float32)
    large = max_exact + (np.log(nf / np.float32(max_exact)) / np.float32(math.log(REL_MAX_DIST / max_exact))
                         * np.float32(REL_BUCKETS - max_exact)).astype(np.int32)
    large = np.minimum(large, REL_BUCKETS - 1)
    bucket = np.where(n < max_exact, n, large)
    starts = []
    for k in range(REL_BUCKETS):
        hit = np.nonzero(bucket >= k)[0]
        starts.append(int(hit[0]) if hit.size else int(n[-1]) + 1)
    return starts


_BUCKET_START = _bucket_starts()


def _bias_kernel(rb_ref, bp_ref, bs_ref, bf_ref):
    def bias_of(n, h):
        b = jnp.full(n.shape, rb_ref[REL_BUCKETS - 1, h], F32)
        for k in range(REL_BUCKETS - 2, -1, -1):
            b = jnp.where(n < _BUCKET_START[k + 1], rb_ref[k, h], b)
        return b

    s_i = lax.broadcasted_iota(I32, (KV_CHUNK, Q_TILE), 0)
    t_i = lax.broadcasted_iota(I32, (KV_CHUNK, Q_TILE), 1)
    for w in range(4):
        n = jnp.maximum(w * Q_TILE + t_i - s_i, 0)
        for h in range(DSA_HEADS):
            bp_ref[w, h] = bias_of(n, h)
    t_s = lax.broadcasted_iota(I32, (SUBLANES, PAGE), 0)
    u_s = lax.broadcasted_iota(I32, (SUBLANES, PAGE), 1)
    for w in range(2):
        n = jnp.maximum((1 - w) * PAGE + t_s - u_s, 0)
        for h in range(DSA_HEADS):
            bs_ref[w, h] = bias_of(n, h)
    for h in range(DSA_HEADS):
        bf_ref[h] = jnp.full((SUBLANES, PAGE), rb_ref[REL_BUCKETS - 1, h], F32)


def _bias_tables(rel_bias):
    return pl.pallas_call(
        _bias_kernel,
        out_shape=(jax.ShapeDtypeStruct((4, DSA_HEADS, KV_CHUNK, Q_TILE), F32),
                   jax.ShapeDtypeStruct((2, DSA_HEADS, SUBLANES, PAGE), F32),
                   jax.ShapeDtypeStruct((DSA_HEADS, SUBLANES, PAGE), F32)),
        in_specs=[pl.BlockSpec(memory_space=pltpu.SMEM)],
        name="bias_tables",
    )(rel_bias)


def _inproj_kernel(x_ref, ga_ref, w1_ref, gq_ref, wuq_ref, gkv_ref, wuk_ref, ca_ref, sa_ref, cb_ref, sb_ref,
                   ckv_ref, krope_ref, kb_ref, vb_ref, ki_ref,
                   kcat_ref, kbbf_ref, vbbf_ref, kibf_ref, qcat_ref, qb_ref, qi_ref, wi_ref):
    xn = _rms(x_ref[...], ga_ref[...]).astype(BF16)

    def proj(lo, hi):
        return _dot(xn, w1_ref[:, lo:hi])

    ca, sa, cb, sb = ca_ref[...], sa_ref[...], cb_ref[...], sb_ref[...]

    ckv = _rms(proj(_C_CKV, _C_QB), gkv_ref[...])
    ckv_ref[...] = ckv
    kcat_ref[:, 0:KV_LORA] = ckv.astype(BF16)
    kr = proj(_C_KR, _C_KRP) * ca + proj(_C_KRP, _C_KI) * sa
    krope_ref[...] = kr.T[:MLA_ROPE]
    kcat_ref[:, KV_LORA:KCAT] = kr.astype(BF16)
    ki = proj(_C_KI, _C_KIP) * ca + proj(_C_KIP, _C_WI) * sa
    ki_ref[...] = ki.T[:IDX_DIM]
    kibf_ref[...] = ki[:, :IDX_DIM].astype(BF16)
    kb = proj(_C_KB, _C_VB)
    kb_ref[...] = kb.T
    kbbf_ref[...] = kb.astype(BF16)
    vb = proj(_C_VB, _C_QI)
    vb_ref[...] = vb.T
    vbbf_ref[...] = vb.astype(BF16)
    wi_ref[...] = proj(_C_WI, _C_END)[:, :IDX_HEADS] * IDX_W_SCALE

    qb = proj(_C_QB, _C_KB)
    for h in range(DSA_HEADS):
        qb_ref[h] = qb[:, h * DSA_HEAD_DIM:(h + 1) * DSA_HEAD_DIM].astype(BF16)
    qi = proj(_C_QI, _C_QIP)
    qip = proj(_C_QIP, _C_KR)
    for s in range(4):
        slab = qi[:, s * LANES:(s + 1) * LANES] * cb + qip[:, s * LANES:(s + 1) * LANES] * sb
        qi_ref[2 * s] = slab[:, :IDX_DIM].astype(BF16)
        qi_ref[2 * s + 1] = slab[:, IDX_DIM:].astype(BF16)

    cq = _rms(proj(_C_CQ, _C_CKV), gq_ref[...]).astype(BF16)
    n_nope = MLA_HEADS * MLA_NOPE
    n_pad = MLA_HEADS * LANES
    nope = _dot(cq, wuq_ref[:, 0:n_nope]).astype(BF16)
    for p in range(MLA_HEADS // 2):
        ql = _dot(nope[:, p * LANES:(p + 1) * LANES], wuk_ref[p])
        qcat_ref[2 * p, :, 0:KV_LORA] = ql[:, :KV_LORA].astype(BF16)
        qcat_ref[2 * p + 1, :, 0:KV_LORA] = ql[:, KV_LORA:].astype(BF16)
    for h in range(MLA_HEADS):
        lo = n_nope + h * LANES
        qr = _dot(cq, wuq_ref[:, lo:lo + LANES]) * ca + _dot(cq, wuq_ref[:, lo + n_pad:lo + n_pad + LANES]) * sa
        qcat_ref[h, :, KV_LORA:KCAT] = qr.astype(BF16)


def _inproj(x2d, tabs, seq, wts, tm):
    n = x2d.shape[0]
    tab_blocks = seq // tm
    n_seq = n // seq
    const2 = lambda i: (0, 0)
    const3 = lambda i: (0, 0, 0)
    row = lambda i: (i, 0)
    tab = lambda i: (i % tab_blocks, 0)
    hm = lambda i: (0, i, 0)
    col = lambda i: (i // tab_blocks, 0, i % tab_blocks)
    t_shape = lambda width: jax.ShapeDtypeStruct((n_seq, width, seq), F32)
    t_spec = lambda width: pl.BlockSpec((None, width, tm), col)
    in_specs = [
        pl.BlockSpec((tm, D_MODEL), row),
        pl.BlockSpec((1, D_MODEL), const2),
        pl.BlockSpec((D_MODEL, _C_END), const2),
        pl.BlockSpec((1, Q_LORA), const2),
        pl.BlockSpec(wts["wuq"].shape, const2),
        pl.BlockSpec((1, KV_LORA), const2),
        pl.BlockSpec(wts["wuk"].shape, const3),
    ] + [pl.BlockSpec((tm, LANES), tab)] * 4
    out_shape = (
        jax.ShapeDtypeStruct((n, KV_LORA), F32), t_shape(MLA_ROPE), t_shape(LANES), t_shape(LANES), t_shape(IDX_DIM),
        jax.ShapeDtypeStruct((n, KCAT), BF16), jax.ShapeDtypeStruct((n, LANES), BF16),
        jax.ShapeDtypeStruct((n, LANES), BF16), jax.ShapeDtypeStruct((n, IDX_DIM), BF16),
        jax.ShapeDtypeStruct((MLA_HEADS, n, KCAT), BF16),
        jax.ShapeDtypeStruct((DSA_HEADS, n, DSA_HEAD_DIM), BF16),
        jax.ShapeDtypeStruct((IDX_HEADS, n, IDX_DIM), BF16),
        jax.ShapeDtypeStruct((n, IDX_HEADS), F32),
    )
    out_specs = (
        pl.BlockSpec((tm, KV_LORA), row), t_spec(MLA_ROPE), t_spec(LANES), t_spec(LANES), t_spec(IDX_DIM),
        pl.BlockSpec((tm, KCAT), row), pl.BlockSpec((tm, LANES), row),
        pl.BlockSpec((tm, LANES), row), pl.BlockSpec((tm, IDX_DIM), row),
        pl.BlockSpec((MLA_HEADS, tm, KCAT), hm),
        pl.BlockSpec((DSA_HEADS, tm, DSA_HEAD_DIM), hm),
        pl.BlockSpec((IDX_HEADS, tm, IDX_DIM), hm),
        pl.BlockSpec((tm, IDX_HEADS), row),
    )
    return pl.pallas_call(
        _inproj_kernel, grid=(n // tm,), in_specs=in_specs, out_specs=out_specs, out_shape=out_shape,
        compiler_params=_cparams(1), name="inproj",
    )(x2d, wts["g_attn"], wts["w1"], wts["g_q"], wts["wuq"], wts["g_kv"], wts["wuk"], *tabs)


def _rep(x, width):
    k = width // LANES
    return x if k == 1 else jnp.concatenate([x] * k, axis=1)


def _mla_prompt_kernel(q_ref, k_ref, wuv_ref, o_ref, m_scr, l_scr, acc_scr):
    j = pl.program_id(1)
    rows = MLA_HEADS * Q_TILE
    q = q_ref[...].reshape(rows, KCAT)
    m_scr[...] = jnp.full(m_scr.shape, NEG, F32)
    l_scr[...] = jnp.zeros(l_scr.shape, F32)
    acc_scr[...] = jnp.zeros(acc_scr.shape, F32)
    n_full = (j * Q_TILE) // MLA_CHUNK

    n_part = 2
    part = rows // n_part

    def body(c, masked):
        k = k_ref[pl.ds(pl.multiple_of(c * MLA_CHUNK, MLA_CHUNK), MLA_CHUNK), :]
        for pt in range(n_part):
            rs = slice(pt * part, (pt + 1) * part)
            s = _dot_nt(q[rs], k) * MLA_SCALE
            if masked:
                t_row = j * Q_TILE + lax.broadcasted_iota(I32, (part, MLA_CHUNK), 0) % Q_TILE
                u_col = lax.broadcasted_iota(I32, (part, MLA_CHUNK), 1)
                s = jnp.where(c * MLA_CHUNK + u_col <= t_row, s, NEG)
            m_prev = m_scr[rs, :]
            m_new = jnp.maximum(m_prev, jnp.max(s, axis=1, keepdims=True))
            alpha = jnp.exp(m_prev - m_new)
            p = jnp.exp(s - _rep(m_new, MLA_CHUNK))
            l_scr[rs, :] = alpha * l_scr[rs, :] + jnp.sum(p, axis=1, keepdims=True)
            acc_scr[rs, :] = acc_scr[rs, :] * _rep(alpha, KV_LORA) + _dot(p.astype(BF16), k[:, :KV_LORA])
            m_scr[rs, :] = m_new

    lax.fori_loop(0, n_full, lambda c, carry: (body(c, False), carry)[1], 0)
    body(n_full, True)
    o_lat = (acc_scr[...] / _rep(l_scr[...], KV_LORA)).astype(BF16)
    for h in range(MLA_HEADS):
        o = _dot(o_lat[h * Q_TILE:(h + 1) * Q_TILE], wuv_ref[h])
        o_ref[:, h * MLA_V:(h + 1) * MLA_V] = o.astype(BF16)


def _mla_prompt(qcat, kcat, wuv, b, s):
    nq = s // Q_TILE
    rows = MLA_HEADS * Q_TILE
    return pl.pallas_call(
        _mla_prompt_kernel, grid=(b, nq),
        in_specs=[pl.BlockSpec((MLA_HEADS, Q_TILE, KCAT), lambda bi, j: (0, bi * nq + j, 0)),
                  pl.BlockSpec((None, s, KCAT), lambda bi, j: (bi, 0, 0)),
                  pl.BlockSpec(wuv.shape, lambda bi, j: (0, 0, 0))],
        out_specs=pl.BlockSpec((Q_TILE, MLA_HEADS * MLA_V), lambda bi, j: (bi * nq + j, 0)),
        out_shape=jax.ShapeDtypeStruct((b * s, MLA_HEADS * MLA_V), BF16),
        scratch_shapes=[pltpu.VMEM((rows, LANES), F32), pltpu.VMEM((rows, LANES), F32),
                        pltpu.VMEM((rows, KV_LORA), F32)],
        compiler_params=_cparams(2), name="mla_prompt",
    )(qcat, kcat.reshape(b, s, KCAT), wuv)


def _topk_threshold(key_scr, n_chunks, topk, n_keys_pow2_bits):
    lanes = key_scr.shape[1]
    sub = KV_CHUNK // SUBLANES

    def count(pred_fn):
        def body(c, acc):
            off = pl.multiple_of(c * KV_CHUNK, KV_CHUNK)
            k = key_scr[pl.ds(off, KV_CHUNK), :]
            hit = pred_fn(k, c).astype(I32)
            return acc + jnp.sum(hit.reshape(sub, SUBLANES, lanes), axis=0)

        acc = lax.fori_loop(0, n_chunks, body, jnp.zeros((SUBLANES, lanes), I32))
        return jnp.sum(acc, axis=0, keepdims=True)

    def bit_body(i, res):
        cand = res | jnp.left_shift(jnp.int32(1), 31 - i)
        cs = cand ^ INT_MIN
        cnt = count(lambda k, c: k >= cs)
        return jnp.where(cnt >= topk, cand, res)

    res = lax.fori_loop(0, 32, bit_body, jnp.zeros((1, lanes), I32))
    tau = res ^ INT_MIN
    cnt_gt = count(lambda k, c: k > tau)
    cnt_eq = count(lambda k, c: k == tau)
    need = topk - cnt_gt
    row0 = lax.broadcasted_iota(I32, (KV_CHUNK, lanes), 0)
    big = jnp.int32(2 ** 30)

    def cut_search():
        def cbody(i, cur):
            cand = cur | jnp.left_shift(jnp.int32(1), n_keys_pow2_bits - 1 - i)
            f = count(lambda k, c: jnp.where(k == tau, row0 + c * KV_CHUNK, big) < cand)
            return jnp.where(f < need, cand, cur)

        return lax.fori_loop(0, n_keys_pow2_bits, cbody, jnp.zeros((1, lanes), I32))

    cut = lax.cond(jnp.max(cnt_eq - need) > 0, cut_search, lambda: jnp.full((1, lanes), big, I32))
    return tau, cut


def _dsa_prompt_kernel(qi_ref, wt_ref, ki_ref, qb_ref, kb_ref, vt_ref, bias_ref, o_ref,
                       key_scr, mb_scr, tc_scr, m_scr, l_scr, acc_scr, *, topk, idx_bits):
    j = pl.program_id(1)
    n_chunks = (j * Q_TILE) // KV_CHUNK + 1
    t_row = j * Q_TILE + lax.broadcasted_iota(I32, (KV_CHUNK, Q_TILE), 1)
    s_loc = lax.broadcasted_iota(I32, (KV_CHUNK, Q_TILE), 0)
    qi = qi_ref[...].reshape(IDX_HEADS * Q_TILE, IDX_DIM)
    wt = wt_ref[...]

    def score_body(c, carry):
        off = pl.multiple_of(c * KV_CHUNK, KV_CHUNK)
        a = _dot_nt(ki_ref[pl.ds(off, KV_CHUNK), :], qi)
        sc = jnp.zeros((KV_CHUNK, Q_TILE), F32)
        for h in range(IDX_HEADS):
            sc = sc + wt[h:h + 1, :] * jnp.maximum(a[:, h * Q_TILE:(h + 1) * Q_TILE], 0.0)
        key = jnp.where(off + s_loc <= t_row, _sort_key(sc), INT_MIN)
        key_scr[pl.ds(off, KV_CHUNK), :] = key
        return carry

    lax.fori_loop(0, n_chunks, score_body, 0)

    @pl.when((j + 1) * Q_TILE <= topk)
    def _():
        tc_scr[0:1, :] = jnp.full((1, Q_TILE), INT_MIN, I32)
        tc_scr[1:2, :] = jnp.full((1, Q_TILE), -1, I32)

    @pl.when((j + 1) * Q_TILE > topk)
    def _():
        tau, cut = _topk_threshold(key_scr, n_chunks, topk, idx_bits)
        tc_scr[0:1, :] = tau
        tc_scr[1:2, :] = cut

    tau = tc_scr[0:1, :]
    cut = tc_scr[1:2, :]

    def mask_body(c, carry):
        off = pl.multiple_of(c * KV_CHUNK, KV_CHUNK)
        k = key_scr[pl.ds(off, KV_CHUNK), :]
        spos = off + s_loc
        v = jnp.where(k > tau, 0.0, jnp.where(k == tau, jnp.where(spos <= cut, 0.0, NEG), NEG))
        mb_scr[pl.ds(off, KV_CHUNK), :] = jnp.where(spos <= t_row, v, NEG)
        return carry

    lax.fori_loop(0, n_chunks, mask_body, 0)

    m_scr[...] = jnp.full(m_scr.shape, NEG, F32)
    l_scr[...] = jnp.zeros(l_scr.shape, F32)
    acc_scr[...] = jnp.zeros(acc_scr.shape, F32)
    qb = qb_ref[...]

    def att_body(c, carry):
        off = pl.multiple_of(c * KV_CHUNK, KV_CHUNK)
        kb = kb_ref[pl.ds(off, KV_CHUNK), :]
        mb = mb_scr[pl.ds(off, KV_CHUNK), :]
        bidx = jnp.minimum((j * Q_TILE - c * KV_CHUNK) // Q_TILE, 3)
        for g in range(DSA_KV_HEADS):
            kg = kb[:, g * DSA_HEAD_DIM:(g + 1) * DSA_HEAD_DIM]
            qg = qb[g * DSA_REP:(g + 1) * DSA_REP].reshape(DSA_REP * Q_TILE, DSA_HEAD_DIM)
            lg4 = _dot_nt(kg, qg) * DSA_SCALE
            vg = vt_ref[c, g * DSA_HEAD_DIM:(g + 1) * DSA_HEAD_DIM, :]
            for r in range(DSA_REP):
                h = g * DSA_REP + r
                lg = lg4[:, r * Q_TILE:(r + 1) * Q_TILE] + bias_ref[bidx, h] + mb
                m_prev = m_scr[h:h + 1, :]
                m_new = jnp.maximum(m_prev, jnp.max(lg, axis=0, keepdims=True))
                alpha = jnp.exp(m_prev - m_new)
                p = jnp.exp(lg - m_new)
                l_scr[h:h + 1, :] = alpha * l_scr[h:h + 1, :] + jnp.sum(p, axis=0, keepdims=True)
                rs = slice(h * DSA_HEAD_DIM, (h + 1) * DSA_HEAD_DIM)
                acc_scr[rs, :] = alpha * acc_scr[rs, :] + _dot(vg, p.astype(BF16))
                m_scr[h:h + 1, :] = m_new
        return carry

    lax.fori_loop(0, n_chunks, att_body, 0)
    inv = 1.0 / l_scr[...]
    parts = [acc_scr[h * DSA_HEAD_DIM:(h + 1) * DSA_HEAD_DIM, :] * inv[h:h + 1, :] for h in range(DSA_HEADS)]
    o_ref[...] = jnp.concatenate(parts, axis=0).T.astype(BF16)


def _dsa_prompt(qi_hm, wi_t, ki_bf, qb_hm, kb_bf, v_t, bias_p, b, s):
    nq = s // Q_TILE
    topk = min(IDX_TOPK_MAX, s // 4)
    idx_bits = max(1, int(math.ceil(math.log2(s))))
    width = DSA_HEADS * DSA_HEAD_DIM
    kern = functools.partial(_dsa_prompt_kernel, topk=topk, idx_bits=idx_bits)
    return pl.pallas_call(
        kern, grid=(b, nq),
        in_specs=[pl.BlockSpec((IDX_HEADS, Q_TILE, IDX_DIM), lambda bi, j: (0, bi * nq + j, 0)),
                  pl.BlockSpec((IDX_HEADS, Q_TILE), lambda bi, j: (0, bi * nq + j)),
                  pl.BlockSpec((None, s, IDX_DIM), lambda bi, j: (bi, 0, 0)),
                  pl.BlockSpec((DSA_HEADS, Q_TILE, DSA_HEAD_DIM), lambda bi, j: (0, bi * nq + j, 0)),
                  pl.BlockSpec((None, s, LANES), lambda bi, j: (bi, 0, 0)),
                  pl.BlockSpec((None, s // KV_CHUNK, LANES, KV_CHUNK), lambda bi, j: (bi, 0, 0, 0)),
                  pl.BlockSpec(bias_p.shape, lambda bi, j: (0, 0, 0, 0))],
        out_specs=pl.BlockSpec((Q_TILE, width), lambda bi, j: (bi * nq + j, 0)),
        out_shape=jax.ShapeDtypeStruct((b * s, width), BF16),
        scratch_shapes=[pltpu.VMEM((s, Q_TILE), I32), pltpu.VMEM((s, Q_TILE), F32), pltpu.VMEM((SUBLANES, Q_TILE), I32),
                        pltpu.VMEM((DSA_HEADS, Q_TILE), F32), pltpu.VMEM((DSA_HEADS, Q_TILE), F32),
                        pltpu.VMEM((width, Q_TILE), F32)],
        compiler_params=_cparams(2), name="dsa_prompt",
    )(qi_hm, wi_t, ki_bf.reshape(b, s, IDX_DIM), qb_hm, kb_bf.reshape(b, s, LANES), v_t, bias_p)


def _mxu_tiles(pages, axis):
    group = MXU_N // PAGE
    return [pages[i] if len(pages[i:i + group]) == 1 else jnp.concatenate(pages[i:i + group], axis=axis)
            for i in range(0, len(pages), group)]


def _sample1_kernel(pt_ref, ql_ref, qr_ref, qi_ref, wi_ref, ckvn_ref, krn_ref, kin_ref, ckv_hbm, kr_hbm, ki_hbm,
                    olat_ref, sc_ref, scn_ref, ckv_buf, kr_buf, ki_buf, sem, m_scr, l_scr, acc_scr, *, pp, n_pages):
    slot = _page_fetch(pt_ref, (ckv_hbm, kr_hbm, ki_hbm), (ckv_buf, kr_buf, ki_buf), sem, pp, n_pages)
    ckv_pages = [ckv_buf.at[slot, k] for k in range(pp)]
    kr_pages = [kr_buf.at[slot, k] for k in range(pp)]
    ki_pages = [ki_buf.at[slot, k] for k in range(pp)]
    j = pl.program_id(1)
    last = pl.num_programs(1) - 1
    rows = ql_ref.shape[0]
    n_tok = rows // MLA_HEADS

    @pl.when(j == 0)
    def _():
        m_scr[...] = jnp.full(m_scr.shape, NEG, F32)
        l_scr[...] = jnp.zeros(l_scr.shape, F32)
        acc_scr[...] = jnp.zeros(acc_scr.shape, F32)

    ql, qr, qi, wi = ql_ref[...], qr_ref[...], qi_ref[...], wi_ref[...]

    def attend(kcs, krs, mask):
        kcs, krs = _mxu_tiles(kcs, axis=0), _mxu_tiles(krs, axis=1)
        s = jnp.concatenate([_dot_nt(ql, kc) + _dot(qr, kr) for kc, kr in zip(kcs, krs)], axis=1) * MLA_SCALE
        if mask is not None:
            s = jnp.where(mask, s, NEG)
        m_prev = m_scr[...]
        m_new = jnp.maximum(m_prev, jnp.max(s, axis=1, keepdims=True))
        alpha = jnp.exp(m_prev - m_new)
        p = jnp.exp(s - _rep(m_new, s.shape[1])).astype(BF16)
        l_scr[...] = alpha * l_scr[...] + jnp.sum(p.astype(F32), axis=1, keepdims=True)
        pv, off = None, 0
        for kc in kcs:
            d = _dot(p[:, off:off + kc.shape[0]], kc)
            pv = d if pv is None else pv + d
            off += kc.shape[0]
        acc_scr[...] = acc_scr[...] * _rep(alpha, KV_LORA) + pv
        m_scr[...] = m_new

    def index(kidx_t):
        a = jnp.maximum(_dot(qi, kidx_t), 0.0) * wi
        return jnp.sum(a.reshape(n_tok, IDX_HEADS, kidx_t.shape[1]), axis=1)

    attend([r[...].astype(BF16) for r in ckv_pages], [r[...].astype(BF16) for r in kr_pages], None)
    off = 0
    for kt in _mxu_tiles([r[...].astype(BF16) for r in ki_pages], axis=1):
        sc_ref[:, off:off + kt.shape[1]] = index(kt)
        off += kt.shape[1]

    @pl.when(j == last)
    def _():
        t_r = lax.broadcasted_iota(I32, (rows, PAGE), 0) % n_tok
        u_c = lax.broadcasted_iota(I32, (rows, PAGE), 1)
        attend([ckvn_ref[...]], [krn_ref[...]], u_c <= t_r)
        t4 = lax.broadcasted_iota(I32, (n_tok, PAGE), 0)
        u4 = lax.broadcasted_iota(I32, (n_tok, PAGE), 1)
        scn_ref[...] = jnp.where(u4 <= t4, index(kin_ref[...]), -jnp.inf)
        olat_ref[...] = acc_scr[...] / _rep(l_scr[...], KV_LORA)


def _sample1(pt_flat, ql, qr, qi, wi, ckvn, krn_t, kin_t, c_ckv, c_kr_t, c_ki_t, n_pages, pp):
    bs, rows = ql.shape[0], ql.shape[1]
    n_tok = rows // MLA_HEADS
    per_b = lambda tail: pl.BlockSpec((None,) + tail, lambda bi, j, pt: (bi,) + (0,) * len(tail))
    in_specs = [per_b((rows, KV_LORA)), per_b((rows, MLA_ROPE)), per_b((rows, IDX_DIM)), per_b((rows, 1)),
                per_b((PAGE, KV_LORA)), per_b((MLA_ROPE, PAGE)), per_b((IDX_DIM, PAGE))]
    in_specs += [pl.BlockSpec(memory_space=pl.ANY)] * 3
    out_specs = (per_b((rows, KV_LORA)),
                 pl.BlockSpec((None, n_tok, pp * PAGE), lambda bi, j, pt: (bi, 0, j)),
                 per_b((n_tok, PAGE)))
    out_shape = (jax.ShapeDtypeStruct((bs, rows, KV_LORA), F32),
                 jax.ShapeDtypeStruct((bs, n_tok, n_pages * PAGE), F32),
                 jax.ShapeDtypeStruct((bs, n_tok, PAGE), F32))
    grid_spec = pltpu.PrefetchScalarGridSpec(
        num_scalar_prefetch=1, grid=(bs, n_pages // pp), in_specs=in_specs, out_specs=out_specs,
        scratch_shapes=[pltpu.VMEM((2, pp, PAGE, KV_LORA), F32), pltpu.VMEM((2, pp, MLA_ROPE, PAGE), F32),
                        pltpu.VMEM((2, pp, IDX_DIM, PAGE), F32), pltpu.SemaphoreType.DMA((2, 3)),
                        pltpu.VMEM((rows, LANES), F32), pltpu.VMEM((rows, LANES), F32),
                        pltpu.VMEM((rows, KV_LORA), F32)])
    return pl.pallas_call(
        functools.partial(_sample1_kernel, pp=pp, n_pages=n_pages), grid_spec=grid_spec, out_shape=out_shape,
        compiler_params=_cparams(2), name="sample_mla_index",
    )(pt_flat, ql, qr, qi, wi, ckvn, krn_t, kin_t, c_ckv, c_kr_t, c_ki_t)


def _mla_out_kernel(o_ref, wuv_ref, out_ref):
    for h in range(MLA_HEADS):
        out_ref[:, h * MLA_V:(h + 1) * MLA_V] = _dot(o_ref[h], wuv_ref[h]).astype(BF16)


def _mla_out(olat_hm, wuv):
    n = olat_hm.shape[1]
    return pl.pallas_call(
        _mla_out_kernel, out_shape=jax.ShapeDtypeStruct((n, MLA_HEADS * MLA_V), BF16), name="sample_mla_out",
    )(olat_hm, wuv)


def _sample_select_kernel(sc_ref, tau_ref, cut_ref, key_scr, *, topk, idx_bits):
    n_chunks = sc_ref.shape[0] // KV_CHUNK

    def kbody(c, carry):
        off = pl.multiple_of(c * KV_CHUNK, KV_CHUNK)
        key_scr[pl.ds(off, KV_CHUNK), :] = _sort_key(sc_ref[pl.ds(off, KV_CHUNK), :])
        return carry

    lax.fori_loop(0, n_chunks, kbody, 0)
    tau, cut = _topk_threshold(key_scr, n_chunks, topk, idx_bits)
    tau_ref[...] = tau
    cut_ref[...] = cut


def _sample_select(sc_t, topk):
    kp, ns = sc_t.shape
    lt = min(LANES, ns)
    idx_bits = max(1, int(math.ceil(math.log2(kp))))
    kern = functools.partial(_sample_select_kernel, topk=topk, idx_bits=idx_bits)
    return pl.pallas_call(
        kern, grid=(ns // lt,),
        in_specs=[pl.BlockSpec((kp, lt), lambda i: (0, i))],
        out_specs=(pl.BlockSpec((1, lt), lambda i: (0, i)), pl.BlockSpec((1, lt), lambda i: (0, i))),
        out_shape=(jax.ShapeDtypeStruct((1, ns), I32), jax.ShapeDtypeStruct((1, ns), I32)),
        scratch_shapes=[pltpu.VMEM((kp, lt), I32)],
        compiler_params=_cparams(1), name="sample_select",
    )(sc_t)


def _page_fetch(pt_ref, caches, bufs, sem, pp, n_pages):
    bi, j, nj = pl.program_id(0), pl.program_id(1), pl.num_programs(1)
    step = bi * nj + j
    slot = step % 2

    def copies(first_page, sl):
        out = []
        for k in range(pp):
            page = 0 if first_page is None else pt_ref[first_page + k]
            for i, (cache, buf) in enumerate(zip(caches, bufs)):
                out.append(pltpu.make_async_copy(cache.at[page], buf.at[sl, k], sem.at[sl, i]))
        return out

    @pl.when(step == 0)
    def _():
        for c in copies(0, 0):
            c.start()

    @pl.when(step + 1 < pl.num_programs(0) * nj)
    def _():
        nxt = step + 1
        for c in copies((nxt // nj) * n_pages + (nxt % nj) * pp, 1 - slot):
            c.start()

    for c in copies(None, slot):
        c.wait()
    return slot


def _sample3_kernel(pt_ref, qb_ref, sc_ref, scn_ref, tau_ref, cut_ref, kn_ref, vn_ref, bs_ref, bf_ref, ck_hbm, cv_hbm,
                    o_ref, kbuf, vbuf, sem, m_scr, l_scr, acc_scr, *, pp, past):
    slot = _page_fetch(pt_ref, (ck_hbm, cv_hbm), (kbuf, vbuf), sem, pp, past // PAGE)
    k_pages = [kbuf.at[slot, k] for k in range(pp)]
    v_pages = [vbuf.at[slot, k] for k in range(pp)]
    j = pl.program_id(1)
    last = pl.num_programs(1) - 1
    n_tok = sc_ref.shape[0]

    @pl.when(j == 0)
    def _():
        m_scr[...] = jnp.full(m_scr.shape, NEG, F32)
        l_scr[...] = jnp.zeros(l_scr.shape, F32)
        acc_scr[...] = jnp.zeros(acc_scr.shape, F32)

    tau, cut = tau_ref[...], cut_ref[...]

    def mask_bias(sc, base):
        k = _sort_key(sc)
        spos = base + lax.broadcasted_iota(I32, sc.shape, 1)
        return jnp.where(k > tau, 0.0, jnp.where(k == tau, jnp.where(spos <= cut, 0.0, NEG), NEG))

    def attend(kts, vts, mb4, biases):
        mb = jnp.concatenate([mb4] * DSA_REP, axis=0)
        kts, vts = _mxu_tiles(kts, axis=1), _mxu_tiles(vts, axis=1)
        for g in range(DSA_KV_HEADS):
            rs = slice(g * DSA_HEAD_DIM, (g + 1) * DSA_HEAD_DIM)
            lg = jnp.concatenate([_dot(qb_ref[g], kt[rs, :]) for kt in kts], axis=1) * DSA_SCALE
            lg = lg + jnp.concatenate([b[g] for b in biases], axis=1) + mb
            m_prev = m_scr[g]
            m_new = jnp.maximum(m_prev, jnp.max(lg, axis=1, keepdims=True))
            alpha = jnp.exp(m_prev - m_new)
            p = jnp.exp(lg - _rep(m_new, lg.shape[1])).astype(BF16)
            l_scr[g] = alpha * l_scr[g] + jnp.sum(p.astype(F32), axis=1, keepdims=True)
            pv, off = None, 0
            for vt in vts:
                d = _dot_nt(p[:, off:off + vt.shape[1]], vt[rs, :])
                pv = d if pv is None else pv + d
                off += vt.shape[1]
            acc_scr[g] = acc_scr[g] * alpha[:, :DSA_HEAD_DIM] + pv
            m_scr[g] = m_new

    far = bf_ref[...]
    biases = [far] * (pp - 1) + [jnp.where(j == last, bs_ref[0], far)]
    attend([r[...].astype(BF16) for r in k_pages], [r[...].astype(BF16) for r in v_pages],
           mask_bias(sc_ref[...], j * (pp * PAGE)), biases)

    @pl.when(j == last)
    def _():
        attend([kn_ref[...]], [vn_ref[...]], mask_bias(scn_ref[...], past), [bs_ref[1]])
        for g in range(DSA_KV_HEADS):
            o_ref[g] = acc_scr[g] / l_scr[g][:, :DSA_HEAD_DIM]


def _sample3(pt_flat, qb, sc, scn, tau, cut, kn_t, vn_t, bias_s, bias_f, c_kt, c_vt, n_pages, pp):
    bs, n_tok = sc.shape[0], sc.shape[1]
    rows = DSA_REP * n_tok
    per_b = lambda tail: pl.BlockSpec((None,) + tail, lambda bi, j, pt: (bi,) + (0,) * len(tail))
    const = lambda shape: pl.BlockSpec(shape, lambda bi, j, pt: (0,) * len(shape))
    in_specs = [per_b((DSA_KV_HEADS, rows, DSA_HEAD_DIM)),
                pl.BlockSpec((None, n_tok, pp * PAGE), lambda bi, j, pt: (bi, 0, j)),
                per_b((n_tok, PAGE)), per_b((n_tok, 1)), per_b((n_tok, 1)),
                per_b((LANES, PAGE)), per_b((LANES, PAGE)),
                const(bias_s.shape), const(bias_f.shape),
                pl.BlockSpec(memory_space=pl.ANY), pl.BlockSpec(memory_space=pl.ANY)]
    grid_spec = pltpu.PrefetchScalarGridSpec(
        num_scalar_prefetch=1, grid=(bs, n_pages // pp), in_specs=in_specs,
        out_specs=per_b((DSA_KV_HEADS, rows, DSA_HEAD_DIM)),
        scratch_shapes=[pltpu.VMEM((2, pp, LANES, PAGE), F32), pltpu.VMEM((2, pp, LANES, PAGE), F32),
                        pltpu.SemaphoreType.DMA((2, 2)),
                        pltpu.VMEM((DSA_KV_HEADS, rows, LANES), F32), pltpu.VMEM((DSA_KV_HEADS, rows, LANES), F32),
                        pltpu.VMEM((DSA_KV_HEADS, rows, DSA_HEAD_DIM), F32)])
    kern = functools.partial(_sample3_kernel, pp=pp, past=n_pages * PAGE)
    return pl.pallas_call(
        kern, grid_spec=grid_spec,
        out_shape=jax.ShapeDtypeStruct((bs, DSA_KV_HEADS, rows, DSA_HEAD_DIM), F32),
        compiler_params=_cparams(2), name="sample_dsa",
    )(pt_flat, qb, sc, scn, tau, cut, kn_t, vn_t, bias_s, bias_f, c_kt, c_vt)


_N_EXTRACT = PEER_TOPK + 1


def _extract_top(cur, n):
    vals = []
    for _ in range(n):
        m = jnp.max(cur, axis=0, keepdims=True)
        vals.append(m)
        cur = jnp.where(cur == m, -jnp.inf, cur)
    return vals


def _peer_prep_kernel(x_ref, mla_ref, dsa_ref, wo_ref, g_ref, wpq_ref, keys_ref,
                      h_ref, xnt_ref, thr_ref, a_ref, s2_ref, b_ref):
    half = wo_ref.shape[0] // 2
    h = x_ref[...] + _dot(mla_ref[...], wo_ref[0:half, :]) + _dot(dsa_ref[...], wo_ref[half:, :])
    h_ref[...] = h
    xnt = _rms(h, g_ref[...]).T.astype(BF16)
    xnt_ref[...] = xnt
    tc = xnt.shape[1]
    r8 = lax.broadcasted_iota(I32, (SUBLANES, LANES), 0)
    for hh in range(PEER_HEADS):
        for p, ref in ((0, thr_ref), (1, s2_ref)):
            hp = hh * 2 + p
            qt = _dot(wpq_ref[hp * PEER_HALF:(hp + 1) * PEER_HALF, :], xnt)
            ref[hh] = _dot(keys_ref[hp], qt.astype(BF16))
        for lt in range(tc // LANES):
            ls = slice(lt * LANES, (lt + 1) * LANES)
            s1, s2 = thr_ref[hh, :, ls], s2_ref[hh, :, ls]
            sv1 = _extract_top(s1, _N_EXTRACT)
            sv2 = _extract_top(s2, _N_EXTRACT)
            sv2_16 = jnp.concatenate(sv2[:PEER_TOPK], axis=0)
            sv2_8 = sv2_16[:8]
            blocks = [sv1[0] + sv2_16]
            for r1 in range(1, 8):
                blocks.append(jnp.where(r8 < PEER_TOPK // (r1 + 1), sv1[r1] + sv2_8, -jnp.inf))
            blocks.append(jnp.concatenate(sv1[8:PEER_TOPK], axis=0) + sv2[0])
            extra = jnp.where(r8 == 0, sv1[0] + sv2[PEER_TOPK],
                              jnp.where(r8 == 1, sv1[PEER_TOPK] + sv2[0], -jnp.inf))
            blocks.append(extra)
            cand = _extract_top(jnp.concatenate(blocks, axis=0), _N_EXTRACT)
            m0 = sv1[0] + sv2[0]
            z = jnp.zeros_like(m0)
            for r in range(PEER_TOPK):
                z = z + jnp.exp(cand[r] - m0)
            c16, c17 = cand[PEER_TOPK - 1], cand[PEER_TOPK]
            tau = jnp.where(c17 == -jnp.inf, c16, 0.5 * (c16 + c17))
            thr_ref[hh, :, ls] = tau - s1
            a_ref[hh, :, ls] = jnp.exp(s1 - sv1[0]) / z * 0.5
            b_ref[hh, :, ls] = jnp.exp(s2 - sv2[0])


def _peer_prep(x2d, mla, dsa, wts, tc):
    n = x2d.shape[0]
    row = lambda i: (i, 0)
    const2 = lambda i: (0, 0)
    col3 = lambda i: (0, 0, i)
    gate_shape = jax.ShapeDtypeStruct((PEER_HEADS, PEER_NKEYS, n), F32)
    gate_spec = pl.BlockSpec((PEER_HEADS, PEER_NKEYS, tc), col3)
    mix = mla.shape[1]
    return pl.pallas_call(
        _peer_prep_kernel, grid=(n // tc,),
        in_specs=[pl.BlockSpec((tc, D_MODEL), row), pl.BlockSpec((tc, mix), row), pl.BlockSpec((tc, mix), row),
                  pl.BlockSpec(wts["w_out"].shape, const2), pl.BlockSpec((1, D_MODEL), const2),
                  pl.BlockSpec(wts["wpq_t"].shape, const2), pl.BlockSpec(wts["peer_keys"].shape, lambda i: (0, 0, 0))],
        out_specs=(pl.BlockSpec((tc, D_MODEL), row), pl.BlockSpec((D_MODEL, tc), lambda i: (0, i)),
                   gate_spec, gate_spec, gate_spec, gate_spec),
        out_shape=(jax.ShapeDtypeStruct((n, D_MODEL), F32), jax.ShapeDtypeStruct((D_MODEL, n), BF16),
                   gate_shape, gate_shape, gate_shape, gate_shape),
        compiler_params=_cparams(1), name="peer_prep",
    )(x2d, mla, dsa, wts["w_out"], wts["g_ffn"], wts["wpq_t"], wts["peer_keys"])


def _gelu_x2(x):
    return x * (1.0 + lax.erf(x * np.float32(math.sqrt(0.5))))


def _peer_chain_kernel(xnt_ref, thr_ref, a_ref, s2_ref, b_ref, u_ref, vt_ref, h_ref, gf_ref, y_ref, acc_scr, *, ni):
    e = pl.program_id(1)

    @pl.when(e == 0)
    def _():
        acc_scr[...] = jnp.zeros(acc_scr.shape, F32)

    tc = acc_scr.shape[1]
    tw = min(tc, MXU_N)
    per_slice = MXU_N // PEER_NKEYS
    chains = [(ks, k) for ks in range(ni // per_slice) for k in range(tc // tw)]
    n_r = D_MODEL // MXU_N

    def gates(ks, k):
        out = []
        for i2 in range(per_slice):
            i1 = e * ni + ks * per_slice + i2
            thr_rows = [thr_ref[hh, pl.ds(i1, 1), :] for hh in range(PEER_HEADS)]
            a_rows = [a_ref[hh, pl.ds(i1, 1), :] for hh in range(PEER_HEADS)]
            for lt in range(tw // LANES):
                ls = slice(k * tw + lt * LANES, k * tw + (lt + 1) * LANES)
                gate = None
                for hh in range(PEER_HEADS):
                    term = jnp.where(s2_ref[hh, :, ls] >= thr_rows[hh][:, ls], b_ref[hh, :, ls], 0.0)
                    term = term * a_rows[hh][:, ls]
                    gate = term if gate is None else gate + term
                out.append(gate)
        return out

    for ks, k in chains:
        es = slice(ks * MXU_N, (ks + 1) * MXU_N)
        cs = slice(k * tw, (k + 1) * tw)
        act = _gelu_x2(_dot(u_ref[es, :], xnt_ref[:, cs]))
        g_cur = gates(ks, k)
        n_lt = tw // LANES
        parts = []
        for i2 in range(per_slice):
            tiles = [(g_cur[i2 * n_lt + lt] * act[i2 * PEER_NKEYS:(i2 + 1) * PEER_NKEYS,
                                                   lt * LANES:(lt + 1) * LANES]).astype(BF16) for lt in range(n_lt)]
            parts.append(tiles[0] if n_lt == 1 else jnp.concatenate(tiles, axis=1))
        w = jnp.concatenate(parts, axis=0)
        for r in range(n_r):
            rs = slice(r * MXU_N, (r + 1) * MXU_N)
            acc_scr[rs, cs] += _dot(vt_ref[rs, es], w)

    @pl.when(e == pl.num_programs(1) - 1)
    def _():
        y_ref[...] = _rms(acc_scr[...].T + h_ref[...], gf_ref[...])


def _peer_chain(xnt, thr, a, s2, b, h, wts, tc, ni):
    n = h.shape[0]
    eb = ni * PEER_NKEYS
    gate_spec = pl.BlockSpec((PEER_HEADS, PEER_NKEYS, tc), lambda i, e: (0, 0, i))
    return pl.pallas_call(
        functools.partial(_peer_chain_kernel, ni=ni), grid=(n // tc, PEER_EXPERTS // eb),
        in_specs=[pl.BlockSpec((D_MODEL, tc), lambda i, e: (0, i)), gate_spec, gate_spec, gate_spec, gate_spec,
                  pl.BlockSpec((eb, D_MODEL), lambda i, e: (e, 0)), pl.BlockSpec((D_MODEL, eb), lambda i, e: (0, e)),
                  pl.BlockSpec((tc, D_MODEL), lambda i, e: (i, 0)), pl.BlockSpec((1, D_MODEL), lambda i, e: (0, 0))],
        out_specs=pl.BlockSpec((tc, D_MODEL), lambda i, e: (i, 0)),
        out_shape=jax.ShapeDtypeStruct((n, D_MODEL), F32),
        scratch_shapes=[pltpu.VMEM((D_MODEL, tc), F32)],
        compiler_params=_cparams(2), name="peer_main",
    )(xnt, thr, a, s2, b, wts["peer_u"], wts["peer_vt"], h, wts["g_final"])


def _peer(x2d, mla, dsa, wts):
    n = x2d.shape[0]
    tc = min(512, n)
    h, xnt, thr, a, s2, b = _peer_prep(x2d, mla, dsa, wts, tc)
    return _peer_chain(xnt, thr, a, s2, b, h, wts, tc, ni=16)


def _pad_cols(w, n):
    return jnp.pad(w, ((0, 0), (0, n - w.shape[1])))


def _swap_halves(w):
    half = w.shape[-1] // 2
    return jnp.concatenate([w[..., half:], w[..., :half]], axis=-1)


def _prep_weights(g_attn, w_in, g_q, w_uq, g_kv, w_uk, w_uv, w_out, g_ffn, w_pq, peer_keys, peer_u, peer_v, g_final):
    w_cq, w_ckv, w_kr = w_in[:, 0:384], w_in[:, 384:640], w_in[:, 640:672]
    w_qb, w_kb, w_vb = w_in[:, 672:1184], w_in[:, 1184:1312], w_in[:, 1312:1440]
    w_qi, w_ki, w_wi = w_in[:, 1440:1952], w_in[:, 1952:2016], w_in[:, 2016:2024]
    qi3 = w_qi.reshape(D_MODEL, IDX_HEADS, IDX_DIM)
    qi_partner = jnp.concatenate([_swap_halves(qi3[..., :IDX_ROPE]), jnp.zeros_like(qi3[..., IDX_ROPE:])], axis=-1)
    w1 = jnp.concatenate([
        w_cq, w_ckv, w_qb, w_kb, w_vb, w_qi, qi_partner.reshape(D_MODEL, IDX_HEADS * IDX_DIM),
        _pad_cols(w_kr, LANES), _pad_cols(_swap_halves(w_kr), LANES),
        _pad_cols(w_ki, LANES), _pad_cols(_swap_halves(w_ki[:, :IDX_ROPE]), LANES),
        _pad_cols(w_wi, LANES)], axis=1).astype(BF16)
    uq3 = w_uq.reshape(Q_LORA, MLA_HEADS, MLA_NOPE + MLA_ROPE)
    rope3 = uq3[..., MLA_NOPE:]
    pad3 = lambda w: jnp.pad(w, ((0, 0), (0, 0), (0, LANES - MLA_ROPE))).reshape(Q_LORA, MLA_HEADS * LANES)
    wuq = jnp.concatenate([uq3[..., :MLA_NOPE].reshape(Q_LORA, MLA_HEADS * MLA_NOPE),
                           pad3(rope3), pad3(_swap_halves(rope3))], axis=1).astype(BF16)
    ukt = jnp.transpose(w_uk, (1, 2, 0))
    zero = jnp.zeros((MLA_NOPE, KV_LORA), F32)
    wuk = jnp.stack([jnp.concatenate([jnp.concatenate([ukt[2 * p], zero], axis=1),
                                      jnp.concatenate([zero, ukt[2 * p + 1]], axis=1)], axis=0)
                     for p in range(MLA_HEADS // 2)]).astype(BF16)
    return dict(
        g_attn=g_attn.reshape(1, -1), w1=w1, g_q=g_q.reshape(1, -1), wuq=wuq, g_kv=g_kv.reshape(1, -1), wuk=wuk,
        wuv=jnp.transpose(w_uv, (1, 0, 2)).astype(BF16), w_out=w_out.astype(BF16), g_ffn=g_ffn.reshape(1, -1),
        wpq_t=w_pq.T.astype(BF16),
        peer_keys=peer_keys.reshape(PEER_HEADS * 2, PEER_NKEYS, PEER_HALF).astype(BF16),
        peer_u=peer_u.astype(BF16), peer_vt=peer_v.T.astype(BF16), g_final=g_final.reshape(1, -1))


def _rope_tables(pos):
    half = MLA_ROPE // 2
    inv = ROPE_THETA ** (-jnp.arange(half, dtype=F32) / half)
    ang = pos.astype(F32)[:, None] * inv
    cos, sin = jnp.cos(ang), jnp.sin(ang)
    c32 = jnp.concatenate([cos, cos], axis=1)
    s32 = jnp.concatenate([-sin, sin], axis=1)
    n = pos.shape[0]
    one, zero = jnp.ones((n, 32), F32), jnp.zeros((n, 32), F32)
    ca = jnp.concatenate([c32, one, one, one], axis=1)
    sa = jnp.concatenate([s32, zero, zero, zero], axis=1)
    cb = jnp.concatenate([c32, one, c32, one], axis=1)
    sb = jnp.concatenate([s32, zero, s32, zero], axis=1)
    return ca, sa, cb, sb


def _pick_tile(n, choices):
    for c in choices:
        if n % c == 0:
            return c
    raise ValueError(f"no tile in {choices} divides {n}")


def kernel(x_prompt, x_sample, cache_ckv, cache_krope, cache_k, cache_v, cache_kidx, page_table, rel_bias, g_attn,
           w_in, g_q, w_uq, g_kv, w_uk, w_uv, w_out, g_ffn, w_pq, peer_keys, peer_u, peer_v, g_final):
    assert g_attn.shape[0] == 1, "single-layer kernel"
    b, s, d = x_prompt.shape
    bs, ts, _ = x_sample.shape
    n_pages = page_table.shape[1]
    past = n_pages * PAGE
    assert s % MLA_CHUNK == 0 and s % KV_CHUNK == 0 and ts <= 8 and (bs * ts) % LANES == 0
    wts = _prep_weights(g_attn[0], w_in[0], g_q[0], w_uq[0], g_kv[0], w_uk[0], w_uv[0], w_out[0], g_ffn[0],
                        w_pq[0], peer_keys[0], peer_u[0], peer_v[0], g_final)
    bias_p, bias_s, bias_f = _bias_tables(rel_bias)

    xp = x_prompt.reshape(b * s, d)
    tm = _pick_tile(s, (512, 256))
    (ckv_p, kr_p, kb_p, vb_p, ki_p, kcat, kb_bf, vb_bf, ki_bf, qcat, qb_hm, qi_hm, wi_p) = _inproj(
        xp, _rope_tables(jnp.arange(s)), s, wts, tm)
    mla_p = _mla_prompt(qcat, kcat, wts["wuv"], b, s)
    v_t = jnp.transpose(vb_bf.reshape(b, s // KV_CHUNK, KV_CHUNK, LANES), (0, 1, 3, 2))
    dsa_p = _dsa_prompt(qi_hm, wi_p.T, ki_bf, qb_hm, kb_bf, v_t, bias_p, b, s)
    y_p = _peer(xp, mla_p, dsa_p, wts)

    ns = bs * ts
    xs = x_sample.reshape(ns, d)
    pos_s = past + jnp.tile(jnp.arange(ts), bs)
    (ckv_s, kr_s, kb_s, vb_s, ki_s, kcat_s, kb_sbf, vb_sbf, ki_sbf, qcat_s, qb_shm, qi_shm, wi_s) = _inproj(
        xs, _rope_tables(pos_s), ns, wts, ns)
    pt_flat = page_table.reshape(-1).astype(I32)
    pp = _pick_tile(n_pages, (64, 32, 16, 8, 4, 2, 1))
    q5 = qcat_s.reshape(MLA_HEADS, bs, ts, KCAT).transpose(1, 0, 2, 3).reshape(bs, MLA_HEADS * ts, KCAT)
    qi_s = qi_shm.reshape(IDX_HEADS, bs, ts, IDX_DIM).transpose(1, 2, 0, 3).reshape(bs, ts * IDX_HEADS, IDX_DIM)
    wi_col = wi_s.reshape(bs, ts * IDX_HEADS, 1)
    pad_new = lambda a: jnp.pad(a.reshape(bs, ts, a.shape[-1]), ((0, 0), (0, PAGE - ts), (0, 0)))
    pad_new_t = lambda a: jnp.swapaxes(pad_new(a), 1, 2)
    olat, sc_past, sc_new = _sample1(
        pt_flat, q5[..., :KV_LORA], q5[..., KV_LORA:KV_LORA + MLA_ROPE], qi_s, wi_col,
        pad_new(kcat_s[:, :KV_LORA]), pad_new_t(kcat_s[:, KV_LORA:KV_LORA + MLA_ROPE]), pad_new_t(ki_sbf),
        cache_ckv[0], jnp.swapaxes(cache_krope[0], 1, 2), jnp.swapaxes(cache_kidx[0], 1, 2), n_pages, pp)
    olat_hm = olat.reshape(bs, MLA_HEADS, ts, KV_LORA).transpose(1, 0, 2, 3).reshape(MLA_HEADS, ns, KV_LORA)
    mla_s = _mla_out(olat_hm.astype(BF16), wts["wuv"])
    topk_s = min(IDX_TOPK_MAX, (past + ts) // 4)
    kp = -(-(past + PAGE) // KV_CHUNK) * KV_CHUNK
    sc_all = jnp.concatenate([sc_past, sc_new], axis=2).reshape(ns, past + PAGE)
    sc_t = jnp.pad(sc_all, ((0, 0), (0, kp - past - PAGE)), constant_values=-jnp.inf).T
    tau_s, cut_s = _sample_select(sc_t, topk_s)
    qb_s = qb_shm.reshape(DSA_KV_HEADS, DSA_REP, bs, ts, DSA_HEAD_DIM).transpose(2, 0, 1, 3, 4)
    qb_s = qb_s.reshape(bs, DSA_KV_HEADS, DSA_REP * ts, DSA_HEAD_DIM)
    bias_s4 = bias_s[:, :, :ts, :].reshape(2, DSA_KV_HEADS, DSA_REP * ts, PAGE)
    bias_f4 = bias_f[:, :ts, :].reshape(DSA_KV_HEADS, DSA_REP * ts, PAGE)
    n_pool = cache_k.shape[1]
    page_t = lambda c: jnp.transpose(c[0], (0, 2, 3, 1)).reshape(n_pool, LANES, PAGE)
    o_s = _sample3(pt_flat, qb_s, sc_past, sc_new, tau_s.reshape(bs, ts, 1), cut_s.reshape(bs, ts, 1),
                   pad_new_t(kb_sbf), pad_new_t(vb_sbf), bias_s4, bias_f4,
                   page_t(cache_k), page_t(cache_v), n_pages, pp)
    dsa_s = o_s.reshape(bs, DSA_KV_HEADS, DSA_REP, ts, DSA_HEAD_DIM).transpose(0, 3, 1, 2, 4)
    dsa_s = dsa_s.reshape(ns, DSA_HEADS * DSA_HEAD_DIM).astype(BF16)
    y_s = _peer(xs, mla_s, dsa_s, wts)

    def rows(a_t, nb, nt):
        return jnp.swapaxes(a_t, 1, 2).reshape(1, nb, nt, a_t.shape[1])

    kv5 = lambda a: a.reshape(a.shape[:3] + (DSA_KV_HEADS, DSA_HEAD_DIM))
    return (y_p.reshape(b, s, d), y_s.reshape(bs, ts, d),
            ckv_p.reshape(1, b, s, KV_LORA), rows(kr_p, b, s), kv5(rows(kb_p, b, s)), kv5(rows(vb_p, b, s)),
            rows(ki_p, b, s),
            ckv_s.reshape(1, bs, ts, KV_LORA), rows(kr_s, bs, ts), kv5(rows(kb_s, bs, ts)), kv5(rows(vb_s, bs, ts)),
            rows(ki_s, bs, ts))
```
